```python
import jax
import jax.numpy as jnp
from jax import lax
import numpy as np

D_MODEL = 2048
BATCH = 2
SEQ = 4096
DEPTH = 4
DEC_BATCH = 16
DEC_SEQ = 16
PAST_LEN = 1024

CHUNK = 64
N_EVEN = (DEPTH + 1) // 2
N_ODD = DEPTH // 2
EPS = 1e-6
QBLOCK = 128

H_A = 8
QK_NOPE = 128
QK_ROPE = 64
V_A = 128
Q_LORA = 512
KV_LORA = 256
ROPE_THETA = 10000.0
COLS_A = Q_LORA + KV_LORA + QK_ROPE

H_B = 16
HD_B = 64
C_B = H_B * HD_B
W_LORA = 64
A_LORA = 64
G_LORA = 128
LNX_EPS = 64e-5
COLS_B = 3 * C_B + W_LORA + A_LORA + G_LORA

H_C = 16
HD_C = 64
C_C = H_C * HD_C
BAND_CHUNKS = 8
BAND = BAND_CHUNKS * CHUNK
REL_CLIP = 128

H_D = 4
DK_D = 128
DV_D = 256
GK_LORA = 16
GATE_NORM = 16.0
COLS_D = 2 * H_D * DK_D + 2 * H_D * DV_D + GK_LORA

D_FF = 5632
CONV_W = 3

COLS_EVEN = COLS_A + COLS_B
COLS_ODD = 3 * C_C + COLS_D
A_SPLITS = [Q_LORA, Q_LORA + KV_LORA]
B_SPLITS = [C_B, 2 * C_B, 3 * C_B, 3 * C_B + W_LORA, 3 * C_B + W_LORA + A_LORA]
ODD_SPLITS = [C_C, 2 * C_C, 3 * C_C, 3 * C_C + H_D * DK_D, 3 * C_C + 2 * H_D * DK_D,
              3 * C_C + 2 * H_D * DK_D + H_D * DV_D, 3 * C_C + 2 * H_D * DK_D + H_D * DV_D + GK_LORA]

kernel_name = "hybrid_streaming_encoder_step"


def rmsnorm(x, g):
    xf = x.astype(jnp.float32)
    y = xf * lax.rsqrt(jnp.mean(xf * xf, axis=-1, keepdims=True) + EPS)
    return (y * g).astype(x.dtype)


def rope(x, pos):
    half = QK_ROPE // 2
    inv = ROPE_THETA ** (-jnp.arange(half, dtype=jnp.float32) / half)
    ang = pos.astype(jnp.float32)[:, None] * inv[None, :]
    shape = (ang.shape[0],) + (1,) * (x.ndim - 3) + (half,)
    cos = jnp.cos(ang).reshape(shape).astype(x.dtype)
    sin = jnp.sin(ang).reshape(shape).astype(x.dtype)
    x1, x2 = x[..., :half], x[..., half:]
    return jnp.concatenate([x1 * cos - x2 * sin, x1 * sin + x2 * cos], axis=-1)


def mla_expand(ckv, w_ukv):
    B, S, _ = ckv.shape
    kv = (ckv @ w_ukv).reshape(B, S, H_A, QK_NOPE + V_A)
    return kv[..., :QK_NOPE], kv[..., QK_NOPE:]


def _mla_scores(q_nope, q_pe, k_nope, kpe):
    s = jnp.einsum("bqhd,bkhd->bhqk", q_nope, k_nope) + jnp.einsum("bqhr,bkr->bhqk", q_pe, kpe)
    return s.astype(jnp.float32) * (QK_NOPE + QK_ROPE) ** -0.5


def mla_attend_prompt(q_nope, q_pe, k_nope, kpe, v):
    B, T = q_nope.shape[:2]
    nb = T // QBLOCK
    qn = jnp.moveaxis(q_nope.reshape(B, nb, QBLOCK, H_A, QK_NOPE), 1, 0)
    qp = jnp.moveaxis(q_pe.reshape(B, nb, QBLOCK, H_A, QK_ROPE), 1, 0)
    k_chunk = jnp.arange(T) // CHUNK

    def block(args):
        qn_b, qp_b, bi = args
        q_chunk = (bi * QBLOCK + jnp.arange(QBLOCK)) // CHUNK
        s = _mla_scores(qn_b, qp_b, k_nope, kpe)
        s = jnp.where(k_chunk[None, :] <= q_chunk[:, None], s, -jnp.inf)
        p = jax.nn.softmax(s, axis=-1).astype(v.dtype)
        return jnp.einsum("bhqk,bkhd->bqhd", p, v)

    o = lax.map(block, (qn, qp, jnp.arange(nb)))
    return jnp.moveaxis(o, 0, 1).reshape(B, T, H_A * V_A)


def mla_attend_sample(q_nope, q_pe, k_nope, kpe, v):
    B, T = q_nope.shape[:2]
    p = jax.nn.softmax(_mla_scores(q_nope, q_pe, k_nope, kpe), axis=-1).astype(v.dtype)
    return jnp.einsum("bhqk,bkhd->bqhd", p, v).reshape(B, T, H_A * V_A)


def rwkv7_scan(r, w, k, v, a_vec, b_vec, S0):
    def step(S, inp):
        r_t, w_t, k_t, v_t, a_t, b_t = inp
        sa = jnp.einsum("bhij,bhj->bhi", S, a_t)
        S = S * w_t[:, :, None, :] + sa[..., None] * b_t[:, :, None, :] + v_t[..., None] * k_t[:, :, None, :]
        return S, jnp.einsum("bhij,bhj->bhi", S, r_t)

    xs = tuple(jnp.moveaxis(t, 1, 0) for t in (r, w, k, v, a_vec, b_vec))
    S, y = lax.scan(step, S0, xs)
    return jnp.moveaxis(y, 0, 1), S


def rwkv7_mix(pb, shift_prev, S0, prm):
    B, T, _ = pb.shape
    f32 = jnp.float32
    p_prev = jnp.concatenate([shift_prev.astype(pb.dtype)[:, None], pb[:, :-1]], axis=1)
    xs = pb + (p_prev - pb) * prm["mu"]
    r, k, v, xw, xa, xg = jnp.split(xs, B_SPLITS, axis=-1)
    w_log = -jax.nn.softplus(-(prm["w0"] + jnp.tanh(xw) @ prm["w_w2"])) - 0.5
    decay = jnp.exp(-jnp.exp(w_log.astype(f32)))
    a = jax.nn.sigmoid(prm["a0"] + xa @ prm["w_a2"])
    g = jax.nn.sigmoid(xg) @ prm["w_g2"]

    def heads(t):
        return t.reshape(B, T, H_B, HD_B).astype(f32)

    kk = heads(k * prm["k_k"])
    kk = kk / jnp.maximum(jnp.sqrt(jnp.sum(kk * kk, axis=-1, keepdims=True)), 1e-12)
    k = k * (1.0 + (a - 1.0) * prm["k_a"])
    rh, kh, vh, ah = heads(r), heads(k), heads(v), heads(a)
    y, S = rwkv7_scan(rh, heads(decay), kh, vh, -kk, kk * ah, S0.astype(f32))
    mean = jnp.mean(y, axis=-1, keepdims=True)
    var = jnp.mean(jnp.square(y - mean), axis=-1, keepdims=True)
    yn = ((y - mean) * lax.rsqrt(var + LNX_EPS)).reshape(B, T, C_B) * prm["lnx_w"] + prm["lnx_b"]
    bonus = (jnp.sum(rh * kh * prm["r_k"], axis=-1, keepdims=True) * vh).reshape(B, T, C_B)
    out = (yn + bonus) * g
    return out.astype(pb.dtype), S.astype(S0.dtype), pb[:, -1]


def even_mixer(h, pos, ckv_past, kpe_past, rwkv_S, rwkv_shift, prm):
    B, T, _ = h.shape
    proj = h @ prm["w_in"]
    pa, pb = proj[..., :COLS_A], proj[..., COLS_A:]
    cq, ckv, kpe = jnp.split(pa, A_SPLITS, axis=-1)
    q = (rmsnorm(cq, prm["q_norm"]) @ prm["w_uq"]).reshape(B, T, H_A, QK_NOPE + QK_ROPE)
    q_nope, q_pe = q[..., :QK_NOPE], rope(q[..., QK_NOPE:], pos)
    ckv = rmsnorm(ckv, prm["kv_norm"])
    kpe = rope(kpe, pos)
    if ckv_past is None:
        k_nope, v = mla_expand(ckv, prm["w_ukv"])
        o_a = mla_attend_prompt(q_nope, q_pe, k_nope, kpe, v)
    else:
        ckv_all = jnp.concatenate([ckv_past.astype(ckv.dtype), ckv], axis=1)
        kpe_all = jnp.concatenate([kpe_past.astype(kpe.dtype), kpe], axis=1)
        k_nope, v = mla_expand(ckv_all, prm["w_ukv"])
        o_a = mla_attend_sample(q_nope, q_pe, k_nope, kpe_all, v)
    o_b, S_new, shift_new = rwkv7_mix(pb, rwkv_shift, rwkv_S, prm)
    y = jnp.concatenate([o_a, o_b], axis=-1) @ prm["w_out"]
    return y, ckv, kpe, S_new, shift_new


def rel_bias_lookup(table, rel):
    idx = jnp.clip(rel, -REL_CLIP, REL_CLIP) + REL_CLIP
    return table[:, idx].astype(jnp.float32)


def band_attend_prompt(q, k, v, rel_bias, band_past):
    B, T = q.shape[:2]
    nc = T // CHUNK
    span = BAND + CHUNK
    kp = jnp.pad(k, ((0, 0), (BAND, 0), (0, 0), (0, 0)))
    vp = jnp.pad(v, ((0, 0), (BAND, 0), (0, 0), (0, 0)))
    qc = jnp.moveaxis(q.reshape(B, nc, CHUNK, H_C, HD_C), 1, 0)
    bias = rel_bias_lookup(rel_bias, jnp.arange(CHUNK)[:, None] + BAND - jnp.arange(span)[None, :])

    def one_chunk(args):
        q_c, c = args
        start = c * CHUNK
        k_c = lax.dynamic_slice_in_dim(kp, start, span, axis=1)
        v_c = lax.dynamic_slice_in_dim(vp, start, span, axis=1)
        s = jnp.einsum("bqhd,bkhd->bhqk", q_c, k_c).astype(jnp.float32) * HD_C ** -0.5 + bias
        k_pos = start - BAND + jnp.arange(span)
        s = jnp.where(k_pos >= 0, s, -jnp.inf)
        p = jax.nn.softmax(s, axis=-1).astype(v.dtype)
        return jnp.einsum("bhqk,bkhd->bqhd", p, v_c)

    o = lax.map(one_chunk, (qc, jnp.arange(nc)))
    return jnp.moveaxis(o, 0, 1).reshape(B, T, C_C), kp[:, -band_past:], vp[:, -band_past:]


def band_attend_sample(q, k, v, k_past, v_past, rel_bias):
    B, T = q.shape[:2]
    P = k_past.shape[1]
    k_all = jnp.concatenate([k_past.astype(k.dtype), k], axis=1)
    v_all = jnp.concatenate([v_past.astype(v.dtype), v], axis=1)
    q_pos = PAST_LEN + jnp.arange(T)
    k_pos = PAST_LEN - P + jnp.arange(P + T)
    bias = rel_bias_lookup(rel_bias, q_pos[:, None] - k_pos[None, :])
    s = jnp.einsum("bqhd,bkhd->bhqk", q, k_all).astype(jnp.float32) * HD_C ** -0.5 + bias
    p = jax.nn.softmax(s, axis=-1).astype(v.dtype)
    o = jnp.einsum("bhqk,bkhd->bqhd", p, v_all).reshape(B, T, C_C)
    return o, k_all[:, T:], v_all[:, T:]


def gla_mix(q, k, v, gk_in, g_out, S0, prm):
    B, T = q.shape[:2]
    f32 = jnp.float32
    L = min(CHUNK, T)
    nc = T // L
    log_a = jax.nn.log_sigmoid((gk_in @ prm["w_gk2"] + prm["b_gk"]).astype(f32)) / GATE_NORM

    def chunks(t, d):
        return jnp.moveaxis(t.astype(f32).reshape(B, nc, L, H_D, d), 1, 0)

    qs = chunks(q * DK_D ** -0.5, DK_D)
    ks = chunks(k, DK_D)
    vs = chunks(v, DV_D)
    gs = chunks(log_a, DK_D)
    causal = jnp.tril(jnp.ones((L, L), dtype=bool))

    def step(S, inp):
        qc, kc, vc, gc = inp
        b = jnp.cumsum(gc, axis=1)
        qe = qc * jnp.exp(b)
        ke = kc * jnp.exp(-b)
        att = jnp.where(causal, jnp.einsum("bthd,bshd->bhts", qe, ke), 0.0)
        o = jnp.einsum("bhts,bshv->bthv", att, vc) + jnp.einsum("bthd,bhdv->bthv", qe, S)
        b_end = b[:, -1]
        S = S * jnp.exp(b_end)[..., None] + jnp.einsum("bshd,bshv->bhdv", kc * jnp.exp(b_end[:, None] - b), vc)
        return S, o

    S, o = lax.scan(step, S0.astype(f32), (qs, ks, vs, gs))
    o = jnp.moveaxis(o, 0, 1).reshape(B, T, H_D, DV_D)
    o = o * lax.rsqrt(jnp.mean(o * o, axis=-1, keepdims=True) + EPS) * prm["gnorm"]
    out = o.reshape(B, T, H_D * DV_D) * jax.nn.silu(g_out.astype(f32))
    return out.astype(q.dtype), S.astype(S0.dtype)


def odd_mixer(h, band_k_past, band_v_past, gla_S, band_past, prm):
    B, T, _ = h.shape
    proj = h @ prm["w_in"]
    qc, kc, vc, qd, kd, vd, gk_in, g_out = jnp.split(proj, ODD_SPLITS, axis=-1)
    qc = qc.reshape(B, T, H_C, HD_C)
    kc = kc.reshape(B, T, H_C, HD_C)
    vc = vc.reshape(B, T, H_C, HD_C)
    if band_k_past is None:
        o_c, k_new, v_new = band_attend_prompt(qc, kc, vc, prm["rel_bias"], band_past)
    else:
        o_c, k_new, v_new = band_attend_sample(qc, kc, vc, band_k_past, band_v_past, prm["rel_bias"])
    o_d, S_new = gla_mix(qd.reshape(B, T, H_D, DK_D), kd.reshape(B, T, H_D, DK_D),
                         vd.reshape(B, T, H_D, DV_D), gk_in, g_out, gla_S, prm)
    y = jnp.concatenate([o_c, o_d], axis=-1) @ prm["w_out"]
    return y, k_new, v_new, S_new


def conv_ffn(h, buf, w_up, conv_w, conv_b, w_down):
    T = h.shape[1]
    u = h @ w_up
    ext = jnp.concatenate([buf.astype(u.dtype), u], axis=1)
    c = conv_b + conv_w[0] * ext[:, :T]
    for j in range(1, CONV_W):
        c = c + conv_w[j] * ext[:, j:j + T]
    gate, val = jnp.split(c, 2, axis=-1)
    return (jax.nn.silu(gate) * val) @ w_down, ext[:, T:]


def setup_inputs(seed: int = 0) -> dict:
    key = jax.random.key(seed)
    ks = iter(jax.random.split(key, 64))

    def nrm(shape, scale=1.0):
        return jax.random.normal(next(ks), shape, jnp.float32) * scale

    def unif(shape, lo, hi):
        return jax.random.uniform(next(ks), shape, jnp.float32, minval=lo, maxval=hi)

    band_past = min(BAND, PAST_LEN)
    return {
        "x_prompt": nrm((BATCH, SEQ, D_MODEL)),
        "x_sample": nrm((DEC_BATCH, DEC_SEQ, D_MODEL)),
        "cache_mla_ckv": nrm((N_EVEN, DEC_BATCH, PAST_LEN, KV_LORA)),
        "cache_mla_kpe": nrm((N_EVEN, DEC_BATCH, PAST_LEN, QK_ROPE)),
        "state_rwkv": nrm((N_EVEN, DEC_BATCH, H_B, HD_B, HD_B), 0.3),
        "state_rwkv_shift": nrm((N_EVEN, DEC_BATCH, COLS_B)),
        "cache_band_k": nrm((N_ODD, DEC_BATCH, band_past, H_C, HD_C)),
        "cache_band_v": nrm((N_ODD, DEC_BATCH, band_past, H_C, HD_C)),
        "state_gla": nrm((N_ODD, DEC_BATCH, H_D, DK_D, DV_D), 0.3),
        "state_ffn_conv": nrm((DEPTH, DEC_BATCH, CONV_W - 1, 2 * D_FF)),
        "norm_mix": 1.0 + nrm((DEPTH, D_MODEL), 0.05),
        "norm_ffn": 1.0 + nrm((DEPTH, D_MODEL), 0.05),
        "norm_final": 1.0 + nrm((D_MODEL,), 0.05),
        "ev_w_in": nrm((N_EVEN, D_MODEL, COLS_EVEN), D_MODEL ** -0.5),
        "ev_q_norm": 1.0 + nrm((N_EVEN, Q_LORA), 0.05),
        "ev_w_uq": nrm((N_EVEN, Q_LORA, H_A * (QK_NOPE + QK_ROPE)), Q_LORA ** -0.5),
        "ev_kv_norm": 1.0 + nrm((N_EVEN, KV_LORA), 0.05),
        "ev_w_ukv": nrm((N_EVEN, KV_LORA, H_A * (QK_NOPE + V_A)), KV_LORA ** -0.5),
        "ev_mu": unif((N_EVEN, COLS_B), 0.0, 1.0),
        "ev_w0": unif((N_EVEN, C_B), -2.0, 1.0),
        "ev_w_w2": nrm((N_EVEN, W_LORA, C_B), 0.1 * W_LORA ** -0.5),
        "ev_a0": nrm((N_EVEN, C_B), 0.5),
        "ev_w_a2": nrm((N_EVEN, A_LORA, C_B), 0.5 * A_LORA ** -0.5),
        "ev_w_g2": nrm((N_EVEN, G_LORA, C_B), G_LORA ** -0.5),
        "ev_k_k": 0.85 + nrm((N_EVEN, C_B), 0.05),
        "ev_k_a": 1.0 + nrm((N_EVEN, C_B), 0.05),
        "ev_r_k": nrm((N_EVEN, H_B, HD_B), 0.1),
        "ev_lnx_w": 1.0 + nrm((N_EVEN, C_B), 0.05),
        "ev_lnx_b": nrm((N_EVEN, C_B), 0.02),
        "ev_w_out": nrm((N_EVEN, H_A * V_A + C_B, D_MODEL), (H_A * V_A + C_B) ** -0.5),
        "od_w_in": nrm((N_ODD, D_MODEL, COLS_ODD), D_MODEL ** -0.5),
        "od_rel_bias": nrm((N_ODD, H_C, 2 * REL_CLIP + 1), 0.5),
        "od_w_gk2": nrm((N_ODD, GK_LORA, H_D * DK_D), GK_LORA ** -0.5),
        "od_b_gk": nrm((N_ODD, H_D * DK_D), 0.5),
        "od_gnorm": 1.0 + nrm((N_ODD, DV_D), 0.05),
        "od_w_out": nrm((N_ODD, C_C + H_D * DV_D, D_MODEL), (C_C + H_D * DV_D) ** -0.5),
        "ffn_w_up": nrm((DEPTH, D_MODEL, 2 * D_FF), D_MODEL ** -0.5),
        "ffn_conv_w": nrm((DEPTH, CONV_W, 2 * D_FF), CONV_W ** -0.5),
        "ffn_conv_b": nrm((DEPTH, 2 * D_FF), 0.02),
        "ffn_w_down": nrm((DEPTH, D_FF, D_MODEL), D_FF ** -0.5),
    }


def reference(x_prompt, x_sample, cache_mla_ckv, cache_mla_kpe, state_rwkv, state_rwkv_shift,
              cache_band_k, cache_band_v, state_gla, state_ffn_conv,
              norm_mix, norm_ffn, norm_final,
              ev_w_in, ev_q_norm, ev_w_uq, ev_kv_norm, ev_w_ukv, ev_mu, ev_w0, ev_w_w2, ev_a0,
              ev_w_a2, ev_w_g2, ev_k_k, ev_k_a, ev_r_k, ev_lnx_w, ev_lnx_b, ev_w_out,
              od_w_in, od_rel_bias, od_w_gk2, od_b_gk, od_gnorm, od_w_out,
              ffn_w_up, ffn_conv_w, ffn_conv_b, ffn_w_down):
    f32 = jnp.float32
    dt = x_prompt.dtype
    Bp, Tp = x_prompt.shape[:2]
    Bs, Ts = x_sample.shape[:2]
    band_past = cache_band_k.shape[2]
    pos_p = jnp.arange(Tp)
    pos_s = PAST_LEN + jnp.arange(Ts)
    xp, xs = x_prompt, x_sample

    ckv_p_l, ckv_s_l, kpe_p_l, kpe_s_l = [], [], [], []
    rw_p_l, rw_s_l, sh_p_l, sh_s_l = [], [], [], []
    bk_p_l, bk_s_l, bv_p_l, bv_s_l = [], [], [], []
    gla_p_l, gla_s_l, ffn_p_l, ffn_s_l = [], [], [], []

    for layer in range(DEPTH):
        h_p = rmsnorm(xp, norm_mix[layer])
        h_s = rmsnorm(xs, norm_mix[layer])
        if layer % 2 == 0:
            e = layer // 2
            prm = {"w_in": ev_w_in[e], "q_norm": ev_q_norm[e], "w_uq": ev_w_uq[e],
                   "kv_norm": ev_kv_norm[e], "w_ukv": ev_w_ukv[e], "mu": ev_mu[e], "w0": ev_w0[e],
                   "w_w2": ev_w_w2[e], "a0": ev_a0[e], "w_a2": ev_w_a2[e], "w_g2": ev_w_g2[e],
                   "k_k": ev_k_k[e], "k_a": ev_k_a[e], "r_k": ev_r_k[e], "lnx_w": ev_lnx_w[e],
                   "lnx_b": ev_lnx_b[e], "w_out": ev_w_out[e]}
            y_p, ckv_p, kpe_p, rw_p, sh_p = even_mixer(
                h_p, pos_p, None, None, jnp.zeros((Bp, H_B, HD_B, HD_B), f32),
                jnp.zeros((Bp, COLS_B), dt), prm)
            y_s, ckv_s, kpe_s, rw_s, sh_s = even_mixer(
                h_s, pos_s, cache_mla_ckv[e], cache_mla_kpe[e], state_rwkv[e], state_rwkv_shift[e], prm)
            ckv_p_l.append(ckv_p); ckv_s_l.append(ckv_s)
            kpe_p_l.append(kpe_p); kpe_s_l.append(kpe_s)
            rw_p_l.append(rw_p); rw_s_l.append(rw_s)
            sh_p_l.append(sh_p); sh_s_l.append(sh_s)
        else:
            o = layer // 2
            prm = {"w_in": od_w_in[o], "rel_bias": od_rel_bias[o], "w_gk2": od_w_gk2[o],
                   "b_gk": od_b_gk[o], "gnorm": od_gnorm[o], "w_out": od_w_out[o]}
            y_p, bk_p, bv_p, g_p = odd_mixer(
                h_p, None, None, jnp.zeros((Bp, H_D, DK_D, DV_D), f32), band_past, prm)
            y_s, bk_s, bv_s, g_s = odd_mixer(
                h_s, cache_band_k[o], cache_band_v[o], state_gla[o], band_past, prm)
            bk_p_l.append(bk_p); bk_s_l.append(bk_s)
            bv_p_l.append(bv_p); bv_s_l.append(bv_s)
            gla_p_l.append(g_p); gla_s_l.append(g_s)
        xp = xp + y_p
        xs = xs + y_s
        f_p, buf_p = conv_ffn(rmsnorm(xp, norm_ffn[layer]), jnp.zeros((Bp, CONV_W - 1, 2 * D_FF), dt),
                              ffn_w_up[layer], ffn_conv_w[layer], ffn_conv_b[layer], ffn_w_down[layer])
        f_s, buf_s = conv_ffn(rmsnorm(xs, norm_ffn[layer]), state_ffn_conv[layer],
                              ffn_w_up[layer], ffn_conv_w[layer], ffn_conv_b[layer], ffn_w_down[layer])
        xp = xp + f_p
        xs = xs + f_s
        ffn_p_l.append(buf_p); ffn_s_l.append(buf_s)

    y_prompt = rmsnorm(xp, norm_final)
    y_sample = rmsnorm(xs, norm_final)
    return (y_prompt, y_sample,
            jnp.stack(ckv_p_l), jnp.stack(ckv_s_l),
            jnp.stack(kpe_p_l), jnp.stack(kpe_s_l),
            jnp.stack(rw_p_l), jnp.stack(rw_s_l),
            jnp.stack(sh_p_l), jnp.stack(sh_s_l),
            jnp.stack(bk_p_l), jnp.stack(bk_s_l),
            jnp.stack(bv_p_l), jnp.stack(bv_s_l),
            jnp.stack(gla_p_l), jnp.stack(gla_s_l),
            jnp.stack(ffn_p_l), jnp.stack(ffn_s_l))
```

```python
import functools

import jax
import jax.numpy as jnp
from jax import lax
from jax.experimental import pallas as pl
from jax.experimental.pallas import tpu as pltpu

F32 = jnp.float32
BF16 = jnp.bfloat16

D_MODEL = 2048
CHUNK = 64
EPS = 1e-6
PAST_LEN = 1024

H_A = 8
QK_NOPE = 128
QK_ROPE = 64
V_A = 128
Q_LORA = 512
KV_LORA = 256
ROPE_THETA = 10000.0

H_B = 16
HD_B = 64
C_B = H_B * HD_B
W_LORA = 64
A_LORA = 64
G_LORA = 128
LNX_EPS = 64e-5

H_C = 16
HD_C = 64
C_C = H_C * HD_C
BAND = 8 * CHUNK
REL_CLIP = 128

H_D = 4
DK_D = 128
DV_D = 256
GK_LORA = 16
GATE_NORM = 16.0

D_FF = 5632
CONV_W = 3

LANES = 128
VMEM_LIMIT_BYTES = 56 * 2 ** 20
NEG_BIG = -1e30
ROW_TILE = 256
MLA_SCALE = (QK_NOPE + QK_ROPE) ** -0.5
HI = lax.Precision.HIGHEST


def _cparams(*sem):
    return pltpu.CompilerParams(dimension_semantics=sem, vmem_limit_bytes=VMEM_LIMIT_BYTES)


def _resident(shape, index_map):
    return pl.BlockSpec(shape, index_map, pipeline_mode=pl.Buffered(1))


def _dot(a, b):
    return jnp.dot(a.astype(BF16), b.astype(BF16), preferred_element_type=F32)


def _dot_nt(a, b):
    return lax.dot_general(a.astype(BF16), b.astype(BF16), (((1,), (1,)), ((), ())),
                           preferred_element_type=F32)


def _dot_tn(a, b):
    return lax.dot_general(a.astype(BF16), b.astype(BF16), (((0,), (0,)), ((), ())),
                           preferred_element_type=F32)


def _dot_hi(a, b):
    return jnp.dot(a, b, precision=HI, preferred_element_type=F32)


def _dot_nt_hi(a, b):
    return lax.dot_general(a, b, (((1,), (1,)), ((), ())), precision=HI, preferred_element_type=F32)


def _dot_tn_hi(a, b):
    return lax.dot_general(a, b, (((0,), (0,)), ((), ())), precision=HI, preferred_element_type=F32)


def _rms(x, g):
    return x * lax.rsqrt(jnp.mean(x * x, axis=-1, keepdims=True) + EPS) * g


def _sigmoid(x):
    return 1.0 / (1.0 + jnp.exp(-x))


def _softplus(x):
    return jnp.maximum(x, 0.0) + jnp.log(1.0 + jnp.exp(-jnp.abs(x)))


def _cumsum_rows(x):
    n = x.shape[0]
    row = lax.broadcasted_iota(jnp.int32, x.shape, 0)
    s = 1
    while s < n:
        x = x + jnp.where(row >= s, pltpu.roll(x, s, 0), 0.0)
        s *= 2
    return x


def _norm_matmul_kernel(x_ref, g_ref, w_ref, o_ref):
    o_ref[...] = _dot(_rms(x_ref[...], g_ref[...]), w_ref[...])


def norm_matmul(x, g, w):
    m, k = x.shape
    n = w.shape[1]
    tm = ROW_TILE
    return pl.pallas_call(
        _norm_matmul_kernel,
        grid=(m // tm,),
        in_specs=[pl.BlockSpec((tm, k), lambda i: (i, 0)),
                  pl.BlockSpec((1, k), lambda i: (0, 0)),
                  _resident((k, n), lambda i: (0, 0))],
        out_specs=pl.BlockSpec((tm, n), lambda i: (i, 0)),
        out_shape=jax.ShapeDtypeStruct((m, n), F32),
        compiler_params=_cparams("parallel"),
        name="norm_matmul",
    )(x, g.reshape(1, k), w)


def _out_proj_kernel(res_ref, a1_ref, a2_ref, w1_ref, w2_ref, o_ref):
    o_ref[...] = res_ref[...] + _dot(a1_ref[...], w1_ref[...]) + _dot(a2_ref[...], w2_ref[...])


def out_proj(res, a1, a2, w1, w2):
    m, n = res.shape
    k1, k2 = a1.shape[1], a2.shape[1]
    tm = ROW_TILE
    return pl.pallas_call(
        _out_proj_kernel,
        grid=(m // tm,),
        in_specs=[pl.BlockSpec((tm, n), lambda i: (i, 0)),
                  pl.BlockSpec((tm, k1), lambda i: (i, 0)),
                  pl.BlockSpec((tm, k2), lambda i: (i, 0)),
                  _resident((k1, n), lambda i: (0, 0)),
                  _resident((k2, n), lambda i: (0, 0))],
        out_specs=pl.BlockSpec((tm, n), lambda i: (i, 0)),
        out_shape=jax.ShapeDtypeStruct((m, n), F32),
        compiler_params=_cparams("parallel"),
        name="out_proj",
    )(res, a1, a2, w1, w2)


def _final_norm_kernel(x_ref, g_ref, o_ref):
    o_ref[...] = _rms(x_ref[...], g_ref[...])


def final_norm(x, g):
    m, k = x.shape
    tm = ROW_TILE
    return pl.pallas_call(
        _final_norm_kernel,
        grid=(m // tm,),
        in_specs=[pl.BlockSpec((tm, k), lambda i: (i, 0)), pl.BlockSpec((1, k), lambda i: (0, 0))],
        out_specs=pl.BlockSpec((tm, k), lambda i: (i, 0)),
        out_shape=jax.ShapeDtypeStruct((m, k), F32),
        compiler_params=_cparams("parallel"),
        name="final_norm",
    )(x, g.reshape(1, k))


COL_CQ = 3 * C_B
COL_CKV = COL_CQ + Q_LORA
COL_KPE = COL_CKV + KV_LORA
COL_WA = COL_KPE + 2 * QK_ROPE
COL_G = COL_WA + W_LORA + A_LORA
COLS_EVEN_EXT = COL_G + G_LORA
HEAD_W = 2 * LANES


def _mla_prep_kernel(cq_ref, ckv_ref, kpe_ref, cos_ref, sin_ref, qn_ref, kvn_ref, wq_ref, wkv_ref,
                     q_ref, kx_ref, v_ref, ckv_out_ref, kpe_out_ref):
    cos = cos_ref[...]
    sin = sin_ref[...]
    z = _dot(_rms(cq_ref[...], qn_ref[...]), wq_ref[...])
    ckvn = _rms(ckv_ref[...], kvn_ref[...])
    ckv_out_ref[...] = ckvn
    kp = kpe_ref[...]
    kr = kp * cos + pltpu.roll(kp, QK_ROPE, 1) * sin
    kpe_out_ref[...] = kr[:, :QK_ROPE]
    kr16 = kr.astype(BF16)
    kv = _dot(ckvn, wkv_ref[...])
    for h in range(H_A):
        c0 = h * HEAD_W
        q_ref[h, :, 0:LANES] = (z[:, c0:c0 + LANES] * MLA_SCALE).astype(BF16)
        t2 = z[:, c0 + LANES:c0 + HEAD_W]
        q_ref[h, :, LANES:HEAD_W] = ((t2 * cos + pltpu.roll(t2, QK_ROPE, 1) * sin) * MLA_SCALE).astype(BF16)
        kx_ref[h, :, 0:LANES] = kv[:, c0:c0 + LANES].astype(BF16)
        kx_ref[h, :, LANES:HEAD_W] = kr16
        v_ref[h] = kv[:, c0 + LANES:c0 + HEAD_W].astype(BF16)


def mla_prep(proj, cos, sin, q_norm, kv_norm, wq, wkv):
    m = proj.shape[0]
    tm = ROW_TILE
    return pl.pallas_call(
        _mla_prep_kernel,
        grid=(m // tm,),
        in_specs=[pl.BlockSpec((tm, Q_LORA), lambda i: (i, COL_CQ // Q_LORA)),
                  pl.BlockSpec((tm, KV_LORA), lambda i: (i, COL_CKV // KV_LORA)),
                  pl.BlockSpec((tm, LANES), lambda i: (i, COL_KPE // LANES)),
                  pl.BlockSpec((tm, LANES), lambda i: (i, 0)),
                  pl.BlockSpec((tm, LANES), lambda i: (i, 0)),
                  pl.BlockSpec((1, Q_LORA), lambda i: (0, 0)),
                  pl.BlockSpec((1, KV_LORA), lambda i: (0, 0)),
                  _resident((Q_LORA, H_A * HEAD_W), lambda i: (0, 0)),
                  _resident((KV_LORA, H_A * HEAD_W), lambda i: (0, 0))],
        out_specs=[pl.BlockSpec((H_A, tm, HEAD_W), lambda i: (0, i, 0)),
                   pl.BlockSpec((H_A, tm, HEAD_W), lambda i: (0, i, 0)),
                   pl.BlockSpec((H_A, tm, V_A), lambda i: (0, i, 0)),
                   pl.BlockSpec((tm, KV_LORA), lambda i: (i, 0)),
                   pl.BlockSpec((tm, QK_ROPE), lambda i: (i, 0))],
        out_shape=[jax.ShapeDtypeStruct((H_A, m, HEAD_W), BF16),
                   jax.ShapeDtypeStruct((H_A, m, HEAD_W), BF16),
                   jax.ShapeDtypeStruct((H_A, m, V_A), BF16),
                   jax.ShapeDtypeStruct((m, KV_LORA), F32),
                   jax.ShapeDtypeStruct((m, QK_ROPE), F32)],
        compiler_params=_cparams("parallel"),
        name="mla_prep",
    )(proj, proj, proj, cos, sin, q_norm.reshape(1, -1), kv_norm.reshape(1, -1), wq, wkv)


ATT_BLOCK = 256


def _mla_prompt_kernel(q_ref, k_ref, v_ref, o_ref):
    t = ATT_BLOCK
    qi = pl.program_id(2)
    q = q_ref[0]

    def block(j, carry, masked):
        m, l, acc = carry
        start = pl.multiple_of(j * t, t)
        k = k_ref[0, pl.ds(start, t), :]
        v = v_ref[0, pl.ds(start, t), :]
        s = lax.dot_general(q, k, (((1,), (1,)), ((), ())), preferred_element_type=F32)
        if masked:
            qc = lax.broadcasted_iota(jnp.int32, (t, t), 0) // CHUNK
            kc = lax.broadcasted_iota(jnp.int32, (t, t), 1) // CHUNK
            s = jnp.where(kc <= qc, s, NEG_BIG)
        m_new = jnp.maximum(m, jnp.max(s, axis=-1, keepdims=True))
        alpha = jnp.exp(m - m_new)
        p = jnp.exp(s - m_new)
        l = alpha * l + jnp.sum(p, axis=-1, keepdims=True)
        acc = alpha * acc + jnp.dot(p.astype(BF16), v, preferred_element_type=F32)
        return m_new, l, acc

    init = (jnp.full((t, 1), NEG_BIG, F32), jnp.zeros((t, 1), F32), jnp.zeros((t, V_A), F32))
    carry = lax.fori_loop(0, qi, lambda j, c: block(j, c, False), init)
    m, l, acc = block(qi, carry, True)
    o_ref[...] = (acc / l).astype(BF16)


def mla_prompt(q, kx, v, batch, seq):
    t = ATT_BLOCK
    nq = seq // t
    return pl.pallas_call(
        _mla_prompt_kernel,
        grid=(batch, H_A, nq),
        in_specs=[pl.BlockSpec((1, t, HEAD_W), lambda b, h, i: (h, b * nq + i, 0)),
                  pl.BlockSpec((1, seq, HEAD_W), lambda b, h, i: (h, b, 0)),
                  pl.BlockSpec((1, seq, V_A), lambda b, h, i: (h, b, 0))],
        out_specs=pl.BlockSpec((t, V_A), lambda b, h, i: (b * nq + i, h)),
        out_shape=jax.ShapeDtypeStruct((batch * seq, H_A * V_A), BF16),
        compiler_params=_cparams("parallel", "parallel", "arbitrary"),
        name="mla_prompt",
    )(q, kx, v)


def _mla_sample_kernel(q_ref, ckvp_ref, kpep_ref, ckvn_ref, kpen_ref, wkv_ref, o_ref):
    ckvp = ckvp_ref[0].astype(BF16)
    ckvn = ckvn_ref[...].astype(BF16)
    kpep = kpep_ref[0].astype(BF16)
    kpen = kpen_ref[...].astype(BF16)
    for h in range(H_A):
        c0 = h * HEAD_W
        w = wkv_ref[:, c0:c0 + HEAD_W]
        kvp = jnp.dot(ckvp, w, preferred_element_type=F32)
        kvn = jnp.dot(ckvn, w, preferred_element_type=F32)
        qn = q_ref[h, :, 0:QK_NOPE]
        qp = q_ref[h, :, LANES:LANES + QK_ROPE]
        s_p = _dot_nt(qn, kvp[:, :QK_NOPE]) + _dot_nt(qp, kpep)
        s_n = _dot_nt(qn, kvn[:, :QK_NOPE]) + _dot_nt(qp, kpen)
        m = jnp.maximum(jnp.max(s_p, axis=-1, keepdims=True), jnp.max(s_n, axis=-1, keepdims=True))
        p_p = jnp.exp(s_p - m)
        p_n = jnp.exp(s_n - m)
        l = jnp.sum(p_p, axis=-1, keepdims=True) + jnp.sum(p_n, axis=-1, keepdims=True)
        o = _dot(p_p, kvp[:, QK_NOPE:]) + _dot(p_n, kvn[:, QK_NOPE:])
        o_ref[:, h * V_A:(h + 1) * V_A] = (o / l).astype(BF16)


def mla_sample(q, ckv_past, kpe_past, ckv_new, kpe_new, wkv, batch, seq):
    past = ckv_past.shape[1]
    return pl.pallas_call(
        _mla_sample_kernel,
        grid=(batch,),
        in_specs=[pl.BlockSpec((H_A, seq, HEAD_W), lambda b: (0, b, 0)),
                  pl.BlockSpec((1, past, KV_LORA), lambda b: (b, 0, 0)),
                  pl.BlockSpec((1, past, QK_ROPE), lambda b: (b, 0, 0)),
                  pl.BlockSpec((seq, KV_LORA), lambda b: (b, 0)),
                  pl.BlockSpec((seq, QK_ROPE), lambda b: (b, 0)),
                  _resident((KV_LORA, H_A * HEAD_W), lambda b: (0, 0))],
        out_specs=pl.BlockSpec((seq, H_A * V_A), lambda b: (b, 0)),
        out_shape=jax.ShapeDtypeStruct((batch * seq, H_A * V_A), BF16),
        compiler_params=_cparams("parallel"),
        name="mla_sample",
    )(q, ckv_past, kpe_past, ckv_new, kpe_new, wkv)


def _rwkv_prep_kernel(r_ref, k_ref, v_ref, wa_ref, g_ref, rp_ref, kp_ref, vp_ref, wap_ref, gp_ref,
                      mu_r_ref, mu_k_ref, mu_v_ref, mu_wa_ref, mu_g_ref,
                      w0_ref, ww2_ref, a0_ref, wa2_ref, wg2_ref, kk_ref, ka_ref,
                      r_out, lw_out, k_out, v_out, kk_out, a_out, g_out):
    def mix(p_ref, pp_ref, mu_ref):
        p = p_ref[...]
        return p + (pp_ref[...] - p) * mu_ref[...]

    r_out[...] = mix(r_ref, rp_ref, mu_r_ref)
    v_out[...] = mix(v_ref, vp_ref, mu_v_ref)
    k = mix(k_ref, kp_ref, mu_k_ref)
    xwa = mix(wa_ref, wap_ref, mu_wa_ref)
    xg = mix(g_ref, gp_ref, mu_g_ref)
    w_log = -_softplus(-(w0_ref[...] + _dot(jnp.tanh(xwa), ww2_ref[...]))) - 0.5
    lw_out[...] = -jnp.exp(w_log)
    a = _sigmoid(a0_ref[...] + _dot(xwa, wa2_ref[...]))
    a_out[...] = a
    g_out[...] = _dot(_sigmoid(xg), wg2_ref[...])
    kk_out[...] = k * kk_ref[...]
    k_out[...] = k * (1.0 + (a - 1.0) * ka_ref[...])


def rwkv_prep(proj, pprev, mu_parts, w0, ww2p, a0, wa2p, wg2, k_k, k_a):
    m = proj.shape[0]
    tm = ROW_TILE
    wide = lambda c: pl.BlockSpec((tm, C_B), lambda i: (i, c))
    narrow = lambda c: pl.BlockSpec((tm, LANES), lambda i: (i, c))
    vec = lambda n: pl.BlockSpec((1, n), lambda i: (0, 0))
    out = jax.ShapeDtypeStruct((m, C_B), F32)
    return pl.pallas_call(
        _rwkv_prep_kernel,
        grid=(m // tm,),
        in_specs=[wide(0), wide(1), wide(2), narrow(COL_WA // LANES), narrow(COL_G // LANES),
                  wide(0), wide(1), wide(2), narrow(3 * C_B // LANES), narrow(3 * C_B // LANES + 1),
                  vec(C_B), vec(C_B), vec(C_B), vec(LANES), vec(LANES),
                  vec(C_B), _resident((LANES, C_B), lambda i: (0, 0)),
                  vec(C_B), _resident((LANES, C_B), lambda i: (0, 0)),
                  _resident((G_LORA, C_B), lambda i: (0, 0)), vec(C_B), vec(C_B)],
        out_specs=[pl.BlockSpec((tm, C_B), lambda i: (i, 0))] * 7,
        out_shape=[out] * 7,
        compiler_params=_cparams("parallel"),
        name="rwkv_prep",
    )(proj, proj, proj, proj, proj, pprev, pprev, pprev, pprev, pprev,
      *mu_parts, w0.reshape(1, -1), ww2p, a0.reshape(1, -1), wa2p, wg2,
      k_k.reshape(1, -1), k_a.reshape(1, -1))


def _tri_inverse(n_mat, size):
    eye = (lax.broadcasted_iota(jnp.int32, (size, size), 0)
           == lax.broadcasted_iota(jnp.int32, (size, size), 1)).astype(F32)
    t = eye + n_mat
    p = n_mat
    k = 1
    while 2 * k < size:
        p = _dot_hi(p, p)
        t = t + _dot_hi(t, p)
        k *= 2
    return t


def _rwkv_chunk_kernel(r_ref, lw_ref, k_ref, v_ref, kk_ref, a_ref, g_ref, rk_ref, lnw_ref, lnb_ref,
                       m0_ref, o_ref, mout_ref, m_scr, *, chunk, pairs):
    c = pl.program_id(2)
    nc = pl.num_programs(2)
    L = chunk

    @pl.when(c == 0)
    def _():
        m_scr[...] = m0_ref[0]

    lane = lax.broadcasted_iota(jnp.int32, (L, LANES), 1)
    low = lane < HD_B
    row_l = lax.broadcasted_iota(jnp.int32, (L, L), 0)
    col_l = lax.broadcasted_iota(jnp.int32, (L, L), 1)
    strict = row_l > col_l
    incl = row_l >= col_l
    r128 = lax.broadcasted_iota(jnp.int32, (LANES, LANES), 0)
    c128 = lax.broadcasted_iota(jnp.int32, (LANES, LANES), 1)
    same_head = (r128 < HD_B) == (c128 < HD_B)
    diag = r128 == c128

    def seg_sum(x):
        s0 = jnp.sum(jnp.where(low, x, 0.0), axis=-1, keepdims=True)
        s1 = jnp.sum(jnp.where(low, 0.0, x), axis=-1, keepdims=True)
        return jnp.where(low, s0, s1)

    for p in range(pairs):
        sl = slice(p * LANES, (p + 1) * LANES)
        r = r_ref[:, sl]
        lw = lw_ref[:, sl]
        k = k_ref[:, sl]
        v = v_ref[:, sl]
        kkr = kk_ref[:, sl]
        a = a_ref[:, sl]

        kk = kkr / jnp.maximum(jnp.sqrt(seg_sum(kkr * kkr)), 1e-12)
        avec = -kk
        bvec = kk * a
        ci = _cumsum_rows(lw)
        ce = ci - lw
        c_end = ci[L - 1:L, :]
        e_neg = jnp.exp(-ci)
        e_end = jnp.exp(c_end - ci)
        at = avec * jnp.exp(ce)
        bt = bvec * e_neg
        kt = k * e_neg
        rt = r * jnp.exp(ci)
        bh = bvec * e_end
        kh = k * e_end

        wa = None
        uv = None
        q = None
        yv = None
        for hh in range(2):
            msk = low if hh == 0 else jnp.logical_not(low)
            at_h = jnp.where(msk, at, 0.0)
            rt_h = jnp.where(msk, rt, 0.0)
            nab = jnp.where(strict, _dot_nt_hi(at_h, bt), 0.0)
            nak = jnp.where(strict, _dot_nt_hi(at_h, kt), 0.0)
            lrb = jnp.where(incl, _dot_nt_hi(rt_h, bt), 0.0)
            lrk = jnp.where(incl, _dot_nt_hi(rt_h, kt), 0.0)
            t_inv = _tri_inverse(nab, L)
            wa_h = _dot_hi(t_inv, at)
            uv_h = _dot_hi(t_inv, _dot_hi(nak, v))
            q_h = rt + _dot_hi(lrb, wa_h)
            yv_h = _dot_hi(lrb, uv_h) + _dot_hi(lrk, v)
            if hh == 0:
                wa, uv, q, yv = wa_h, uv_h, q_h, yv_h
            else:
                wa = jnp.where(low, wa, wa_h)
                uv = jnp.where(low, uv, uv_h)
                q = jnp.where(low, q, q_h)
                yv = jnp.where(low, yv, yv_h)

        p_end = jnp.broadcast_to(jnp.exp(c_end), (LANES, LANES))
        gmat = jnp.where(diag, p_end, 0.0) + jnp.where(same_head, _dot_tn_hi(bh, wa), 0.0)
        hmat = jnp.where(same_head, _dot_tn_hi(bh, uv) + _dot_tn_hi(kh, v), 0.0)
        m_prev = m_scr[p]
        y = _dot_hi(q, m_prev) + yv
        m_scr[p] = _dot_hi(gmat, m_prev) + hmat

        mean = seg_sum(y) * (1.0 / HD_B)
        d = y - mean
        var = seg_sum(d * d) * (1.0 / HD_B)
        yn = d * lax.rsqrt(var + LNX_EPS) * lnw_ref[:, sl] + lnb_ref[:, sl]
        bonus = seg_sum(r * k * rk_ref[:, sl]) * v
        o_ref[:, sl] = ((yn + bonus) * g_ref[:, sl]).astype(BF16)

    @pl.when(c == nc - 1)
    def _():
        mout_ref[0] = m_scr[...]


def rwkv_chunk(r, lw, k, v, kk, a, g, r_k, lnx_w, lnx_b, m0, row0, batch, seq, chunk, pairs):
    nc = seq // chunk
    npair = C_B // LANES
    ngrp = npair // pairs
    w = pairs * LANES
    blk0 = row0 // chunk
    row = lambda b, pg, c: (blk0 + b * nc + c, pg)
    par = lambda b, pg, c: (0, pg)
    act = pl.BlockSpec((chunk, w), row)
    prm = pl.BlockSpec((1, w), par)
    st = pl.BlockSpec((1, pairs, LANES, LANES), lambda b, pg, c: (b, pg, 0, 0))
    return pl.pallas_call(
        functools.partial(_rwkv_chunk_kernel, chunk=chunk, pairs=pairs),
        grid=(batch, ngrp, nc),
        in_specs=[act] * 7 + [prm] * 3 + [st],
        out_specs=[pl.BlockSpec((chunk, w), lambda b, pg, c: (b * nc + c, pg)), st],
        out_shape=[jax.ShapeDtypeStruct((batch * seq, C_B), BF16),
                   jax.ShapeDtypeStruct((batch, npair, LANES, LANES), F32)],
        scratch_shapes=[pltpu.VMEM((pairs, LANES, LANES), F32)],
        compiler_params=_cparams("parallel", "parallel", "arbitrary"),
        name="rwkv_chunk",
    )(r, lw, k, v, kk, a, g, r_k.reshape(1, -1), lnx_w.reshape(1, -1), lnx_b.reshape(1, -1), m0)


BAND_Q = 256
BAND_SEGS = BAND // BAND_Q + 1


def _band_prompt_kernel(q_ref, k_ref, v_ref, bias_ref, o_ref):
    qb = pl.program_id(2)
    t = BAND_Q
    q = q_ref[...] * HD_C ** -0.5
    lane = lax.broadcasted_iota(jnp.int32, (t, LANES), 1)
    low = lane < HD_C
    ks, vs, valid = [], [], []
    for seg in range(BAND_SEGS):
        kb = qb - (BAND_SEGS - 1) + seg
        start = pl.multiple_of(jnp.maximum(kb, 0) * t, t)
        ks.append(k_ref[pl.ds(start, t), :].astype(BF16))
        vs.append(v_ref[pl.ds(start, t), :].astype(BF16))
        valid.append(kb >= 0)
    outs = []
    for hh in range(2):
        qh = jnp.where(low if hh == 0 else jnp.logical_not(low), q, 0.0).astype(BF16)
        s = []
        for seg in range(BAND_SEGS):
            sv = lax.dot_general(qh, ks[seg], (((1,), (1,)), ((), ())), preferred_element_type=F32)
            sv = sv + bias_ref[hh, :, seg * t:(seg + 1) * t]
            s.append(jnp.where(valid[seg], sv, NEG_BIG))
        m = jnp.max(s[0], axis=-1, keepdims=True)
        for seg in range(1, BAND_SEGS):
            m = jnp.maximum(m, jnp.max(s[seg], axis=-1, keepdims=True))
        l = jnp.zeros((t, 1), F32)
        acc = jnp.zeros((t, LANES), F32)
        for seg in range(BAND_SEGS):
            p = jnp.exp(s[seg] - m)
            l = l + jnp.sum(p, axis=-1, keepdims=True)
            acc = acc + jnp.dot(p.astype(BF16), vs[seg], preferred_element_type=F32)
        outs.append(acc / l)
    o_ref[...] = jnp.where(low, outs[0], outs[1]).astype(BF16)


def band_prompt(proj, bias, batch, seq):
    t = BAND_Q
    nq = seq // t
    npair = C_C // LANES
    return pl.pallas_call(
        _band_prompt_kernel,
        grid=(npair, batch, nq),
        in_specs=[pl.BlockSpec((t, LANES), lambda hp, b, i: (b * nq + i, hp)),
                  pl.BlockSpec((seq, LANES), lambda hp, b, i: (b, npair + hp)),
                  pl.BlockSpec((seq, LANES), lambda hp, b, i: (b, 2 * npair + hp)),
                  pl.BlockSpec((2, t, BAND_SEGS * t), lambda hp, b, i: (hp, 0, 0))],
        out_specs=pl.BlockSpec((t, LANES), lambda hp, b, i: (b * nq + i, hp)),
        out_shape=jax.ShapeDtypeStruct((batch * seq, C_C), BF16),
        compiler_params=_cparams("parallel", "parallel", "parallel"),
        name="band_prompt",
    )(proj, proj, proj, bias)


def _band_sample_kernel(q_ref, kn_ref, vn_ref, kp_ref, vp_ref, biasp_ref, biasn_ref, o_ref):
    t = q_ref.shape[0]
    lane = lax.broadcasted_iota(jnp.int32, (t, LANES), 1)
    low = lane < HD_C
    for hp in range(C_C // LANES):
        sl = slice(hp * LANES, (hp + 1) * LANES)
        q = q_ref[:, sl] * HD_C ** -0.5
        kp = kp_ref[0, :, sl].astype(BF16)
        vp = vp_ref[0, :, sl].astype(BF16)
        kn = kn_ref[:, sl].astype(BF16)
        vn = vn_ref[:, sl].astype(BF16)
        outs = []
        for hh in range(2):
            qh = jnp.where(low if hh == 0 else jnp.logical_not(low), q, 0.0)
            s_p = _dot_nt(qh, kp) + biasp_ref[2 * hp + hh]
            s_n = _dot_nt(qh, kn) + biasn_ref[2 * hp + hh]
            m = jnp.maximum(jnp.max(s_p, axis=-1, keepdims=True), jnp.max(s_n, axis=-1, keepdims=True))
            p_p = jnp.exp(s_p - m)
            p_n = jnp.exp(s_n - m)
            l = jnp.sum(p_p, axis=-1, keepdims=True) + jnp.sum(p_n, axis=-1, keepdims=True)
            outs.append((_dot(p_p, vp) + _dot(p_n, vn)) / l)
        o_ref[:, sl] = jnp.where(low, outs[0], outs[1]).astype(BF16)


def band_sample(q, kn, vn, k_past, v_past, bias_p, bias_n, batch, seq):
    past = k_past.shape[1]
    new = pl.BlockSpec((seq, C_C), lambda b: (b, 0))
    old = pl.BlockSpec((1, past, C_C), lambda b: (b, 0, 0))
    return pl.pallas_call(
        _band_sample_kernel,
        grid=(batch,),
        in_specs=[new, new, new, old, old,
                  _resident((H_C, seq, past), lambda b: (0, 0, 0)),
                  _resident((H_C, seq, seq), lambda b: (0, 0, 0))],
        out_specs=pl.BlockSpec((seq, C_C), lambda b: (b, 0)),
        out_shape=jax.ShapeDtypeStruct((batch * seq, C_C), BF16),
        compiler_params=_cparams("parallel"),
        name="band_sample",
    )(q, kn, vn, k_past, v_past, bias_p, bias_n)


def _gla_kernel(q_ref, k_ref, v_ref, go_ref, gk_ref, wgk_ref, bgk_ref, gn_ref, s0_ref,
                o_ref, sout_ref, s_scr, *, chunk):
    c = pl.program_id(1)
    nc = pl.num_programs(1)
    L = chunk

    @pl.when(c == 0)
    def _():
        s_scr[...] = s0_ref[0]

    causal = (lax.broadcasted_iota(jnp.int32, (L, L), 0) >= lax.broadcasted_iota(jnp.int32, (L, L), 1))
    diag = (lax.broadcasted_iota(jnp.int32, (DK_D, DK_D), 0)
            == lax.broadcasted_iota(jnp.int32, (DK_D, DK_D), 1))
    gk = gk_ref[...]
    for h in range(H_D):
        ksl = slice(h * DK_D, (h + 1) * DK_D)
        vsl = slice(h * DV_D, (h + 1) * DV_D)
        x = _dot(gk, wgk_ref[:, ksl]) + bgk_ref[:, ksl]
        la = -_softplus(-x) / GATE_NORM
        b = _cumsum_rows(la)
        qh = q_ref[:, ksl] * DK_D ** -0.5
        kh = k_ref[:, ksl]
        vh = v_ref[:, vsl]
        qe = qh * jnp.exp(b)
        ke = kh * jnp.exp(-b)
        att = jnp.where(causal, _dot_nt(qe, ke), 0.0)
        s_prev = s_scr[h]
        o = _dot(att, vh) + _dot(qe, s_prev)
        b_end = b[L - 1:L, :]
        kd = kh * jnp.exp(b_end - b)
        e_end = jnp.broadcast_to(jnp.exp(b_end), (DK_D, DK_D))
        e_col = jnp.sum(jnp.where(diag, e_end, 0.0), axis=1, keepdims=True)
        s_scr[h] = s_prev * e_col + _dot_tn(kd, vh)
        on = o * lax.rsqrt(jnp.mean(o * o, axis=-1, keepdims=True) + EPS) * gn_ref[...]
        gate = go_ref[:, vsl]
        o_ref[:, vsl] = (on * (gate * _sigmoid(gate))).astype(BF16)

    @pl.when(c == nc - 1)
    def _():
        sout_ref[0] = s_scr[...]


COL_QD = 3 * C_C
COL_KD = COL_QD + H_D * DK_D
COL_VD = COL_KD + H_D * DK_D
COL_GO = COL_VD + H_D * DV_D
COL_GK = COL_GO + H_D * DV_D
COLS_ODD_EXT = COL_GK + LANES


def gla(proj, wgk, bgk, gnorm, s0, row0, batch, seq, chunk):
    nc = seq // chunk
    kw = H_D * DK_D
    vw = H_D * DV_D
    blk0 = row0 // chunk
    row = lambda col: (lambda b, c: (blk0 + b * nc + c, col))
    return pl.pallas_call(
        functools.partial(_gla_kernel, chunk=chunk),
        grid=(batch, nc),
        in_specs=[pl.BlockSpec((chunk, kw), row(COL_QD // kw)),
                  pl.BlockSpec((chunk, kw), row(COL_KD // kw)),
                  pl.BlockSpec((chunk, vw), row(COL_VD // vw)),
                  pl.BlockSpec((chunk, vw), row(COL_GO // vw)),
                  pl.BlockSpec((chunk, LANES), row(COL_GK // LANES)),
                  _resident((LANES, kw), lambda b, c: (0, 0)),
                  pl.BlockSpec((1, kw), lambda b, c: (0, 0)),
                  pl.BlockSpec((1, DV_D), lambda b, c: (0, 0)),
                  pl.BlockSpec((1, H_D, DK_D, DV_D), lambda b, c: (b, 0, 0, 0))],
        out_specs=[pl.BlockSpec((chunk, vw), lambda b, c: (b * nc + c, 0)),
                   pl.BlockSpec((1, H_D, DK_D, DV_D), lambda b, c: (b, 0, 0, 0))],
        out_shape=[jax.ShapeDtypeStruct((batch * seq, vw), BF16),
                   jax.ShapeDtypeStruct((batch, H_D, DK_D, DV_D), F32)],
        scratch_shapes=[pltpu.VMEM((H_D, DK_D, DV_D), F32)],
        compiler_params=_cparams("parallel", "arbitrary"),
        name="gla",
    )(proj, proj, proj, proj, proj, wgk, bgk.reshape(1, -1), gnorm.reshape(1, -1), s0)


FFN_TM = 512
FFN_TF = 512
HALO = 16


def _conv3(u_scr, cw_ref, cb_ref, tm, prev1=None, prev2=None):
    u0 = u_scr[HALO:HALO + tm, :]
    u1 = u_scr[HALO - 1:HALO - 1 + tm, :]
    u2 = u_scr[HALO - 2:HALO - 2 + tm, :]
    if prev1 is not None:
        u1 = prev1(u1)
        u2 = prev2(u2)
    return cb_ref[...] + cw_ref[2:3, :] * u0 + cw_ref[1:2, :] * u1 + cw_ref[0:1, :] * u2


def _ffn_prompt_kernel(x_ref, xh_ref, g_ref, wg_ref, wv_ref, cwg_ref, cwv_ref, cbg_ref, cbv_ref, wd_ref,
                       o_ref, tg_ref, tv_ref, h_scr, acc_scr, ug_scr, uv_scr, *, tm, blocks_per_seq):
    i = pl.program_id(0)
    j = pl.program_id(1)
    nj = pl.num_programs(1)

    @pl.when(j == 0)
    def _():
        h_scr[HALO:HALO + tm, :] = _rms(x_ref[...], g_ref[...]).astype(BF16)
        keep = (i % blocks_per_seq != 0).astype(F32)
        h_scr[0:HALO, :] = (_rms(xh_ref[...], g_ref[...]) * keep).astype(BF16)
        acc_scr[...] = jnp.zeros_like(acc_scr)

    h = h_scr[...]
    ug_scr[...] = jnp.dot(h, wg_ref[...], preferred_element_type=F32)
    uv_scr[...] = jnp.dot(h, wv_ref[...], preferred_element_type=F32)
    cg = _conv3(ug_scr, cwg_ref, cbg_ref, tm)
    cv = _conv3(uv_scr, cwv_ref, cbv_ref, tm)
    act = cg * _sigmoid(cg) * cv
    acc_scr[...] += _dot(act, wd_ref[...])
    tg_ref[0] = ug_scr[HALO + tm - 8:HALO + tm, :]
    tv_ref[0] = uv_scr[HALO + tm - 8:HALO + tm, :]

    @pl.when(j == nj - 1)
    def _():
        o_ref[...] = x_ref[...] + acc_scr[...]


def ffn_prompt(x, g, w_up, conv_w, conv_b, w_down, seq):
    m, k = x.shape
    tm, tf = FFN_TM, FFN_TF
    nj = D_FF // tf
    nblk = m // tm
    bps = seq // tm
    halo_blocks = tm // HALO
    cb = conv_b.reshape(1, -1)
    return pl.pallas_call(
        functools.partial(_ffn_prompt_kernel, tm=tm, blocks_per_seq=bps),
        grid=(nblk, nj),
        in_specs=[pl.BlockSpec((tm, k), lambda i, j: (i, 0)),
                  pl.BlockSpec((HALO, k), lambda i, j: (jnp.maximum(i * halo_blocks - 1, 0), 0)),
                  pl.BlockSpec((1, k), lambda i, j: (0, 0)),
                  pl.BlockSpec((k, tf), lambda i, j: (0, j)),
                  pl.BlockSpec((k, tf), lambda i, j: (0, nj + j)),
                  pl.BlockSpec((CONV_W, tf), lambda i, j: (0, j)),
                  pl.BlockSpec((CONV_W, tf), lambda i, j: (0, nj + j)),
                  pl.BlockSpec((1, tf), lambda i, j: (0, j)),
                  pl.BlockSpec((1, tf), lambda i, j: (0, nj + j)),
                  pl.BlockSpec((tf, k), lambda i, j: (j, 0))],
        out_specs=[pl.BlockSpec((tm, k), lambda i, j: (i, 0)),
                   pl.BlockSpec((1, 8, tf), lambda i, j: (i, 0, j)),
                   pl.BlockSpec((1, 8, tf), lambda i, j: (i, 0, j))],
        out_shape=[jax.ShapeDtypeStruct((m, k), F32),
                   jax.ShapeDtypeStruct((nblk, 8, D_FF), F32),
                   jax.ShapeDtypeStruct((nblk, 8, D_FF), F32)],
        scratch_shapes=[pltpu.VMEM((tm + HALO, k), BF16), pltpu.VMEM((tm, k), F32),
                        pltpu.VMEM((tm + HALO, tf), F32), pltpu.VMEM((tm + HALO, tf), F32)],
        compiler_params=_cparams("parallel", "arbitrary"),
        name="ffn_prompt",
    )(x, x, g.reshape(1, k), w_up, w_up, conv_w, conv_w, cb, cb, w_down)


def _ffn_sample_kernel(x_ref, g_ref, wg_ref, wv_ref, cwg_ref, cwv_ref, cbg_ref, cbv_ref, wd_ref,
                       s1g_ref, s2g_ref, s1v_ref, s2v_ref,
                       o_ref, ug_ref, uv_ref, h_scr, acc_scr, ug_scr, uv_scr, *, tm, seq):
    j = pl.program_id(0)
    nj = pl.num_programs(0)

    @pl.when(j == 0)
    def _():
        h_scr[...] = _rms(x_ref[...], g_ref[...]).astype(BF16)
        acc_scr[...] = jnp.zeros_like(acc_scr)
        ug_scr[0:HALO, :] = jnp.zeros((HALO, ug_scr.shape[1]), F32)
        uv_scr[0:HALO, :] = jnp.zeros((HALO, uv_scr.shape[1]), F32)

    h = h_scr[...]
    ug = jnp.dot(h, wg_ref[...], preferred_element_type=F32)
    uv = jnp.dot(h, wv_ref[...], preferred_element_type=F32)
    ug_scr[HALO:HALO + tm, :] = ug
    uv_scr[HALO:HALO + tm, :] = uv
    ug_ref[...] = ug
    uv_ref[...] = uv
    pos = lax.broadcasted_iota(jnp.int32, (tm, 1), 0) % seq
    in1 = pos >= 1
    in2 = pos >= 2
    cg = _conv3(ug_scr, cwg_ref, cbg_ref, tm,
                lambda u: jnp.where(in1, u, s1g_ref[...]), lambda u: jnp.where(in2, u, s2g_ref[...]))
    cv = _conv3(uv_scr, cwv_ref, cbv_ref, tm,
                lambda u: jnp.where(in1, u, s1v_ref[...]), lambda u: jnp.where(in2, u, s2v_ref[...]))
    act = cg * _sigmoid(cg) * cv
    acc_scr[...] += _dot(act, wd_ref[...])

    @pl.when(j == nj - 1)
    def _():
        o_ref[...] = x_ref[...] + acc_scr[...]


def ffn_sample(x, g, w_up, conv_w, conv_b, w_down, st1, st2, seq):
    m, k = x.shape
    tm, tf = m, FFN_TF
    nj = D_FF // tf
    cb = conv_b.reshape(1, -1)
    gate = lambda shape: pl.BlockSpec(shape, lambda j: (0, j))
    val = lambda shape: pl.BlockSpec(shape, lambda j: (0, nj + j))
    return pl.pallas_call(
        functools.partial(_ffn_sample_kernel, tm=tm, seq=seq),
        grid=(nj,),
        in_specs=[pl.BlockSpec((tm, k), lambda j: (0, 0)),
                  pl.BlockSpec((1, k), lambda j: (0, 0)),
                  gate((k, tf)), val((k, tf)), gate((CONV_W, tf)), val((CONV_W, tf)),
                  gate((1, tf)), val((1, tf)),
                  pl.BlockSpec((tf, k), lambda j: (j, 0)),
                  gate((tm, tf)), gate((tm, tf)), val((tm, tf)), val((tm, tf))],
        out_specs=[pl.BlockSpec((tm, k), lambda j: (0, 0)), gate((tm, tf)), gate((tm, tf))],
        out_shape=[jax.ShapeDtypeStruct((m, k), F32),
                   jax.ShapeDtypeStruct((m, D_FF), F32),
                   jax.ShapeDtypeStruct((m, D_FF), F32)],
        scratch_shapes=[pltpu.VMEM((tm, k), BF16), pltpu.VMEM((tm, k), F32),
                        pltpu.VMEM((tm + HALO, tf), F32), pltpu.VMEM((tm + HALO, tf), F32)],
        compiler_params=_cparams("arbitrary"),
        name="ffn_sample",
    )(x, g.reshape(1, k), w_up, w_up, conv_w, conv_w, cb, cb, w_down, st1, st2, st1, st2)


def _rot_cols(w):
    half = QK_ROPE // 2
    return jnp.concatenate([-w[..., half:], w[..., :half]], axis=-1)


def _rope_tables(pos):
    half = QK_ROPE // 2
    inv = ROPE_THETA ** (-jnp.arange(half, dtype=F32) / half)
    ang = pos.astype(F32)[:, None] * inv[None, :]
    cos = jnp.cos(ang)
    sin = jnp.sin(ang)
    zeros = jnp.zeros((pos.shape[0], LANES - QK_ROPE), F32)
    return (jnp.concatenate([cos, cos, zeros], axis=1), jnp.concatenate([sin, sin, zeros], axis=1))


def _shift_rows(pb, first_rows, nseq, seq):
    x = pb.reshape(nseq, seq, -1)
    return jnp.concatenate([first_rows[:, None, :], x[:, :-1]], axis=1).reshape(nseq * seq, -1)


def _pair_state_in(s):
    b = s.shape[0]
    m = jnp.swapaxes(s, -1, -2).reshape(b, H_B // 2, 2, HD_B, HD_B)
    z = jnp.zeros_like(m[:, :, 0])
    top = jnp.concatenate([m[:, :, 0], z], axis=-1)
    bot = jnp.concatenate([z, m[:, :, 1]], axis=-1)
    return jnp.concatenate([top, bot], axis=-2)


def _pair_state_out(m):
    b = m.shape[0]
    h0 = m[:, :, :HD_B, :HD_B]
    h1 = m[:, :, HD_B:, HD_B:]
    s = jnp.stack([h0, h1], axis=2).reshape(b, H_B, HD_B, HD_B)
    return jnp.swapaxes(s, -1, -2)


def _band_bias_prompt(table):
    t = BAND_Q
    q = jnp.arange(t)[:, None]
    j = jnp.arange(BAND_SEGS * t)[None, :]
    lo = (q // CHUNK) * CHUNK
    allowed = (j >= lo) & (j < lo + BAND + CHUNK)
    idx = jnp.clip(q + BAND - j, -REL_CLIP, REL_CLIP) + REL_CLIP
    return jnp.where(allowed[None], table[:, idx], NEG_BIG).astype(F32)


def _band_bias_sample(table, seq, past):
    q_pos = PAST_LEN + jnp.arange(seq)
    k_pos = PAST_LEN - past + jnp.arange(past + seq)
    idx = jnp.clip(q_pos[:, None] - k_pos[None, :], -REL_CLIP, REL_CLIP) + REL_CLIP
    bias = table[:, idx].astype(F32)
    return bias[:, :, :past], bias[:, :, past:]


def _even_layer(x, e, cos, sin, dims, caches, prm):
    bp, tp, bs, ts = dims
    np_rows = bp * tp
    cache_ckv, cache_kpe, state_rwkv, state_shift = caches
    w_in = prm["ev_w_in"][e]
    wa_cols = w_in[:, :Q_LORA + KV_LORA]
    w_kpe = w_in[:, Q_LORA + KV_LORA:Q_LORA + KV_LORA + QK_ROPE]
    w_b = w_in[:, Q_LORA + KV_LORA + QK_ROPE:]
    w_ext = jnp.concatenate([w_b[:, :3 * C_B], wa_cols, w_kpe, _rot_cols(w_kpe), w_b[:, 3 * C_B:]],
                            axis=1).astype(BF16)
    proj = norm_matmul(x, prm["norm_mix"][2 * e], w_ext)

    wq = prm["ev_w_uq"][e].reshape(Q_LORA, H_A, QK_NOPE + QK_ROPE)
    wq_pe = wq[:, :, QK_NOPE:]
    wq_ext = jnp.concatenate([wq[:, :, :QK_NOPE], wq_pe, _rot_cols(wq_pe)], axis=-1)
    wq_ext = wq_ext.reshape(Q_LORA, H_A * HEAD_W).astype(BF16)
    wkv = prm["ev_w_ukv"][e].astype(BF16)
    q, kx, v, ckv, kpe = mla_prep(proj, cos, sin, prm["ev_q_norm"][e], prm["ev_kv_norm"][e], wq_ext, wkv)
    o_a_p = mla_prompt(q, kx, v, bp, tp)
    o_a_s = mla_sample(q[:, np_rows:], cache_ckv[e], cache_kpe[e], ckv[np_rows:], kpe[np_rows:], wkv, bs, ts)
    o_a = jnp.concatenate([o_a_p, o_a_s], axis=0)

    pb = jnp.concatenate([proj[:, :3 * C_B], proj[:, COL_WA:]], axis=1)
    pprev = jnp.concatenate([
        _shift_rows(pb[:np_rows], jnp.zeros((bp, pb.shape[1]), F32), bp, tp),
        _shift_rows(pb[np_rows:], state_shift[e], bs, ts)], axis=0)
    mu = prm["ev_mu"][e]
    mu_parts = [mu[None, 0:C_B], mu[None, C_B:2 * C_B], mu[None, 2 * C_B:3 * C_B],
                mu[None, 3 * C_B:3 * C_B + LANES], mu[None, 3 * C_B + LANES:]]
    zw = jnp.zeros((W_LORA, C_B), F32)
    ww2p = jnp.concatenate([prm["ev_w_w2"][e], zw], axis=0).astype(BF16)
    wa2p = jnp.concatenate([zw, prm["ev_w_a2"][e]], axis=0).astype(BF16)
    r, lw, k, vv, kk, a, g = rwkv_prep(proj, pprev, mu_parts, prm["ev_w0"][e], ww2p, prm["ev_a0"][e], wa2p,
                                       prm["ev_w_g2"][e].astype(BF16), prm["ev_k_k"][e], prm["ev_k_a"][e])
    r_k = prm["ev_r_k"][e].reshape(-1)
    lnw, lnb = prm["ev_lnx_w"][e], prm["ev_lnx_b"][e]
    m0_p = jnp.zeros((bp, H_B // 2, LANES, LANES), F32)
    acts = (r, lw, k, vv, kk, a, g)
    o_b_p, m_p = rwkv_chunk(*acts, r_k, lnw, lnb, m0_p, 0, bp, tp, min(CHUNK, tp), 2)
    o_b_s, m_s = rwkv_chunk(*acts, r_k, lnw, lnb, _pair_state_in(state_rwkv[e]), np_rows, bs, ts,
                            min(CHUNK, ts), 2)
    o_b = jnp.concatenate([o_b_p, o_b_s], axis=0)

    w_out = prm["ev_w_out"][e].astype(BF16)
    x = out_proj(x, o_a, o_b, w_out[:H_A * V_A], w_out[H_A * V_A:])
    outs = dict(
        ckv_p=ckv[:np_rows].reshape(bp, tp, KV_LORA), ckv_s=ckv[np_rows:].reshape(bs, ts, KV_LORA),
        kpe_p=kpe[:np_rows].reshape(bp, tp, QK_ROPE), kpe_s=kpe[np_rows:].reshape(bs, ts, QK_ROPE),
        rw_p=_pair_state_out(m_p), rw_s=_pair_state_out(m_s),
        sh_p=pb[:np_rows].reshape(bp, tp, -1)[:, -1], sh_s=pb[np_rows:].reshape(bs, ts, -1)[:, -1])
    return x, outs


def _odd_layer(x, o, dims, caches, prm):
    bp, tp, bs, ts = dims
    np_rows = bp * tp
    cache_k, cache_v, state_gla = caches
    band_past = cache_k.shape[2]
    w_in = prm["od_w_in"][o]
    c_gk = 3 * C_C + 2 * H_D * DK_D + H_D * DV_D
    w_ext = jnp.concatenate([w_in[:, :c_gk], w_in[:, c_gk + GK_LORA:], w_in[:, c_gk:c_gk + GK_LORA],
                             jnp.zeros((D_MODEL, LANES - GK_LORA), F32)], axis=1).astype(BF16)
    proj = norm_matmul(x, prm["norm_mix"][2 * o + 1], w_ext)

    table = prm["od_rel_bias"][o]
    o_c_p = band_prompt(proj, _band_bias_prompt(table), bp, tp)
    ps = proj[np_rows:]
    bias_p, bias_n = _band_bias_sample(table, ts, band_past)
    k_past = cache_k[o].reshape(bs, band_past, C_C)
    v_past = cache_v[o].reshape(bs, band_past, C_C)
    o_c_s = band_sample(ps[:, :C_C], ps[:, C_C:2 * C_C], ps[:, 2 * C_C:3 * C_C], k_past, v_past,
                        bias_p, bias_n, bs, ts)
    o_c = jnp.concatenate([o_c_p, o_c_s], axis=0)

    wgk = jnp.concatenate([prm["od_w_gk2"][o], jnp.zeros((LANES - GK_LORA, H_D * DK_D), F32)],
                          axis=0).astype(BF16)
    bgk, gn = prm["od_b_gk"][o], prm["od_gnorm"][o]
    s0_p = jnp.zeros((bp, H_D, DK_D, DV_D), F32)
    o_d_p, s_p = gla(proj, wgk, bgk, gn, s0_p, 0, bp, tp, min(CHUNK, tp))
    o_d_s, s_s = gla(proj, wgk, bgk, gn, state_gla[o], np_rows, bs, ts, min(CHUNK, ts))
    o_d = jnp.concatenate([o_d_p, o_d_s], axis=0)

    w_out = prm["od_w_out"][o].astype(BF16)
    x = out_proj(x, o_c, o_d, w_out[:C_C], w_out[C_C:])

    kc_p = proj[:np_rows, C_C:2 * C_C].reshape(bp, tp, H_C, HD_C)
    vc_p = proj[:np_rows, 2 * C_C:3 * C_C].reshape(bp, tp, H_C, HD_C)
    kc_s = ps[:, C_C:2 * C_C].reshape(bs, ts, H_C, HD_C)
    vc_s = ps[:, 2 * C_C:3 * C_C].reshape(bs, ts, H_C, HD_C)

    def tail_prompt(t):
        if tp >= band_past:
            return t[:, tp - band_past:]
        return jnp.pad(t, ((0, 0), (band_past - tp, 0), (0, 0), (0, 0)))

    outs = dict(
        bk_p=tail_prompt(kc_p), bv_p=tail_prompt(vc_p),
        bk_s=jnp.concatenate([cache_k[o], kc_s], axis=1)[:, ts:],
        bv_s=jnp.concatenate([cache_v[o], vc_s], axis=1)[:, ts:],
        gla_p=s_p, gla_s=s_s)
    return x, outs


def _ffn_layer(x, layer, dims, state_conv, prm):
    bp, tp, bs, ts = dims
    np_rows = bp * tp
    g = prm["norm_ffn"][layer]
    w_up = prm["ffn_w_up"][layer].astype(BF16)
    w_down = prm["ffn_w_down"][layer].astype(BF16)
    cw, cb = prm["ffn_conv_w"][layer], prm["ffn_conv_b"][layer]
    xp, tg, tv = ffn_prompt(x[:np_rows], g, w_up, cw, cb, w_down, tp)
    bps = tp // FFN_TM
    tails = jnp.concatenate([tg, tv], axis=-1).reshape(bp, bps, 8, 2 * D_FF)[:, -1, 8 - (CONV_W - 1):]

    st = state_conv[layer]
    zeros = jnp.zeros((bs, ts - 1, 2 * D_FF), F32)
    st1 = jnp.concatenate([st[:, 1:2], zeros], axis=1).reshape(bs * ts, 2 * D_FF)
    st2 = jnp.concatenate([st, zeros[:, 1:]], axis=1).reshape(bs * ts, 2 * D_FF)
    xs, ug, uv = ffn_sample(x[np_rows:], g, w_up, cw, cb, w_down, st1, st2, ts)
    u_s = jnp.concatenate([ug, uv], axis=-1).reshape(bs, ts, 2 * D_FF)
    ext = jnp.concatenate([st, u_s], axis=1)[:, ts:]
    return jnp.concatenate([xp, xs], axis=0), tails, ext


@jax.jit
def kernel(x_prompt, x_sample, cache_mla_ckv, cache_mla_kpe, state_rwkv, state_rwkv_shift, cache_band_k, cache_band_v, state_gla, state_ffn_conv, norm_mix, norm_ffn, norm_final, ev_w_in, ev_q_norm, ev_w_uq, ev_kv_norm, ev_w_ukv, ev_mu, ev_w0, ev_w_w2, ev_a0, ev_w_a2, ev_w_g2, ev_k_k, ev_k_a, ev_r_k, ev_lnx_w, ev_lnx_b, ev_w_out, od_w_in, od_rel_bias, od_w_gk2, od_b_gk, od_gnorm, od_w_out, ffn_w_up, ffn_conv_w, ffn_conv_b, ffn_w_down):
    prm = dict(norm_mix=norm_mix, norm_ffn=norm_ffn, ev_w_in=ev_w_in, ev_q_norm=ev_q_norm, ev_w_uq=ev_w_uq,
               ev_kv_norm=ev_kv_norm, ev_w_ukv=ev_w_ukv, ev_mu=ev_mu, ev_w0=ev_w0, ev_w_w2=ev_w_w2,
               ev_a0=ev_a0, ev_w_a2=ev_w_a2, ev_w_g2=ev_w_g2, ev_k_k=ev_k_k, ev_k_a=ev_k_a, ev_r_k=ev_r_k,
               ev_lnx_w=ev_lnx_w, ev_lnx_b=ev_lnx_b, ev_w_out=ev_w_out, od_w_in=od_w_in,
               od_rel_bias=od_rel_bias, od_w_gk2=od_w_gk2, od_b_gk=od_b_gk, od_gnorm=od_gnorm,
               od_w_out=od_w_out, ffn_w_up=ffn_w_up, ffn_conv_w=ffn_conv_w, ffn_conv_b=ffn_conv_b,
               ffn_w_down=ffn_w_down)
    bp, tp, _ = x_prompt.shape
    bs, ts, _ = x_sample.shape
    dims = (bp, tp, bs, ts)
    np_rows = bp * tp
    depth = norm_mix.shape[0]
    x = jnp.concatenate([x_prompt.reshape(np_rows, D_MODEL), x_sample.reshape(bs * ts, D_MODEL)], axis=0)
    pos = jnp.concatenate([jnp.tile(jnp.arange(tp), bp), jnp.tile(PAST_LEN + jnp.arange(ts), bs)])
    cos, sin = _rope_tables(pos)

    ev, od, ffn_p, ffn_s = [], [], [], []
    for layer in range(depth):
        if layer % 2 == 0:
            x, outs = _even_layer(x, layer // 2, cos, sin, dims,
                                  (cache_mla_ckv, cache_mla_kpe, state_rwkv, state_rwkv_shift), prm)
            ev.append(outs)
        else:
            x, outs = _odd_layer(x, layer // 2, dims, (cache_band_k, cache_band_v, state_gla), prm)
            od.append(outs)
        x, tails, ext = _ffn_layer(x, layer, dims, state_ffn_conv, prm)
        ffn_p.append(tails)
        ffn_s.append(ext)

    y = final_norm(x, norm_final)
    stack = lambda lst, key: jnp.stack([d[key] for d in lst])
    return (y[:np_rows].reshape(bp, tp, D_MODEL), y[np_rows:].reshape(bs, ts, D_MODEL),
            stack(ev, "ckv_p"), stack(ev, "ckv_s"), stack(ev, "kpe_p"), stack(ev, "kpe_s"),
            stack(ev, "rw_p"), stack(ev, "rw_s"), stack(ev, "sh_p"), stack(ev, "sh_s"),
            stack(od, "bk_p"), stack(od, "bk_s"), stack(od, "bv_p"), stack(od, "bv_s"),
            stack(od, "gla_p"), stack(od, "gla_s"), jnp.stack(ffn_p), jnp.stack(ffn_s))
```

```python
import functools

import jax
import jax.numpy as jnp
from jax import lax
from jax.experimental import pallas as pl
from jax.experimental.pallas import tpu as pltpu

F32 = jnp.float32
BF16 = jnp.bfloat16

D_MODEL = 2048
CHUNK = 64
EPS = 1e-6
PAST_LEN = 1024

H_A = 8
QK_NOPE = 128
QK_ROPE = 64
V_A = 128
Q_LORA = 512
KV_LORA = 256
ROPE_THETA = 10000.0

H_B = 16
HD_B = 64
C_B = H_B * HD_B
W_LORA = 64
A_LORA = 64
G_LORA = 128
LNX_EPS = 64e-5

H_C = 16
HD_C = 64
C_C = H_C * HD_C
BAND = 8 * CHUNK
REL_CLIP = 128

H_D = 4
DK_D = 128
DV_D = 256
GK_LORA = 16
GATE_NORM = 16.0

D_FF = 5632
CONV_W = 3

LANES = 128
VMEM_LIMIT_BYTES = 56 * 2 ** 20
NEG_BIG = -1e30
ROW_TILE = 256
MLA_SCALE = (QK_NOPE + QK_ROPE) ** -0.5


def _cparams(*sem):
    return pltpu.CompilerParams(dimension_semantics=sem, vmem_limit_bytes=VMEM_LIMIT_BYTES)


def _resident(shape, index_map):
    return pl.BlockSpec(shape, index_map, pipeline_mode=pl.Buffered(1))


def _dot(a, b):
    return jnp.dot(a.astype(BF16), b.astype(BF16), preferred_element_type=F32)


def _dot_nt(a, b):
    return lax.dot_general(a.astype(BF16), b.astype(BF16), (((1,), (1,)), ((), ())),
                           preferred_element_type=F32)


def _dot_tn(a, b):
    return lax.dot_general(a.astype(BF16), b.astype(BF16), (((0,), (0,)), ((), ())),
                           preferred_element_type=F32)


def _rms(x, g):
    return x * lax.rsqrt(jnp.mean(x * x, axis=-1, keepdims=True) + EPS) * g


def _sigmoid(x):
    return 1.0 / (1.0 + jnp.exp(-x))


def _softplus(x):
    return jnp.maximum(x, 0.0) + jnp.log(1.0 + jnp.exp(-jnp.abs(x)))


def _cumsum_rows(x):
    n = x.shape[0]
    row = lax.broadcasted_iota(jnp.int32, x.shape, 0)
    s = 1
    while s < n:
        x = x + jnp.where(row >= s, pltpu.roll(x, s, 0), 0.0)
        s *= 2
    return x


def _norm_matmul_kernel(x_ref, g_ref, w_ref, o_ref):
    o_ref[...] = _dot(_rms(x_ref[...], g_ref[...]), w_ref[...])


def norm_matmul(x, g, w):
    m, k = x.shape
    n = w.shape[1]
    tm = ROW_TILE
    return pl.pallas_call(
        _norm_matmul_kernel,
        grid=(m // tm,),
        in_specs=[pl.BlockSpec((tm, k), lambda i: (i, 0)),
                  pl.BlockSpec((1, k), lambda i: (0, 0)),
                  _resident((k, n), lambda i: (0, 0))],
        out_specs=pl.BlockSpec((tm, n), lambda i: (i, 0)),
        out_shape=jax.ShapeDtypeStruct((m, n), F32),
        compiler_params=_cparams("parallel"),
        name="norm_matmul",
    )(x, g.reshape(1, k), w)


def _out_proj_kernel(res_ref, a1_ref, a2_ref, w1_ref, w2_ref, o_ref):
    o_ref[...] = res_ref[...] + _dot(a1_ref[...], w1_ref[...]) + _dot(a2_ref[...], w2_ref[...])


def out_proj(res, a1, a2, w1, w2):
    m, n = res.shape
    k1, k2 = a1.shape[1], a2.shape[1]
    tm = ROW_TILE
    return pl.pallas_call(
        _out_proj_kernel,
        grid=(m // tm,),
        in_specs=[pl.BlockSpec((tm, n), lambda i: (i, 0)),
                  pl.BlockSpec((tm, k1), lambda i: (i, 0)),
                  pl.BlockSpec((tm, k2), lambda i: (i, 0)),
                  _resident((k1, n), lambda i: (0, 0)),
                  _resident((k2, n), lambda i: (0, 0))],
        out_specs=pl.BlockSpec((tm, n), lambda i: (i, 0)),
        out_shape=jax.ShapeDtypeStruct((m, n), F32),
        compiler_params=_cparams("parallel"),
        name="out_proj",
    )(res, a1, a2, w1, w2)


def _final_norm_kernel(x_ref, g_ref, o_ref):
    o_ref[...] = _rms(x_ref[...], g_ref[...])


def final_norm(x, g):
    m, k = x.shape
    tm = ROW_TILE
    return pl.pallas_call(
        _final_norm_kernel,
        grid=(m // tm,),
        in_specs=[pl.BlockSpec((tm, k), lambda i: (i, 0)), pl.BlockSpec((1, k), lambda i: (0, 0))],
        out_specs=pl.BlockSpec((tm, k), lambda i: (i, 0)),
        out_shape=jax.ShapeDtypeStruct((m, k), F32),
        compiler_params=_cparams("parallel"),
        name="final_norm",
    )(x, g.reshape(1, k))


COL_CQ = 3 * C_B
COL_CKV = COL_CQ + Q_LORA
COL_KPE = COL_CKV + KV_LORA
COL_WA = COL_KPE + 2 * QK_ROPE
COL_G = COL_WA + W_LORA + A_LORA
COLS_EVEN_EXT = COL_G + G_LORA
HEAD_W = 2 * LANES


def _mla_prep_kernel(cq_ref, ckv_ref, kpe_ref, cos_ref, sin_ref, qn_ref, kvn_ref, wq_ref, wkv_ref,
                     q_ref, kx_ref, v_ref, ckv_out_ref, kpe_out_ref):
    cos = cos_ref[...]
    sin = sin_ref[...]
    z = _dot(_rms(cq_ref[...], qn_ref[...]), wq_ref[...])
    ckvn = _rms(ckv_ref[...], kvn_ref[...])
    ckv_out_ref[...] = ckvn
    kp = kpe_ref[...]
    kr = kp * cos + pltpu.roll(kp, QK_ROPE, 1) * sin
    kpe_out_ref[...] = kr[:, :QK_ROPE]
    kr16 = kr.astype(BF16)
    kv = _dot(ckvn, wkv_ref[...])
    for h in range(H_A):
        c0 = h * HEAD_W
        q_ref[h, :, 0:LANES] = (z[:, c0:c0 + LANES] * MLA_SCALE).astype(BF16)
        t2 = z[:, c0 + LANES:c0 + HEAD_W]
        q_ref[h, :, LANES:HEAD_W] = ((t2 * cos + pltpu.roll(t2, QK_ROPE, 1) * sin) * MLA_SCALE).astype(BF16)
        kx_ref[h, :, 0:LANES] = kv[:, c0:c0 + LANES].astype(BF16)
        kx_ref[h, :, LANES:HEAD_W] = kr16
        v_ref[h] = kv[:, c0 + LANES:c0 + HEAD_W].astype(BF16)


def mla_prep(proj, cos, sin, q_norm, kv_norm, wq, wkv):
    m = proj.shape[0]
    tm = ROW_TILE
    return pl.pallas_call(
        _mla_prep_kernel,
        grid=(m // tm,),
        in_specs=[pl.BlockSpec((tm, Q_LORA), lambda i: (i, COL_CQ // Q_LORA)),
                  pl.BlockSpec((tm, KV_LORA), lambda i: (i, COL_CKV // KV_LORA)),
                  pl.BlockSpec((tm, LANES), lambda i: (i, COL_KPE // LANES)),
                  pl.BlockSpec((tm, LANES), lambda i: (i, 0)),
                  pl.BlockSpec((tm, LANES), lambda i: (i, 0)),
                  pl.BlockSpec((1, Q_LORA), lambda i: (0, 0)),
                  pl.BlockSpec((1, KV_LORA), lambda i: (0, 0)),
                  _resident((Q_LORA, H_A * HEAD_W), lambda i: (0, 0)),
                  _resident((KV_LORA, H_A * HEAD_W), lambda i: (0, 0))],
        out_specs=[pl.BlockSpec((H_A, tm, HEAD_W), lambda i: (0, i, 0)),
                   pl.BlockSpec((H_A, tm, HEAD_W), lambda i: (0, i, 0)),
                   pl.BlockSpec((H_A, tm, V_A), lambda i: (0, i, 0)),
                   pl.BlockSpec((tm, KV_LORA), lambda i: (i, 0)),
                   pl.BlockSpec((tm, QK_ROPE), lambda i: (i, 0))],
        out_shape=[jax.ShapeDtypeStruct((H_A, m, HEAD_W), BF16),
                   jax.ShapeDtypeStruct((H_A, m, HEAD_W), BF16),
                   jax.ShapeDtypeStruct((H_A, m, V_A), BF16),
                   jax.ShapeDtypeStruct((m, KV_LORA), F32),
                   jax.ShapeDtypeStruct((m, QK_ROPE), F32)],
        compiler_params=_cparams("parallel"),
        name="mla_prep",
    )(proj, proj, proj, cos, sin, q_norm.reshape(1, -1), kv_norm.reshape(1, -1), wq, wkv)


ATT_BLOCK = 512
MLA_HEADS_PER_STEP = 2


def _mla_prompt_kernel(q_ref, k_ref, v_ref, o_ref):
    t = ATT_BLOCK
    qi = pl.program_id(2)

    def block(j, carry, masked):
        start = pl.multiple_of(j * t, t)
        heads = range(MLA_HEADS_PER_STEP)
        s = [lax.dot_general(q_ref[g], k_ref[g, pl.ds(start, t), :], (((1,), (1,)), ((), ())),
                             preferred_element_type=F32) for g in heads]
        if masked:
            qc = lax.broadcasted_iota(jnp.int32, (t, t), 0) // CHUNK
            kc = lax.broadcasted_iota(jnp.int32, (t, t), 1) // CHUNK
            s = [jnp.where(kc <= qc, x, NEG_BIG) for x in s]
        m_new = [jnp.maximum(carry[g][0], jnp.max(s[g], axis=-1, keepdims=True)) for g in heads]
        p = [jnp.exp(s[g] - m_new[g]) for g in heads]
        pv = [jnp.dot(p[g].astype(BF16), v_ref[g, pl.ds(start, t), :], preferred_element_type=F32)
              for g in heads]
        out = []
        for g in heads:
            m, l, acc = carry[g]
            alpha = jnp.exp(m - m_new[g])
            out.append((m_new[g], alpha * l + jnp.sum(p[g], axis=-1, keepdims=True), alpha * acc + pv[g]))
        return tuple(out)

    init = tuple((jnp.full((t, 1), NEG_BIG, F32), jnp.zeros((t, 1), F32), jnp.zeros((t, V_A), F32))
                 for _ in range(MLA_HEADS_PER_STEP))
    carry = lax.fori_loop(0, qi, lambda j, c: block(j, c, False), init)
    carry = block(qi, carry, True)
    for g in range(MLA_HEADS_PER_STEP):
        m, l, acc = carry[g]
        o_ref[:, g * V_A:(g + 1) * V_A] = (acc / l).astype(BF16)


def mla_prompt(q, kx, v, batch, seq):
    t = ATT_BLOCK
    g = MLA_HEADS_PER_STEP
    assert seq % t == 0 and H_A % g == 0
    nq = seq // t
    return pl.pallas_call(
        _mla_prompt_kernel,
        grid=(batch, H_A // g, nq),
        in_specs=[pl.BlockSpec((g, t, HEAD_W), lambda b, h, i: (h, b * nq + i, 0)),
                  pl.BlockSpec((g, seq, HEAD_W), lambda b, h, i: (h, b, 0)),
                  pl.BlockSpec((g, seq, V_A), lambda b, h, i: (h, b, 0))],
        out_specs=pl.BlockSpec((t, g * V_A), lambda b, h, i: (b * nq + i, h)),
        out_shape=jax.ShapeDtypeStruct((batch * seq, H_A * V_A), BF16),
        compiler_params=_cparams("parallel", "parallel", "arbitrary"),
        name="mla_prompt",
    )(q, kx, v)


def _mla_sample_kernel(q_ref, ckvp_ref, kpep_ref, ckvn_ref, kpen_ref, wkv_ref, o_ref):
    ckvp = ckvp_ref[0].astype(BF16)
    ckvn = ckvn_ref[...].astype(BF16)
    kpep = kpep_ref[0].astype(BF16)
    kpen = kpen_ref[...].astype(BF16)
    for h in range(H_A):
        c0 = h * HEAD_W
        w = wkv_ref[:, c0:c0 + HEAD_W]
        kvp = jnp.dot(ckvp, w, preferred_element_type=F32)
        kvn = jnp.dot(ckvn, w, preferred_element_type=F32)
        qn = q_ref[h, :, 0:QK_NOPE]
        qp = q_ref[h, :, LANES:LANES + QK_ROPE]
        s_p = _dot_nt(qn, kvp[:, :QK_NOPE]) + _dot_nt(qp, kpep)
        s_n = _dot_nt(qn, kvn[:, :QK_NOPE]) + _dot_nt(qp, kpen)
        m = jnp.maximum(jnp.max(s_p, axis=-1, keepdims=True), jnp.max(s_n, axis=-1, keepdims=True))
        p_p = jnp.exp(s_p - m)
        p_n = jnp.exp(s_n - m)
        l = jnp.sum(p_p, axis=-1, keepdims=True) + jnp.sum(p_n, axis=-1, keepdims=True)
        o = _dot(p_p, kvp[:, QK_NOPE:]) + _dot(p_n, kvn[:, QK_NOPE:])
        o_ref[:, h * V_A:(h + 1) * V_A] = (o / l).astype(BF16)


def mla_sample(q, ckv_past, kpe_past, ckv_new, kpe_new, wkv, batch, seq):
    past = ckv_past.shape[1]
    return pl.pallas_call(
        _mla_sample_kernel,
        grid=(batch,),
        in_specs=[pl.BlockSpec((H_A, seq, HEAD_W), lambda b: (0, b, 0)),
                  pl.BlockSpec((1, past, KV_LORA), lambda b: (b, 0, 0)),
                  pl.BlockSpec((1, past, QK_ROPE), lambda b: (b, 0, 0)),
                  pl.BlockSpec((seq, KV_LORA), lambda b: (b, 0)),
                  pl.BlockSpec((seq, QK_ROPE), lambda b: (b, 0)),
                  _resident((KV_LORA, H_A * HEAD_W), lambda b: (0, 0))],
        out_specs=pl.BlockSpec((seq, H_A * V_A), lambda b: (b, 0)),
        out_shape=jax.ShapeDtypeStruct((batch * seq, H_A * V_A), BF16),
        compiler_params=_cparams("parallel"),
        name="mla_sample",
    )(q, ckv_past, kpe_past, ckv_new, kpe_new, wkv)


def _rwkv_prep_kernel(r_ref, k_ref, v_ref, wa_ref, g_ref, rp_ref, kp_ref, vp_ref, wap_ref, gp_ref,
                      mu_r_ref, mu_k_ref, mu_v_ref, mu_wa_ref, mu_g_ref,
                      w0_ref, ww2_ref, a0_ref, wa2_ref, wg2_ref, kk_ref, ka_ref,
                      r_out, lw_out, k_out, v_out, kk_out, a_out, g_out, *, tm, seq, first_rows):
    row = lax.broadcasted_iota(jnp.int32, (tm, 1), 0)
    if not first_rows:
        keep = ((pl.program_id(0) * tm) % seq != 0).astype(F32)

    def mix(p_ref, pp_ref, mu_ref):
        p = p_ref[...]
        rolled = pltpu.roll(p, 1, 0)
        if first_rows:
            prev = jnp.where(row % seq == 0, pp_ref[...], rolled)
        else:
            prev = jnp.where(row == 0, pp_ref[7:8, :] * keep, rolled)
        return p + (prev - p) * mu_ref[...]

    r_out[...] = mix(r_ref, rp_ref, mu_r_ref)
    v_out[...] = mix(v_ref, vp_ref, mu_v_ref)
    k = mix(k_ref, kp_ref, mu_k_ref)
    xwa = mix(wa_ref, wap_ref, mu_wa_ref)
    xg = mix(g_ref, gp_ref, mu_g_ref)
    w_log = -_softplus(-(w0_ref[...] + _dot(jnp.tanh(xwa), ww2_ref[...]))) - 0.5
    lw_out[...] = -jnp.exp(w_log)
    a = _sigmoid(a0_ref[...] + _dot(xwa, wa2_ref[...]))
    a_out[...] = a
    g_out[...] = _dot(_sigmoid(xg), wg2_ref[...])
    kk_out[...] = k * kk_ref[...]
    k_out[...] = k * (1.0 + (a - 1.0) * ka_ref[...])


def rwkv_prep(proj, first, mu_parts, w0, ww2p, a0, wa2p, wg2, k_k, k_a, seq):
    m = proj.shape[0]
    tm = ROW_TILE
    wide = lambda c: pl.BlockSpec((tm, C_B), lambda i: (i, c))
    narrow = lambda c: pl.BlockSpec((tm, LANES), lambda i: (i, c))
    vec = lambda n: pl.BlockSpec((1, n), lambda i: (0, 0))
    out = jax.ShapeDtypeStruct((m, C_B), F32)
    if first is None:
        assert seq % tm == 0
        before = lambda i: jnp.maximum(i * (tm // 8) - 1, 0)
        prev_specs = [pl.BlockSpec((8, C_B), lambda i, c=c: (before(i), c)) for c in range(3)]
        prev_specs += [pl.BlockSpec((8, LANES), lambda i, c=c: (before(i), c))
                       for c in (COL_WA // LANES, COL_G // LANES)]
        prev = proj
    else:
        assert tm % seq == 0
        prev_specs = [wide(0), wide(1), wide(2), narrow(3 * C_B // LANES), narrow(3 * C_B // LANES + 1)]
        prev = first
    return pl.pallas_call(
        functools.partial(_rwkv_prep_kernel, tm=tm, seq=seq, first_rows=first is not None),
        grid=(m // tm,),
        in_specs=[wide(0), wide(1), wide(2), narrow(COL_WA // LANES), narrow(COL_G // LANES),
                  *prev_specs,
                  vec(C_B), vec(C_B), vec(C_B), vec(LANES), vec(LANES),
                  vec(C_B), _resident((LANES, C_B), lambda i: (0, 0)),
                  vec(C_B), _resident((LANES, C_B), lambda i: (0, 0)),
                  _resident((G_LORA, C_B), lambda i: (0, 0)), vec(C_B), vec(C_B)],
        out_specs=[pl.BlockSpec((tm, C_B), lambda i: (i, 0))] * 7,
        out_shape=[out] * 7,
        compiler_params=_cparams("parallel"),
        name="rwkv_prep",
    )(proj, proj, proj, proj, proj, prev, prev, prev, prev, prev,
      *mu_parts, w0.reshape(1, -1), ww2p, a0.reshape(1, -1), wa2p, wg2,
      k_k.reshape(1, -1), k_a.reshape(1, -1))


RWKV_PAIRS_PROMPT = 8
RWKV_PAIRS_SAMPLE = 8


def _rwkv_chunk_kernel(r_ref, lw_ref, k_ref, v_ref, kk_ref, a_ref, g_ref, rk_ref, lnw_ref, lnb_ref,
                       m0_ref, o_ref, mout_ref, m_scr, *, chunk, pairs):
    c = pl.program_id(2)
    nc = pl.num_programs(2)
    L = chunk
    L2 = 2 * L

    @pl.when(c == 0)
    def _():
        m_scr[...] = m0_ref[0]

    lane = lax.broadcasted_iota(jnp.int32, (L, LANES), 1)
    low = lane < HD_B
    row_p = lax.broadcasted_iota(jnp.int32, (L, L2), 0)
    col_p = lax.broadcasted_iota(jnp.int32, (L, L2), 1) % L
    strict = row_p > col_p
    incl = row_p >= col_p
    eye_p = (row_p == col_p).astype(F32)
    same_blk = ((lax.broadcasted_iota(jnp.int32, (L2, L2), 0) < L)
                == (lax.broadcasted_iota(jnp.int32, (L2, L2), 1) < L))
    r128 = lax.broadcasted_iota(jnp.int32, (LANES, LANES), 0)
    c128 = lax.broadcasted_iota(jnp.int32, (LANES, LANES), 1)
    same_head = (r128 < HD_B) == (c128 < HD_B)
    diag = r128 == c128

    def seg_sum(x):
        s0 = jnp.sum(jnp.where(low, x, 0.0), axis=-1, keepdims=True)
        s1 = jnp.sum(jnp.where(low, 0.0, x), axis=-1, keepdims=True)
        return jnp.where(low, s0, s1)

    def split_heads(x):
        return jnp.concatenate([jnp.where(low, x, 0.0), jnp.where(low, 0.0, x)], axis=0)

    def block_diag(x):
        return jnp.where(same_blk, jnp.concatenate([x, x], axis=0), 0.0)

    P = range(pairs)
    sls = [slice(p * LANES, (p + 1) * LANES) for p in P]

    def prep(sl):
        r = r_ref[:, sl]
        lw = lw_ref[:, sl]
        k = k_ref[:, sl]
        kkr = kk_ref[:, sl]
        kk = kkr / jnp.maximum(jnp.sqrt(seg_sum(kkr * kkr)), 1e-12)
        bvec = kk * a_ref[:, sl]
        ci = _cumsum_rows(lw)
        c_end = ci[L - 1:L, :]
        e_neg = jnp.exp(-ci)
        e_end = jnp.exp(c_end - ci)
        return dict(r=r, k=k, v=v_ref[:, sl], c_end=c_end,
                    at=-kk * jnp.exp(ci - lw), bt=bvec * e_neg, kt=k * e_neg, rt=r * jnp.exp(ci),
                    bh=bvec * e_end, kh=k * e_end)

    d = [prep(sl) for sl in sls]
    sc = [_dot_nt(jnp.concatenate([x["at"], x["rt"]], axis=0),
                  jnp.concatenate([split_heads(x["bt"]), split_heads(x["kt"])], axis=0)) for x in d]
    nab = [jnp.where(strict, s[0:L, 0:L2], 0.0) for s in sc]
    nak = [jnp.where(strict, s[0:L, L2:2 * L2], 0.0) for s in sc]
    lrb = [jnp.where(incl, s[L:L2, 0:L2], 0.0) for s in sc]
    lrk = [jnp.where(incl, s[L:L2, L2:2 * L2], 0.0) for s in sc]
    v_split = [split_heads(x["v"]) for x in d]

    t_inv = [eye_p + n for n in nab]
    pw = [_dot(n, block_diag(n)) for n in nab]
    nakv = [_dot(nak[p], v_split[p]) for p in P]
    m = 2
    while m < L:
        if 2 * m < L:
            res = [_dot(pw[p], jnp.concatenate([block_diag(pw[p]), block_diag(t_inv[p])], axis=1)) for p in P]
            pw = [x[:, 0:L2] for x in res]
            t_inv = [t_inv[p] + res[p][:, L2:2 * L2] for p in P]
        else:
            t_inv = [t_inv[p] + _dot(pw[p], block_diag(t_inv[p])) for p in P]
        m *= 2
    yk = [_dot(lrk[p], v_split[p]) for p in P]
    hk = [_dot_tn(d[p]["kh"], d[p]["v"]) for p in P]

    wu = [_dot(t_inv[p], jnp.concatenate([split_heads(d[p]["at"]), split_heads(nakv[p])], axis=1)) for p in P]
    qy = [_dot(lrb[p], jnp.concatenate([split_heads(wu[p][:, 0:LANES]), split_heads(wu[p][:, LANES:])], axis=1))
          for p in P]
    gh = [_dot_tn(d[p]["bh"], wu[p]) for p in P]
    q = [d[p]["rt"] + qy[p][:, 0:LANES] for p in P]
    gmat = [jnp.where(diag, jnp.broadcast_to(jnp.exp(d[p]["c_end"]), (LANES, LANES)), 0.0)
            + jnp.where(same_head, gh[p][:, 0:LANES], 0.0) for p in P]
    ym = [_dot(jnp.concatenate([q[p], gmat[p]], axis=0), m_scr[p]) for p in P]
    for p in P:
        m_scr[p] = ym[p][L:L + LANES] + jnp.where(same_head, gh[p][:, LANES:] + hk[p], 0.0)
    for p in P:
        sl = sls[p]
        y = ym[p][0:L] + qy[p][:, LANES:] + yk[p]
        mean = seg_sum(y) * (1.0 / HD_B)
        dev = y - mean
        var = seg_sum(dev * dev) * (1.0 / HD_B)
        yn = dev * lax.rsqrt(var + LNX_EPS) * lnw_ref[:, sl] + lnb_ref[:, sl]
        bonus = seg_sum(d[p]["r"] * d[p]["k"] * rk_ref[:, sl]) * d[p]["v"]
        o_ref[:, sl] = ((yn + bonus) * g_ref[:, sl]).astype(BF16)

    @pl.when(c == nc - 1)
    def _():
        mout_ref[0] = m_scr[...]


def rwkv_chunk(r, lw, k, v, kk, a, g, r_k, lnx_w, lnx_b, m0, row0, batch, seq, chunk, pairs):
    nc = seq // chunk
    npair = C_B // LANES
    ngrp = npair // pairs
    w = pairs * LANES
    blk0 = row0 // chunk
    row = lambda b, pg, c: (blk0 + b * nc + c, pg)
    par = lambda b, pg, c: (0, pg)
    act = pl.BlockSpec((chunk, w), row)
    prm = pl.BlockSpec((1, w), par)
    st = pl.BlockSpec((1, pairs, LANES, LANES), lambda b, pg, c: (b, pg, 0, 0))
    return pl.pallas_call(
        functools.partial(_rwkv_chunk_kernel, chunk=chunk, pairs=pairs),
        grid=(batch, ngrp, nc),
        in_specs=[act] * 7 + [prm] * 3 + [st],
        out_specs=[pl.BlockSpec((chunk, w), lambda b, pg, c: (b * nc + c, pg)), st],
        out_shape=[jax.ShapeDtypeStruct((batch * seq, C_B), BF16),
                   jax.ShapeDtypeStruct((batch, npair, LANES, LANES), F32)],
        scratch_shapes=[pltpu.VMEM((pairs, LANES, LANES), F32)],
        compiler_params=_cparams("parallel", "parallel", "arbitrary"),
        name="rwkv_chunk",
    )(r, lw, k, v, kk, a, g, r_k.reshape(1, -1), lnx_w.reshape(1, -1), lnx_b.reshape(1, -1), m0)


BAND_Q = 256
BAND_SEGS = BAND // BAND_Q + 1


def _band_prompt_kernel(q_ref, k_ref, v_ref, bias_ref, o_ref):
    qb = pl.program_id(2)
    t = BAND_Q
    q = q_ref[...] * HD_C ** -0.5
    lane = lax.broadcasted_iota(jnp.int32, (t, LANES), 1)
    low = lane < HD_C
    ks, vs, valid = [], [], []
    for seg in range(BAND_SEGS):
        kb = qb - (BAND_SEGS - 1) + seg
        start = pl.multiple_of(jnp.maximum(kb, 0) * t, t)
        ks.append(k_ref[pl.ds(start, t), :].astype(BF16))
        vs.append(v_ref[pl.ds(start, t), :].astype(BF16))
        valid.append(kb >= 0)
    outs = []
    for hh in range(2):
        qh = jnp.where(low if hh == 0 else jnp.logical_not(low), q, 0.0).astype(BF16)
        s = []
        for seg in range(BAND_SEGS):
            sv = lax.dot_general(qh, ks[seg], (((1,), (1,)), ((), ())), preferred_element_type=F32)
            sv = sv + bias_ref[hh, :, seg * t:(seg + 1) * t]
            s.append(jnp.where(valid[seg], sv, NEG_BIG))
        m = jnp.max(s[0], axis=-1, keepdims=True)
        for seg in range(1, BAND_SEGS):
            m = jnp.maximum(m, jnp.max(s[seg], axis=-1, keepdims=True))
        l = jnp.zeros((t, 1), F32)
        acc = jnp.zeros((t, LANES), F32)
        for seg in range(BAND_SEGS):
            p = jnp.exp(s[seg] - m)
            l = l + jnp.sum(p, axis=-1, keepdims=True)
            acc = acc + jnp.dot(p.astype(BF16), vs[seg], preferred_element_type=F32)
        outs.append(acc / l)
    o_ref[...] = jnp.where(low, outs[0], outs[1]).astype(BF16)


def band_prompt(proj, bias, batch, seq):
    t = BAND_Q
    nq = seq // t
    npair = C_C // LANES
    return pl.pallas_call(
        _band_prompt_kernel,
        grid=(npair, batch, nq),
        in_specs=[pl.BlockSpec((t, LANES), lambda hp, b, i: (b * nq + i, hp)),
                  pl.BlockSpec((seq, LANES), lambda hp, b, i: (b, npair + hp)),
                  pl.BlockSpec((seq, LANES), lambda hp, b, i: (b, 2 * npair + hp)),
                  pl.BlockSpec((2, t, BAND_SEGS * t), lambda hp, b, i: (hp, 0, 0))],
        out_specs=pl.BlockSpec((t, LANES), lambda hp, b, i: (b * nq + i, hp)),
        out_shape=jax.ShapeDtypeStruct((batch * seq, C_C), BF16),
        compiler_params=_cparams("parallel", "parallel", "parallel"),
        name="band_prompt",
    )(proj, proj, proj, bias)


def _band_sample_kernel(q_ref, kn_ref, vn_ref, kp_ref, vp_ref, biasp_ref, biasn_ref, o_ref):
    t = q_ref.shape[0]
    lane = lax.broadcasted_iota(jnp.int32, (t, LANES), 1)
    low = lane < HD_C
    for hp in range(C_C // LANES):
        sl = slice(hp * LANES, (hp + 1) * LANES)
        q = q_ref[:, sl] * HD_C ** -0.5
        kp = kp_ref[0, :, sl].astype(BF16)
        vp = vp_ref[0, :, sl].astype(BF16)
        kn = kn_ref[:, sl].astype(BF16)
        vn = vn_ref[:, sl].astype(BF16)
        outs = []
        for hh in range(2):
            qh = jnp.where(low if hh == 0 else jnp.logical_not(low), q, 0.0)
            s_p = _dot_nt(qh, kp) + biasp_ref[2 * hp + hh]
            s_n = _dot_nt(qh, kn) + biasn_ref[2 * hp + hh]
            m = jnp.maximum(jnp.max(s_p, axis=-1, keepdims=True), jnp.max(s_n, axis=-1, keepdims=True))
            p_p = jnp.exp(s_p - m)
            p_n = jnp.exp(s_n - m)
            l = jnp.sum(p_p, axis=-1, keepdims=True) + jnp.sum(p_n, axis=-1, keepdims=True)
            outs.append((_dot(p_p, vp) + _dot(p_n, vn)) / l)
        o_ref[:, sl] = jnp.where(low, outs[0], outs[1]).astype(BF16)


def band_sample(proj, k_past, v_past, bias_p, bias_n, batch, seq):
    past = k_past.shape[1]
    new = lambda col: pl.BlockSpec((seq, C_C), lambda b: (b, col))
    old = pl.BlockSpec((1, past, C_C), lambda b: (b, 0, 0))
    return pl.pallas_call(
        _band_sample_kernel,
        grid=(batch,),
        in_specs=[new(0), new(1), new(2), old, old,
                  _resident((H_C, seq, past), lambda b: (0, 0, 0)),
                  _resident((H_C, seq, seq), lambda b: (0, 0, 0))],
        out_specs=pl.BlockSpec((seq, C_C), lambda b: (b, 0)),
        out_shape=jax.ShapeDtypeStruct((batch * seq, C_C), BF16),
        compiler_params=_cparams("parallel"),
        name="band_sample",
    )(proj, proj, proj, k_past, v_past, bias_p, bias_n)


def _gla_kernel(q_ref, k_ref, v_ref, go_ref, gk_ref, wgk_ref, bgk_ref, gn_ref, s0_ref,
                o_ref, sout_ref, s_scr, *, chunk):
    c = pl.program_id(1)
    nc = pl.num_programs(1)
    L = chunk

    @pl.when(c == 0)
    def _():
        s_scr[...] = s0_ref[0]

    causal = (lax.broadcasted_iota(jnp.int32, (L, L), 0) >= lax.broadcasted_iota(jnp.int32, (L, L), 1))
    diag = (lax.broadcasted_iota(jnp.int32, (DK_D, DK_D), 0)
            == lax.broadcasted_iota(jnp.int32, (DK_D, DK_D), 1))
    gk = gk_ref[...]
    for h in range(H_D):
        ksl = slice(h * DK_D, (h + 1) * DK_D)
        vsl = slice(h * DV_D, (h + 1) * DV_D)
        x = _dot(gk, wgk_ref[:, ksl]) + bgk_ref[:, ksl]
        la = -_softplus(-x) / GATE_NORM
        b = _cumsum_rows(la)
        qh = q_ref[:, ksl] * DK_D ** -0.5
        kh = k_ref[:, ksl]
        vh = v_ref[:, vsl]
        qe = qh * jnp.exp(b)
        ke = kh * jnp.exp(-b)
        att = jnp.where(causal, _dot_nt(qe, ke), 0.0)
        s_prev = s_scr[h]
        o = _dot(att, vh) + _dot(qe, s_prev)
        b_end = b[L - 1:L, :]
        kd = kh * jnp.exp(b_end - b)
        e_end = jnp.broadcast_to(jnp.exp(b_end), (DK_D, DK_D))
        e_col = jnp.sum(jnp.where(diag, e_end, 0.0), axis=1, keepdims=True)
        s_scr[h] = s_prev * e_col + _dot_tn(kd, vh)
        on = o * lax.rsqrt(jnp.mean(o * o, axis=-1, keepdims=True) + EPS) * gn_ref[...]
        gate = go_ref[:, vsl]
        o_ref[:, vsl] = (on * (gate * _sigmoid(gate))).astype(BF16)

    @pl.when(c == nc - 1)
    def _():
        sout_ref[0] = s_scr[...]


COL_QD = 3 * C_C
COL_KD = COL_QD + H_D * DK_D
COL_VD = COL_KD + H_D * DK_D
COL_GO = COL_VD + H_D * DV_D
COL_GK = COL_GO + H_D * DV_D
COLS_ODD_EXT = COL_GK + LANES


def gla(proj, wgk, bgk, gnorm, s0, row0, batch, seq, chunk):
    nc = seq // chunk
    kw = H_D * DK_D
    vw = H_D * DV_D
    blk0 = row0 // chunk
    row = lambda col: (lambda b, c: (blk0 + b * nc + c, col))
    return pl.pallas_call(
        functools.partial(_gla_kernel, chunk=chunk),
        grid=(batch, nc),
        in_specs=[pl.BlockSpec((chunk, kw), row(COL_QD // kw)),
                  pl.BlockSpec((chunk, kw), row(COL_KD // kw)),
                  pl.BlockSpec((chunk, vw), row(COL_VD // vw)),
                  pl.BlockSpec((chunk, vw), row(COL_GO // vw)),
                  pl.BlockSpec((chunk, LANES), row(COL_GK // LANES)),
                  _resident((LANES, kw), lambda b, c: (0, 0)),
                  pl.BlockSpec((1, kw), lambda b, c: (0, 0)),
                  pl.BlockSpec((1, DV_D), lambda b, c: (0, 0)),
                  pl.BlockSpec((1, H_D, DK_D, DV_D), lambda b, c: (b, 0, 0, 0))],
        out_specs=[pl.BlockSpec((chunk, vw), lambda b, c: (b * nc + c, 0)),
                   pl.BlockSpec((1, H_D, DK_D, DV_D), lambda b, c: (b, 0, 0, 0))],
        out_shape=[jax.ShapeDtypeStruct((batch * seq, vw), BF16),
                   jax.ShapeDtypeStruct((batch, H_D, DK_D, DV_D), F32)],
        scratch_shapes=[pltpu.VMEM((H_D, DK_D, DV_D), F32)],
        compiler_params=_cparams("parallel", "arbitrary"),
        name="gla",
    )(proj, proj, proj, proj, proj, wgk, bgk.reshape(1, -1), gnorm.reshape(1, -1), s0)


FFN_TM = 512
FFN_TF = 512
HALO = 16


def _conv3(u_scr, cw_ref, cb_ref, tm, prev1=None, prev2=None):
    u0 = u_scr[HALO:HALO + tm, :]
    u1 = u_scr[HALO - 1:HALO - 1 + tm, :]
    u2 = u_scr[HALO - 2:HALO - 2 + tm, :]
    if prev1 is not None:
        u1 = prev1(u1)
        u2 = prev2(u2)
    return cb_ref[...] + cw_ref[2:3, :] * u0 + cw_ref[1:2, :] * u1 + cw_ref[0:1, :] * u2


def _ffn_prompt_kernel(x_ref, xh_ref, g_ref, wg_ref, wv_ref, cwg_ref, cwv_ref, cbg_ref, cbv_ref, wd_ref,
                       o_ref, tg_ref, tv_ref, h_scr, acc_scr, ug_scr, uv_scr, *, tm, blocks_per_seq):
    i = pl.program_id(0)
    j = pl.program_id(1)
    nj = pl.num_programs(1)

    @pl.when(j == 0)
    def _():
        h_scr[HALO:HALO + tm, :] = _rms(x_ref[...], g_ref[...]).astype(BF16)
        keep = (i % blocks_per_seq != 0).astype(F32)
        h_scr[0:HALO, :] = (_rms(xh_ref[...], g_ref[...]) * keep).astype(BF16)
        acc_scr[...] = jnp.zeros_like(acc_scr)

    h = h_scr[...]
    ug_scr[...] = jnp.dot(h, wg_ref[...], preferred_element_type=F32)
    uv_scr[...] = jnp.dot(h, wv_ref[...], preferred_element_type=F32)
    cg = _conv3(ug_scr, cwg_ref, cbg_ref, tm)
    cv = _conv3(uv_scr, cwv_ref, cbv_ref, tm)
    act = cg * _sigmoid(cg) * cv
    acc_scr[...] += _dot(act, wd_ref[...])
    tg_ref[0] = ug_scr[HALO + tm - 8:HALO + tm, :]
    tv_ref[0] = uv_scr[HALO + tm - 8:HALO + tm, :]

    @pl.when(j == nj - 1)
    def _():
        o_ref[...] = x_ref[...] + acc_scr[...]


def ffn_prompt(x, g, w_up, conv_w, conv_b, w_down, seq):
    m, k = x.shape
    tm, tf = FFN_TM, FFN_TF
    nj = D_FF // tf
    nblk = m // tm
    bps = seq // tm
    halo_blocks = tm // HALO
    cb = conv_b.reshape(1, -1)
    return pl.pallas_call(
        functools.partial(_ffn_prompt_kernel, tm=tm, blocks_per_seq=bps),
        grid=(nblk, nj),
        in_specs=[pl.BlockSpec((tm, k), lambda i, j: (i, 0)),
                  pl.BlockSpec((HALO, k), lambda i, j: (jnp.maximum(i * halo_blocks - 1, 0), 0)),
                  pl.BlockSpec((1, k), lambda i, j: (0, 0)),
                  pl.BlockSpec((k, tf), lambda i, j: (0, j)),
                  pl.BlockSpec((k, tf), lambda i, j: (0, nj + j)),
                  pl.BlockSpec((CONV_W, tf), lambda i, j: (0, j)),
                  pl.BlockSpec((CONV_W, tf), lambda i, j: (0, nj + j)),
                  pl.BlockSpec((1, tf), lambda i, j: (0, j)),
                  pl.BlockSpec((1, tf), lambda i, j: (0, nj + j)),
                  pl.BlockSpec((tf, k), lambda i, j: (j, 0))],
        out_specs=[pl.BlockSpec((tm, k), lambda i, j: (i, 0)),
                   pl.BlockSpec((1, 8, tf), lambda i, j: (i, 0, j)),
                   pl.BlockSpec((1, 8, tf), lambda i, j: (i, 0, j))],
        out_shape=[jax.ShapeDtypeStruct((m, k), F32),
                   jax.ShapeDtypeStruct((nblk, 8, D_FF), F32),
                   jax.ShapeDtypeStruct((nblk, 8, D_FF), F32)],
        scratch_shapes=[pltpu.VMEM((tm + HALO, k), BF16), pltpu.VMEM((tm, k), F32),
                        pltpu.VMEM((tm + HALO, tf), F32), pltpu.VMEM((tm + HALO, tf), F32)],
        compiler_params=_cparams("parallel", "arbitrary"),
        name="ffn_prompt",
    )(x, x, g.reshape(1, k), w_up, w_up, conv_w, conv_w, cb, cb, w_down)


def _ffn_sample_kernel(x_ref, g_ref, wg_ref, wv_ref, cwg_ref, cwv_ref, cbg_ref, cbv_ref, wd_ref,
                       s1g_ref, s2g_ref, s1v_ref, s2v_ref,
                       o_ref, ug_ref, uv_ref, h_scr, acc_scr, ug_scr, uv_scr, *, tm, seq):
    j = pl.program_id(0)
    nj = pl.num_programs(0)

    @pl.when(j == 0)
    def _():
        h_scr[...] = _rms(x_ref[...], g_ref[...]).astype(BF16)
        acc_scr[...] = jnp.zeros_like(acc_scr)
        ug_scr[0:HALO, :] = jnp.zeros((HALO, ug_scr.shape[1]), F32)
        uv_scr[0:HALO, :] = jnp.zeros((HALO, uv_scr.shape[1]), F32)

    h = h_scr[...]
    ug = jnp.dot(h, wg_ref[...], preferred_element_type=F32)
    uv = jnp.dot(h, wv_ref[...], preferred_element_type=F32)
    ug_scr[HALO:HALO + tm, :] = ug
    uv_scr[HALO:HALO + tm, :] = uv
    ug_ref[...] = ug
    uv_ref[...] = uv
    pos = lax.broadcasted_iota(jnp.int32, (tm, 1), 0) % seq
    in1 = pos >= 1
    in2 = pos >= 2
    cg = _conv3(ug_scr, cwg_ref, cbg_ref, tm,
                lambda u: jnp.where(in1, u, s1g_ref[...]), lambda u: jnp.where(in2, u, s2g_ref[...]))
    cv = _conv3(uv_scr, cwv_ref, cbv_ref, tm,
                lambda u: jnp.where(in1, u, s1v_ref[...]), lambda u: jnp.where(in2, u, s2v_ref[...]))
    act = cg * _sigmoid(cg) * cv
    acc_scr[...] += _dot(act, wd_ref[...])

    @pl.when(j == nj - 1)
    def _():
        o_ref[...] = x_ref[...] + acc_scr[...]


def ffn_sample(x, g, w_up, conv_w, conv_b, w_down, st1, st2, seq):
    m, k = x.shape
    tm, tf = m, FFN_TF
    nj = D_FF // tf
    cb = conv_b.reshape(1, -1)
    gate = lambda shape: pl.BlockSpec(shape, lambda j: (0, j))
    val = lambda shape: pl.BlockSpec(shape, lambda j: (0, nj + j))
    return pl.pallas_call(
        functools.partial(_ffn_sample_kernel, tm=tm, seq=seq),
        grid=(nj,),
        in_specs=[pl.BlockSpec((tm, k), lambda j: (0, 0)),
                  pl.BlockSpec((1, k), lambda j: (0, 0)),
                  gate((k, tf)), val((k, tf)), gate((CONV_W, tf)), val((CONV_W, tf)),
                  gate((1, tf)), val((1, tf)),
                  pl.BlockSpec((tf, k), lambda j: (j, 0)),
                  gate((tm, tf)), gate((tm, tf)), val((tm, tf)), val((tm, tf))],
        out_specs=[pl.BlockSpec((tm, k), lambda j: (0, 0)), gate((tm, tf)), gate((tm, tf))],
        out_shape=[jax.ShapeDtypeStruct((m, k), F32),
                   jax.ShapeDtypeStruct((m, D_FF), F32),
                   jax.ShapeDtypeStruct((m, D_FF), F32)],
        scratch_shapes=[pltpu.VMEM((tm, k), BF16), pltpu.VMEM((tm, k), F32),
                        pltpu.VMEM((tm + HALO, tf), F32), pltpu.VMEM((tm + HALO, tf), F32)],
        compiler_params=_cparams("arbitrary"),
        name="ffn_sample",
    )(x, g.reshape(1, k), w_up, w_up, conv_w, conv_w, cb, cb, w_down, st1, st2, st1, st2)


def _rot_cols(w):
    half = QK_ROPE // 2
    return jnp.concatenate([-w[..., half:], w[..., :half]], axis=-1)


def _rope_tables(pos):
    half = QK_ROPE // 2
    inv = ROPE_THETA ** (-jnp.arange(half, dtype=F32) / half)
    ang = pos.astype(F32)[:, None] * inv[None, :]
    cos = jnp.cos(ang)
    sin = jnp.sin(ang)
    zeros = jnp.zeros((pos.shape[0], LANES - QK_ROPE), F32)
    return (jnp.concatenate([cos, cos, zeros], axis=1), jnp.concatenate([sin, sin, zeros], axis=1))


def _pair_state_in(s):
    b = s.shape[0]
    m = jnp.swapaxes(s, -1, -2).reshape(b, H_B // 2, 2, HD_B, HD_B)
    z = jnp.zeros_like(m[:, :, 0])
    top = jnp.concatenate([m[:, :, 0], z], axis=-1)
    bot = jnp.concatenate([z, m[:, :, 1]], axis=-1)
    return jnp.concatenate([top, bot], axis=-2)


def _pair_state_out(m):
    b = m.shape[0]
    h0 = m[:, :, :HD_B, :HD_B]
    h1 = m[:, :, HD_B:, HD_B:]
    s = jnp.stack([h0, h1], axis=2).reshape(b, H_B, HD_B, HD_B)
    return jnp.swapaxes(s, -1, -2)


def _band_bias_prompt(table):
    t = BAND_Q
    span = BAND_SEGS * t
    nd = t + span - 1
    d = (t - 1) - jnp.arange(nd + 1)
    w = table[:, jnp.clip(d + BAND, -REL_CLIP, REL_CLIP) + REL_CLIP]
    rows = jnp.tile(w, (1, t))[:, :t * nd].reshape(table.shape[0], t, nd)
    bias = rows[:, :, t - 1:t - 1 + span]
    q = jnp.arange(t)[:, None]
    j = jnp.arange(span)[None, :]
    lo = (q // CHUNK) * CHUNK
    allowed = (j >= lo) & (j < lo + BAND + CHUNK)
    return jnp.where(allowed[None], bias, NEG_BIG).astype(F32)


def _band_bias_sample(table, seq, past):
    q_pos = PAST_LEN + jnp.arange(seq)
    k_pos = PAST_LEN - past + jnp.arange(past + seq)
    idx = jnp.clip(q_pos[:, None] - k_pos[None, :], -REL_CLIP, REL_CLIP) + REL_CLIP
    bias = table[:, idx].astype(F32)
    return bias[:, :, :past], bias[:, :, past:]


def _last_rows(proj, nseq, seq):
    last = proj.reshape(nseq, seq, -1)[:, -1]
    return jnp.concatenate([last[:, :3 * C_B], last[:, COL_WA:]], axis=1)


def _even_layer(xp, xs, e, rope_p, rope_s, dims, caches, prm):
    bp, tp, bs, ts = dims
    cache_ckv, cache_kpe, state_rwkv, state_shift = caches
    w_in = prm["ev_w_in"][e]
    wa_cols = w_in[:, :Q_LORA + KV_LORA]
    w_kpe = w_in[:, Q_LORA + KV_LORA:Q_LORA + KV_LORA + QK_ROPE]
    w_b = w_in[:, Q_LORA + KV_LORA + QK_ROPE:]
    w_ext = jnp.concatenate([w_b[:, :3 * C_B], wa_cols, w_kpe, _rot_cols(w_kpe), w_b[:, 3 * C_B:]],
                            axis=1).astype(BF16)
    g_mix = prm["norm_mix"][2 * e]
    proj_p = norm_matmul(xp, g_mix, w_ext)
    proj_s = norm_matmul(xs, g_mix, w_ext)

    wq = prm["ev_w_uq"][e].reshape(Q_LORA, H_A, QK_NOPE + QK_ROPE)
    wq_pe = wq[:, :, QK_NOPE:]
    wq_ext = jnp.concatenate([wq[:, :, :QK_NOPE], wq_pe, _rot_cols(wq_pe)], axis=-1)
    wq_ext = wq_ext.reshape(Q_LORA, H_A * HEAD_W).astype(BF16)
    wkv = prm["ev_w_ukv"][e].astype(BF16)
    qn, kvn = prm["ev_q_norm"][e], prm["ev_kv_norm"][e]
    q_p, kx_p, v_p, ckv_p, kpe_p = mla_prep(proj_p, *rope_p, qn, kvn, wq_ext, wkv)
    q_s, _, _, ckv_s, kpe_s = mla_prep(proj_s, *rope_s, qn, kvn, wq_ext, wkv)
    o_a_p = mla_prompt(q_p, kx_p, v_p, bp, tp)
    o_a_s = mla_sample(q_s, cache_ckv[e], cache_kpe[e], ckv_s, kpe_s, wkv, bs, ts)

    mu = prm["ev_mu"][e]
    mu_parts = [mu[None, 0:C_B], mu[None, C_B:2 * C_B], mu[None, 2 * C_B:3 * C_B],
                mu[None, 3 * C_B:3 * C_B + LANES], mu[None, 3 * C_B + LANES:]]
    zw = jnp.zeros((W_LORA, C_B), F32)
    ww2p = jnp.concatenate([prm["ev_w_w2"][e], zw], axis=0).astype(BF16)
    wa2p = jnp.concatenate([zw, prm["ev_w_a2"][e]], axis=0).astype(BF16)
    prep_w = (mu_parts, prm["ev_w0"][e], ww2p, prm["ev_a0"][e], wa2p, prm["ev_w_g2"][e].astype(BF16),
              prm["ev_k_k"][e], prm["ev_k_a"][e])
    acts_p = rwkv_prep(proj_p, None, *prep_w, tp)
    acts_s = rwkv_prep(proj_s, jnp.repeat(state_shift[e], ts, axis=0), *prep_w, ts)
    r_k = prm["ev_r_k"][e].reshape(-1)
    lnw, lnb = prm["ev_lnx_w"][e], prm["ev_lnx_b"][e]
    m0_p = jnp.zeros((bp, H_B // 2, LANES, LANES), F32)
    o_b_p, m_p = rwkv_chunk(*acts_p, r_k, lnw, lnb, m0_p, 0, bp, tp, min(CHUNK, tp), RWKV_PAIRS_PROMPT)
    o_b_s, m_s = rwkv_chunk(*acts_s, r_k, lnw, lnb, _pair_state_in(state_rwkv[e]), 0, bs, ts,
                            min(CHUNK, ts), RWKV_PAIRS_SAMPLE)

    w_out = prm["ev_w_out"][e].astype(BF16)
    w1, w2 = w_out[:H_A * V_A], w_out[H_A * V_A:]
    xp = out_proj(xp, o_a_p, o_b_p, w1, w2)
    xs = out_proj(xs, o_a_s, o_b_s, w1, w2)
    outs = dict(
        ckv_p=ckv_p.reshape(bp, tp, KV_LORA), ckv_s=ckv_s.reshape(bs, ts, KV_LORA),
        kpe_p=kpe_p.reshape(bp, tp, QK_ROPE), kpe_s=kpe_s.reshape(bs, ts, QK_ROPE),
        rw_p=_pair_state_out(m_p), rw_s=_pair_state_out(m_s),
        sh_p=_last_rows(proj_p, bp, tp), sh_s=_last_rows(proj_s, bs, ts))
    return xp, xs, outs


def _odd_layer(xp, xs, o, dims, caches, prm):
    bp, tp, bs, ts = dims
    cache_k, cache_v, state_gla = caches
    band_past = cache_k.shape[2]
    w_in = prm["od_w_in"][o]
    c_gk = 3 * C_C + 2 * H_D * DK_D + H_D * DV_D
    w_ext = jnp.concatenate([w_in[:, :c_gk], w_in[:, c_gk + GK_LORA:], w_in[:, c_gk:c_gk + GK_LORA],
                             jnp.zeros((D_MODEL, LANES - GK_LORA), F32)], axis=1).astype(BF16)
    g_mix = prm["norm_mix"][2 * o + 1]
    proj_p = norm_matmul(xp, g_mix, w_ext)
    proj_s = norm_matmul(xs, g_mix, w_ext)

    table = prm["od_rel_bias"][o]
    o_c_p = band_prompt(proj_p, _band_bias_prompt(table), bp, tp)
    bias_p, bias_n = _band_bias_sample(table, ts, band_past)
    k_past = cache_k[o].reshape(bs, band_past, C_C)
    v_past = cache_v[o].reshape(bs, band_past, C_C)
    o_c_s = band_sample(proj_s, k_past, v_past, bias_p, bias_n, bs, ts)

    wgk = jnp.concatenate([prm["od_w_gk2"][o], jnp.zeros((LANES - GK_LORA, H_D * DK_D), F32)],
                          axis=0).astype(BF16)
    bgk, gn = prm["od_b_gk"][o], prm["od_gnorm"][o]
    s0_p = jnp.zeros((bp, H_D, DK_D, DV_D), F32)
    o_d_p, s_p = gla(proj_p, wgk, bgk, gn, s0_p, 0, bp, tp, min(CHUNK, tp))
    o_d_s, s_s = gla(proj_s, wgk, bgk, gn, state_gla[o], 0, bs, ts, min(CHUNK, ts))

    w_out = prm["od_w_out"][o].astype(BF16)
    xp = out_proj(xp, o_c_p, o_d_p, w_out[:C_C], w_out[C_C:])
    xs = out_proj(xs, o_c_s, o_d_s, w_out[:C_C], w_out[C_C:])

    def tail_prompt(col):
        rows = proj_p.reshape(bp, tp, -1)[:, max(tp - band_past, 0):, col * C_C:(col + 1) * C_C]
        rows = rows.reshape(bp, -1, H_C, HD_C)
        return jnp.pad(rows, ((0, 0), (max(band_past - tp, 0), 0), (0, 0), (0, 0)))

    def tail_sample(cache, col):
        new = proj_s[:, col * C_C:(col + 1) * C_C].reshape(bs, ts, H_C, HD_C)
        return jnp.concatenate([cache, new], axis=1)[:, ts:]

    outs = dict(bk_p=tail_prompt(1), bv_p=tail_prompt(2),
                bk_s=tail_sample(cache_k[o], 1), bv_s=tail_sample(cache_v[o], 2),
                gla_p=s_p, gla_s=s_s)
    return xp, xs, outs


def _ffn_layer(xp, xs, layer, dims, state_conv, prm):
    bp, tp, bs, ts = dims
    g = prm["norm_ffn"][layer]
    w_up = prm["ffn_w_up"][layer].astype(BF16)
    w_down = prm["ffn_w_down"][layer].astype(BF16)
    cw, cb = prm["ffn_conv_w"][layer], prm["ffn_conv_b"][layer]
    xp, tg, tv = ffn_prompt(xp, g, w_up, cw, cb, w_down, tp)
    bps = tp // FFN_TM
    tails = jnp.concatenate([tg.reshape(bp, bps, 8, D_FF)[:, -1, 8 - (CONV_W - 1):],
                             tv.reshape(bp, bps, 8, D_FF)[:, -1, 8 - (CONV_W - 1):]], axis=-1)

    st = state_conv[layer]
    zeros = jnp.zeros((bs, ts - 1, 2 * D_FF), F32)
    st1 = jnp.concatenate([st[:, 1:2], zeros], axis=1).reshape(bs * ts, 2 * D_FF)
    st2 = jnp.concatenate([st, zeros[:, 1:]], axis=1).reshape(bs * ts, 2 * D_FF)
    xs, ug, uv = ffn_sample(xs, g, w_up, cw, cb, w_down, st1, st2, ts)
    u_s = jnp.concatenate([ug, uv], axis=-1).reshape(bs, ts, 2 * D_FF)
    ext = jnp.concatenate([st, u_s], axis=1)[:, ts:]
    return xp, xs, tails, ext


@jax.jit
def kernel(x_prompt, x_sample, cache_mla_ckv, cache_mla_kpe, state_rwkv, state_rwkv_shift, cache_band_k, cache_band_v, state_gla, state_ffn_conv, norm_mix, norm_ffn, norm_final, ev_w_in, ev_q_norm, ev_w_uq, ev_kv_norm, ev_w_ukv, ev_mu, ev_w0, ev_w_w2, ev_a0, ev_w_a2, ev_w_g2, ev_k_k, ev_k_a, ev_r_k, ev_lnx_w, ev_lnx_b, ev_w_out, od_w_in, od_rel_bias, od_w_gk2, od_b_gk, od_gnorm, od_w_out, ffn_w_up, ffn_conv_w, ffn_conv_b, ffn_w_down):
    prm = dict(norm_mix=norm_mix, norm_ffn=norm_ffn, ev_w_in=ev_w_in, ev_q_norm=ev_q_norm, ev_w_uq=ev_w_uq,
               ev_kv_norm=ev_kv_norm, ev_w_ukv=ev_w_ukv, ev_mu=ev_mu, ev_w0=ev_w0, ev_w_w2=ev_w_w2,
               ev_a0=ev_a0, ev_w_a2=ev_w_a2, ev_w_g2=ev_w_g2, ev_k_k=ev_k_k, ev_k_a=ev_k_a, ev_r_k=ev_r_k,
               ev_lnx_w=ev_lnx_w, ev_lnx_b=ev_lnx_b, ev_w_out=ev_w_out, od_w_in=od_w_in,
               od_rel_bias=od_rel_bias, od_w_gk2=od_w_gk2, od_b_gk=od_b_gk, od_gnorm=od_gnorm,
               od_w_out=od_w_out, ffn_w_up=ffn_w_up, ffn_conv_w=ffn_conv_w, ffn_conv_b=ffn_conv_b,
               ffn_w_down=ffn_w_down)
    bp, tp, _ = x_prompt.shape
    bs, ts, _ = x_sample.shape
    dims = (bp, tp, bs, ts)
    depth = norm_mix.shape[0]
    xp = x_prompt.reshape(bp * tp, D_MODEL)
    xs = x_sample.reshape(bs * ts, D_MODEL)
    rope_p = _rope_tables(jnp.tile(jnp.arange(tp), bp))
    rope_s = _rope_tables(jnp.tile(PAST_LEN + jnp.arange(ts), bs))

    ev, od, ffn_p, ffn_s = [], [], [], []
    for layer in range(depth):
        if layer % 2 == 0:
            xp, xs, outs = _even_layer(xp, xs, layer // 2, rope_p, rope_s, dims,
                                       (cache_mla_ckv, cache_mla_kpe, state_rwkv, state_rwkv_shift), prm)
            ev.append(outs)
        else:
            xp, xs, outs = _odd_layer(xp, xs, layer // 2, dims, (cache_band_k, cache_band_v, state_gla), prm)
            od.append(outs)
        xp, xs, tails, ext = _ffn_layer(xp, xs, layer, dims, state_ffn_conv, prm)
        ffn_p.append(tails)
        ffn_s.append(ext)

    yp = final_norm(xp, norm_final)
    ys = final_norm(xs, norm_final)
    stack = lambda lst, key: jnp.stack([d[key] for d in lst])
    return (yp.reshape(bp, tp, D_MODEL), ys.reshape(bs, ts, D_MODEL),
            stack(ev, "ckv_p"), stack(ev, "ckv_s"), stack(ev, "kpe_p"), stack(ev, "kpe_s"),
            stack(ev, "rw_p"), stack(ev, "rw_s"), stack(ev, "sh_p"), stack(ev, "sh_s"),
            stack(od, "bk_p"), stack(od, "bk_s"), stack(od, "bv_p"), stack(od, "bv_s"),
            stack(od, "gla_p"), stack(od, "gla_s"), jnp.stack(ffn_p), jnp.stack(ffn_s))
```

```python
import functools

import jax
import jax.numpy as jnp
from jax import lax
from jax.experimental import pallas as pl
from jax.experimental.pallas import tpu as pltpu

F32 = jnp.float32
BF16 = jnp.bfloat16

D_MODEL = 2048
CHUNK = 64
EPS = 1e-6
PAST_LEN = 1024

H_A = 8
QK_NOPE = 128
QK_ROPE = 64
V_A = 128
Q_LORA = 512
KV_LORA = 256
ROPE_THETA = 10000.0

H_B = 16
HD_B = 64
C_B = H_B * HD_B
W_LORA = 64
A_LORA = 64
G_LORA = 128
LNX_EPS = 64e-5

H_C = 16
HD_C = 64
C_C = H_C * HD_C
BAND = 8 * CHUNK
REL_CLIP = 128

H_D = 4
DK_D = 128
DV_D = 256
GK_LORA = 16
GATE_NORM = 16.0

D_FF = 5632
CONV_W = 3

LANES = 128
VMEM_LIMIT_BYTES = 60 * 2 ** 20
NEG_BIG = -1e30
ROW_TILE = 256
MLA_SCALE = (QK_NOPE + QK_ROPE) ** -0.5


def _cparams(*sem):
    return pltpu.CompilerParams(dimension_semantics=sem, vmem_limit_bytes=VMEM_LIMIT_BYTES)


def _resident(shape, index_map):
    return pl.BlockSpec(shape, index_map, pipeline_mode=pl.Buffered(1))


def _dot(a, b):
    return jnp.dot(a.astype(BF16), b.astype(BF16), preferred_element_type=F32)


def _dot_nt(a, b):
    return lax.dot_general(a.astype(BF16), b.astype(BF16), (((1,), (1,)), ((), ())),
                           preferred_element_type=F32)


def _dot_tn(a, b):
    return lax.dot_general(a.astype(BF16), b.astype(BF16), (((0,), (0,)), ((), ())),
                           preferred_element_type=F32)


def _rms(x, g):
    return x * lax.rsqrt(jnp.mean(x * x, axis=-1, keepdims=True) + EPS) * g


def _sigmoid(x):
    return 1.0 / (1.0 + jnp.exp(-x))


def _softplus(x):
    return jnp.maximum(x, 0.0) + jnp.log(1.0 + jnp.exp(-jnp.abs(x)))


def _cumsum_rows(x):
    n = x.shape[0]
    row = lax.broadcasted_iota(jnp.int32, x.shape, 0)
    s = 1
    while s < n:
        x = x + jnp.where(row >= s, pltpu.roll(x, s, 0), 0.0)
        s *= 2
    return x


def _norm_matmul_kernel(x_ref, g_ref, w_ref, o_ref):
    o_ref[...] = _dot(_rms(x_ref[...], g_ref[...]), w_ref[...])


def norm_matmul(x, g, w):
    m, k = x.shape
    n = w.shape[1]
    tm = ROW_TILE
    return pl.pallas_call(
        _norm_matmul_kernel,
        grid=(m // tm,),
        in_specs=[pl.BlockSpec((tm, k), lambda i: (i, 0)),
                  pl.BlockSpec((1, k), lambda i: (0, 0)),
                  _resident((k, n), lambda i: (0, 0))],
        out_specs=pl.BlockSpec((tm, n), lambda i: (i, 0)),
        out_shape=jax.ShapeDtypeStruct((m, n), F32),
        compiler_params=_cparams("parallel"),
        name="norm_matmul",
    )(x, g.reshape(1, k), w)


def _out_proj_kernel(res_ref, a1_ref, a2_ref, w1_ref, w2_ref, o_ref):
    o_ref[...] = res_ref[...] + _dot(a1_ref[...], w1_ref[...]) + _dot(a2_ref[...], w2_ref[...])


def out_proj(res, a1, a2, w_out, e):
    m, n = res.shape
    ka = a1.shape[1]
    tm = ROW_TILE
    return pl.pallas_call(
        _out_proj_kernel,
        grid=(m // tm,),
        in_specs=[pl.BlockSpec((tm, n), lambda i: (i, 0)),
                  pl.BlockSpec((tm, ka), lambda i: (i, 0)),
                  pl.BlockSpec((tm, ka), lambda i: (i, 0)),
                  _resident((None, ka, n), lambda i: (e, 0, 0)),
                  _resident((None, ka, n), lambda i: (e, 1, 0))],
        out_specs=pl.BlockSpec((tm, n), lambda i: (i, 0)),
        out_shape=jax.ShapeDtypeStruct((m, n), F32),
        compiler_params=_cparams("parallel"),
        name="out_proj",
    )(res, a1, a2, w_out, w_out)


COL_CQ = 3 * C_B
COL_CKV = COL_CQ + Q_LORA
COL_KPE = COL_CKV + KV_LORA
COL_WA = COL_KPE + 2 * QK_ROPE
COL_G = COL_WA + W_LORA + A_LORA
COLS_EVEN_EXT = COL_G + G_LORA
HEAD_W = 2 * LANES


def _mla_prep_kernel(cq_ref, ckv_ref, kpe_ref, cos_ref, sin_ref, qn_ref, kvn_ref, wq_ref, wkv_ref,
                     q_ref, kx_ref, v_ref, ckv_out_ref, kpe_out_ref):
    cos = cos_ref[...]
    sin = sin_ref[...]
    z = _dot(_rms(cq_ref[...], qn_ref[...]), wq_ref[...])
    ckvn = _rms(ckv_ref[...], kvn_ref[...])
    ckv_out_ref[...] = ckvn
    kp = kpe_ref[...]
    kr = kp * cos + pltpu.roll(kp, QK_ROPE, 1) * sin
    kpe_out_ref[...] = kr[:, :QK_ROPE]
    kr16 = kr.astype(BF16)
    kv = _dot(ckvn, wkv_ref[...])
    for h in range(H_A):
        c0 = h * HEAD_W
        q_ref[h, :, 0:LANES] = (z[:, c0:c0 + LANES] * MLA_SCALE).astype(BF16)
        t2 = z[:, c0 + LANES:c0 + HEAD_W]
        q_ref[h, :, LANES:HEAD_W] = ((t2 * cos + pltpu.roll(t2, QK_ROPE, 1) * sin) * MLA_SCALE).astype(BF16)
        kx_ref[h, :, 0:LANES] = kv[:, c0:c0 + LANES].astype(BF16)
        kx_ref[h, :, LANES:HEAD_W] = kr16
        v_ref[h] = kv[:, c0 + LANES:c0 + HEAD_W].astype(BF16)


def mla_prep(proj, cos, sin, q_norm, kv_norm, wq, wkv):
    m = proj.shape[0]
    tm = ROW_TILE
    return pl.pallas_call(
        _mla_prep_kernel,
        grid=(m // tm,),
        in_specs=[pl.BlockSpec((tm, Q_LORA), lambda i: (i, COL_CQ // Q_LORA)),
                  pl.BlockSpec((tm, KV_LORA), lambda i: (i, COL_CKV // KV_LORA)),
                  pl.BlockSpec((tm, LANES), lambda i: (i, COL_KPE // LANES)),
                  pl.BlockSpec((tm, LANES), lambda i: (i, 0)),
                  pl.BlockSpec((tm, LANES), lambda i: (i, 0)),
                  pl.BlockSpec((1, Q_LORA), lambda i: (0, 0)),
                  pl.BlockSpec((1, KV_LORA), lambda i: (0, 0)),
                  _resident((Q_LORA, H_A * HEAD_W), lambda i: (0, 0)),
                  _resident((KV_LORA, H_A * HEAD_W), lambda i: (0, 0))],
        out_specs=[pl.BlockSpec((H_A, tm, HEAD_W), lambda i: (0, i, 0)),
                   pl.BlockSpec((H_A, tm, HEAD_W), lambda i: (0, i, 0)),
                   pl.BlockSpec((H_A, tm, V_A), lambda i: (0, i, 0)),
                   pl.BlockSpec((tm, KV_LORA), lambda i: (i, 0)),
                   pl.BlockSpec((tm, QK_ROPE), lambda i: (i, 0))],
        out_shape=[jax.ShapeDtypeStruct((H_A, m, HEAD_W), BF16),
                   jax.ShapeDtypeStruct((H_A, m, HEAD_W), BF16),
                   jax.ShapeDtypeStruct((H_A, m, V_A), BF16),
                   jax.ShapeDtypeStruct((m, KV_LORA), F32),
                   jax.ShapeDtypeStruct((m, QK_ROPE), F32)],
        compiler_params=_cparams("parallel"),
        name="mla_prep",
    )(proj, proj, proj, cos, sin, q_norm.reshape(1, -1), kv_norm.reshape(1, -1), wq, wkv)


ATT_BLOCK = 512
MLA_HEADS_PER_STEP = 2


def _mla_prompt_kernel(q_ref, k_ref, v_ref, o_ref):
    t = ATT_BLOCK
    qi = pl.program_id(2)

    def block(j, carry, masked):
        start = pl.multiple_of(j * t, t)
        heads = range(MLA_HEADS_PER_STEP)
        s = [lax.dot_general(q_ref[g], k_ref[g, pl.ds(start, t), :], (((1,), (1,)), ((), ())),
                             preferred_element_type=F32) for g in heads]
        if masked:
            qc = lax.broadcasted_iota(jnp.int32, (t, t), 0) // CHUNK
            kc = lax.broadcasted_iota(jnp.int32, (t, t), 1) // CHUNK
            s = [jnp.where(kc <= qc, x, NEG_BIG) for x in s]
        m_new = [jnp.maximum(carry[g][0], jnp.max(s[g], axis=-1, keepdims=True)) for g in heads]
        p = [jnp.exp(s[g] - m_new[g]) for g in heads]
        pv = [jnp.dot(p[g].astype(BF16), v_ref[g, pl.ds(start, t), :], preferred_element_type=F32)
              for g in heads]
        out = []
        for g in heads:
            m, l, acc = carry[g]
            alpha = jnp.exp(m - m_new[g])
            out.append((m_new[g], alpha * l + jnp.sum(p[g], axis=-1, keepdims=True), alpha * acc + pv[g]))
        return tuple(out)

    init = tuple((jnp.full((t, 1), NEG_BIG, F32), jnp.zeros((t, 1), F32), jnp.zeros((t, V_A), F32))
                 for _ in range(MLA_HEADS_PER_STEP))
    carry = lax.fori_loop(0, qi, lambda j, c: block(j, c, False), init)
    carry = block(qi, carry, True)
    for g in range(MLA_HEADS_PER_STEP):
        m, l, acc = carry[g]
        o_ref[:, g * V_A:(g + 1) * V_A] = (acc / l).astype(BF16)


def mla_prompt(q, kx, v, batch, seq):
    t = ATT_BLOCK
    g = MLA_HEADS_PER_STEP
    assert seq % t == 0 and H_A % g == 0
    nq = seq // t
    return pl.pallas_call(
        _mla_prompt_kernel,
        grid=(batch, H_A // g, nq),
        in_specs=[pl.BlockSpec((g, t, HEAD_W), lambda b, h, i: (h, b * nq + i, 0)),
                  pl.BlockSpec((g, seq, HEAD_W), lambda b, h, i: (h, b, 0)),
                  pl.BlockSpec((g, seq, V_A), lambda b, h, i: (h, b, 0))],
        out_specs=pl.BlockSpec((t, g * V_A), lambda b, h, i: (b * nq + i, h)),
        out_shape=jax.ShapeDtypeStruct((batch * seq, H_A * V_A), BF16),
        compiler_params=_cparams("parallel", "parallel", "arbitrary"),
        name="mla_prompt",
    )(q, kx, v)


def _mla_sample_kernel(q_ref, ckvp_ref, kpep_ref, ckvn_ref, kpen_ref, wkv_ref, o_ref):
    seq = ckvn_ref.shape[0]
    ckvp = ckvp_ref[0].astype(BF16)
    ckvn = ckvn_ref[...].astype(BF16)
    kpep = kpep_ref[0].astype(BF16)
    kpen = kpen_ref[...].astype(BF16)
    heads = range(H_A)
    q_lat = jnp.concatenate([_dot_nt(q_ref[h, :, 0:QK_NOPE], wkv_ref[:, h * HEAD_W:h * HEAD_W + QK_NOPE])
                             for h in heads], axis=0)
    q_pe = jnp.concatenate([q_ref[h, :, LANES:LANES + QK_ROPE] for h in heads], axis=0)
    s_p = _dot_nt(q_lat, ckvp) + _dot_nt(q_pe, kpep)
    s_n = _dot_nt(q_lat, ckvn) + _dot_nt(q_pe, kpen)
    m = jnp.maximum(jnp.max(s_p, axis=-1, keepdims=True), jnp.max(s_n, axis=-1, keepdims=True))
    p_p = jnp.exp(s_p - m)
    p_n = jnp.exp(s_n - m)
    l = jnp.sum(p_p, axis=-1, keepdims=True) + jnp.sum(p_n, axis=-1, keepdims=True)
    o_lat = (_dot(p_p, ckvp) + _dot(p_n, ckvn)) / l
    for h in heads:
        o_ref[:, h * V_A:(h + 1) * V_A] = _dot(
            o_lat[h * seq:(h + 1) * seq], wkv_ref[:, h * HEAD_W + QK_NOPE:(h + 1) * HEAD_W]).astype(BF16)


def mla_sample(q, ckv_past, kpe_past, ckv_new, kpe_new, wkv, batch, seq):
    past = ckv_past.shape[1]
    return pl.pallas_call(
        _mla_sample_kernel,
        grid=(batch,),
        in_specs=[pl.BlockSpec((H_A, seq, HEAD_W), lambda b: (0, b, 0)),
                  pl.BlockSpec((1, past, KV_LORA), lambda b: (b, 0, 0)),
                  pl.BlockSpec((1, past, QK_ROPE), lambda b: (b, 0, 0)),
                  pl.BlockSpec((seq, KV_LORA), lambda b: (b, 0)),
                  pl.BlockSpec((seq, QK_ROPE), lambda b: (b, 0)),
                  _resident((KV_LORA, H_A * HEAD_W), lambda b: (0, 0))],
        out_specs=pl.BlockSpec((seq, H_A * V_A), lambda b: (b, 0)),
        out_shape=jax.ShapeDtypeStruct((batch * seq, H_A * V_A), BF16),
        compiler_params=_cparams("parallel"),
        name="mla_sample",
    )(q, ckv_past, kpe_past, ckv_new, kpe_new, wkv)


def _rwkv_prep_kernel(r_ref, k_ref, v_ref, wa_ref, g_ref, rp_ref, kp_ref, vp_ref, wap_ref, gp_ref,
                      mu_r_ref, mu_k_ref, mu_v_ref, mu_wa_ref, mu_g_ref,
                      w0_ref, ww2_ref, a0_ref, wa2_ref, wg2_ref, kk_ref, ka_ref,
                      r_out, lw_out, k_out, v_out, kk_out, a_out, g_out, *, tm, seq, first_rows):
    row = lax.broadcasted_iota(jnp.int32, (tm, 1), 0)
    if not first_rows:
        keep = ((pl.program_id(0) * tm) % seq != 0).astype(F32)

    def mix(p_ref, pp_ref, mu_ref):
        p = p_ref[...]
        rolled = pltpu.roll(p, 1, 0)
        if first_rows:
            prev = jnp.where(row % seq == 0, pp_ref[...], rolled)
        else:
            prev = jnp.where(row == 0, pp_ref[7:8, :] * keep, rolled)
        return p + (prev - p) * mu_ref[...]

    r_out[...] = mix(r_ref, rp_ref, mu_r_ref)
    v_out[...] = mix(v_ref, vp_ref, mu_v_ref)
    k = mix(k_ref, kp_ref, mu_k_ref)
    xwa = mix(wa_ref, wap_ref, mu_wa_ref)
    xg = mix(g_ref, gp_ref, mu_g_ref)
    w_log = -_softplus(-(w0_ref[...] + _dot(jnp.tanh(xwa), ww2_ref[...]))) - 0.5
    lw_out[...] = -jnp.exp(w_log)
    a = _sigmoid(a0_ref[...] + _dot(xwa, wa2_ref[...]))
    a_out[...] = a
    g_out[...] = _dot(_sigmoid(xg), wg2_ref[...])
    kk_out[...] = k * kk_ref[...]
    k_out[...] = k * (1.0 + (a - 1.0) * ka_ref[...])


def rwkv_prep(proj, first, mu_parts, w0, ww2p, a0, wa2p, wg2, k_k, k_a, seq):
    m = proj.shape[0]
    tm = ROW_TILE
    wide = lambda c: pl.BlockSpec((tm, C_B), lambda i: (i, c))
    narrow = lambda c: pl.BlockSpec((tm, LANES), lambda i: (i, c))
    vec = lambda n: pl.BlockSpec((1, n), lambda i: (0, 0))
    out = jax.ShapeDtypeStruct((m, C_B), F32)
    if first is None:
        assert seq % tm == 0
        before = lambda i: jnp.maximum(i * (tm // 8) - 1, 0)
        prev_specs = [pl.BlockSpec((8, C_B), lambda i, c=c: (before(i), c)) for c in range(3)]
        prev_specs += [pl.BlockSpec((8, LANES), lambda i, c=c: (before(i), c))
                       for c in (COL_WA // LANES, COL_G // LANES)]
        prev = proj
    else:
        assert tm % seq == 0
        prev_specs = [wide(0), wide(1), wide(2), narrow(3 * C_B // LANES), narrow(3 * C_B // LANES + 1)]
        prev = first
    return pl.pallas_call(
        functools.partial(_rwkv_prep_kernel, tm=tm, seq=seq, first_rows=first is not None),
        grid=(m // tm,),
        in_specs=[wide(0), wide(1), wide(2), narrow(COL_WA // LANES), narrow(COL_G // LANES),
                  *prev_specs,
                  vec(C_B), vec(C_B), vec(C_B), vec(LANES), vec(LANES),
                  vec(C_B), _resident((LANES, C_B), lambda i: (0, 0)),
                  vec(C_B), _resident((LANES, C_B), lambda i: (0, 0)),
                  _resident((G_LORA, C_B), lambda i: (0, 0)), vec(C_B), vec(C_B)],
        out_specs=[pl.BlockSpec((tm, C_B), lambda i: (i, 0))] * 7,
        out_shape=[out] * 7,
        compiler_params=_cparams("parallel"),
        name="rwkv_prep",
    )(proj, proj, proj, proj, proj, prev, prev, prev, prev, prev,
      *mu_parts, w0.reshape(1, -1), ww2p, a0.reshape(1, -1), wa2p, wg2,
      k_k.reshape(1, -1), k_a.reshape(1, -1))


RWKV_PAIRS_PROMPT = 8
RWKV_PAIRS_SAMPLE = 8


def _rwkv_chunk_kernel(r_ref, lw_ref, k_ref, v_ref, kk_ref, a_ref, g_ref, rk_ref, lnw_ref, lnb_ref,
                       m0_ref, o_ref, mout_ref, m_scr, *, chunk, pairs):
    c = pl.program_id(2)
    nc = pl.num_programs(2)
    L = chunk
    L2 = 2 * L

    @pl.when(c == 0)
    def _():
        m_scr[...] = m0_ref[0]

    lane = lax.broadcasted_iota(jnp.int32, (L, LANES), 1)
    low = lane < HD_B
    row_p = lax.broadcasted_iota(jnp.int32, (L, L2), 0)
    col_p = lax.broadcasted_iota(jnp.int32, (L, L2), 1) % L
    strict = row_p > col_p
    incl = row_p >= col_p
    eye_p = (row_p == col_p).astype(F32)
    same_blk = ((lax.broadcasted_iota(jnp.int32, (L2, L2), 0) < L)
                == (lax.broadcasted_iota(jnp.int32, (L2, L2), 1) < L))
    r128 = lax.broadcasted_iota(jnp.int32, (LANES, LANES), 0)
    c128 = lax.broadcasted_iota(jnp.int32, (LANES, LANES), 1)
    same_head = (r128 < HD_B) == (c128 < HD_B)
    diag = r128 == c128

    def seg_sum(x):
        s0 = jnp.sum(jnp.where(low, x, 0.0), axis=-1, keepdims=True)
        s1 = jnp.sum(jnp.where(low, 0.0, x), axis=-1, keepdims=True)
        return jnp.where(low, s0, s1)

    def split_heads(x):
        return jnp.concatenate([jnp.where(low, x, 0.0), jnp.where(low, 0.0, x)], axis=0)

    def block_diag(x):
        return jnp.where(same_blk, jnp.concatenate([x, x], axis=0), 0.0)

    P = range(pairs)
    sls = [slice(p * LANES, (p + 1) * LANES) for p in P]

    def prep(sl):
        r = r_ref[:, sl]
        lw = lw_ref[:, sl]
        k = k_ref[:, sl]
        kkr = kk_ref[:, sl]
        kk = kkr / jnp.maximum(jnp.sqrt(seg_sum(kkr * kkr)), 1e-12)
        bvec = kk * a_ref[:, sl]
        ci = _cumsum_rows(lw)
        c_end = ci[L - 1:L, :]
        e_neg = jnp.exp(-ci)
        e_end = jnp.exp(c_end - ci)
        return dict(r=r, k=k, v=v_ref[:, sl], c_end=c_end,
                    at=-kk * jnp.exp(ci - lw), bt=bvec * e_neg, kt=k * e_neg, rt=r * jnp.exp(ci),
                    bh=bvec * e_end, kh=k * e_end)

    d = [prep(sl) for sl in sls]
    sc = [_dot_nt(jnp.concatenate([x["at"], x["rt"]], axis=0),
                  jnp.concatenate([split_heads(x["bt"]), split_heads(x["kt"])], axis=0)) for x in d]
    nab = [jnp.where(strict, s[0:L, 0:L2], 0.0) for s in sc]
    nak = [jnp.where(strict, s[0:L, L2:2 * L2], 0.0) for s in sc]
    lrb = [jnp.where(incl, s[L:L2, 0:L2], 0.0) for s in sc]
    lrk = [jnp.where(incl, s[L:L2, L2:2 * L2], 0.0) for s in sc]
    v_split = [split_heads(x["v"]) for x in d]

    t_inv = [eye_p + n for n in nab]
    pw = [_dot(n, block_diag(n)) for n in nab]
    nakv = [_dot(nak[p], v_split[p]) for p in P]
    m = 2
    while m < L:
        if 2 * m < L:
            res = [_dot(pw[p], jnp.concatenate([block_diag(pw[p]), block_diag(t_inv[p])], axis=1)) for p in P]
            pw = [x[:, 0:L2] for x in res]
            t_inv = [t_inv[p] + res[p][:, L2:2 * L2] for p in P]
        else:
            t_inv = [t_inv[p] + _dot(pw[p], block_diag(t_inv[p])) for p in P]
        m *= 2
    yk = [_dot(lrk[p], v_split[p]) for p in P]
    hk = [_dot_tn(d[p]["kh"], d[p]["v"]) for p in P]

    wu = [_dot(t_inv[p], jnp.concatenate([split_heads(d[p]["at"]), split_heads(nakv[p])], axis=1)) for p in P]
    qy = [_dot(lrb[p], jnp.concatenate([split_heads(wu[p][:, 0:LANES]), split_heads(wu[p][:, LANES:])], axis=1))
          for p in P]
    gh = [_dot_tn(d[p]["bh"], wu[p]) for p in P]
    q = [d[p]["rt"] + qy[p][:, 0:LANES] for p in P]
    gmat = [jnp.where(diag, jnp.broadcast_to(jnp.exp(d[p]["c_end"]), (LANES, LANES)), 0.0)
            + jnp.where(same_head, gh[p][:, 0:LANES], 0.0) for p in P]
    ym = [_dot(jnp.concatenate([q[p], gmat[p]], axis=0), m_scr[p]) for p in P]
    for p in P:
        m_scr[p] = ym[p][L:L + LANES] + jnp.where(same_head, gh[p][:, LANES:] + hk[p], 0.0)
    for p in P:
        sl = sls[p]
        y = ym[p][0:L] + qy[p][:, LANES:] + yk[p]
        mean = seg_sum(y) * (1.0 / HD_B)
        dev = y - mean
        var = seg_sum(dev * dev) * (1.0 / HD_B)
        yn = dev * lax.rsqrt(var + LNX_EPS) * lnw_ref[:, sl] + lnb_ref[:, sl]
        bonus = seg_sum(d[p]["r"] * d[p]["k"] * rk_ref[:, sl]) * d[p]["v"]
        o_ref[:, sl] = ((yn + bonus) * g_ref[:, sl]).astype(BF16)

    @pl.when(c == nc - 1)
    def _():
        mout_ref[0] = m_scr[...]


def rwkv_chunk(r, lw, k, v, kk, a, g, r_k, lnx_w, lnx_b, m0, row0, batch, seq, chunk, pairs):
    nc = seq // chunk
    npair = C_B // LANES
    ngrp = npair // pairs
    w = pairs * LANES
    blk0 = row0 // chunk
    row = lambda b, pg, c: (blk0 + b * nc + c, pg)
    par = lambda b, pg, c: (0, pg)
    act = pl.BlockSpec((chunk, w), row)
    prm = pl.BlockSpec((1, w), par)
    st = pl.BlockSpec((1, pairs, LANES, LANES), lambda b, pg, c: (b, pg, 0, 0))
    return pl.pallas_call(
        functools.partial(_rwkv_chunk_kernel, chunk=chunk, pairs=pairs),
        grid=(batch, ngrp, nc),
        in_specs=[act] * 7 + [prm] * 3 + [st],
        out_specs=[pl.BlockSpec((chunk, w), lambda b, pg, c: (b * nc + c, pg)), st],
        out_shape=[jax.ShapeDtypeStruct((batch * seq, C_B), BF16),
                   jax.ShapeDtypeStruct((batch, npair, LANES, LANES), F32)],
        scratch_shapes=[pltpu.VMEM((pairs, LANES, LANES), F32)],
        compiler_params=_cparams("parallel", "parallel", "arbitrary"),
        name="rwkv_chunk",
    )(r, lw, k, v, kk, a, g, r_k.reshape(1, -1), lnx_w.reshape(1, -1), lnx_b.reshape(1, -1), m0)


BAND_Q = 256
BAND_SEGS = BAND // BAND_Q + 1


def _band_prompt_kernel(q_ref, k_ref, v_ref, bias_ref, o_ref):
    qb = pl.program_id(2)
    t = BAND_Q
    q = q_ref[...] * HD_C ** -0.5
    lane = lax.broadcasted_iota(jnp.int32, (t, LANES), 1)
    low = lane < HD_C
    ks, vs, valid = [], [], []
    for seg in range(BAND_SEGS):
        kb = qb - (BAND_SEGS - 1) + seg
        start = pl.multiple_of(jnp.maximum(kb, 0) * t, t)
        ks.append(k_ref[pl.ds(start, t), :].astype(BF16))
        vs.append(v_ref[pl.ds(start, t), :].astype(BF16))
        valid.append(kb >= 0)
    heads = range(2)
    segs = range(BAND_SEGS)
    qh = [jnp.where(low if hh == 0 else jnp.logical_not(low), q, 0.0).astype(BF16) for hh in heads]
    s = [[jnp.where(valid[seg],
                    lax.dot_general(qh[hh], ks[seg], (((1,), (1,)), ((), ())), preferred_element_type=F32)
                    + bias_ref[hh, :, seg * t:(seg + 1) * t], NEG_BIG)
          for seg in segs] for hh in heads]
    m = [functools.reduce(jnp.maximum, [jnp.max(s[hh][seg], axis=-1, keepdims=True) for seg in segs])
         for hh in heads]
    p = [[jnp.exp(s[hh][seg] - m[hh]) for seg in segs] for hh in heads]
    l = [sum(jnp.sum(p[hh][seg], axis=-1, keepdims=True) for seg in segs) for hh in heads]
    acc = [sum(jnp.dot(p[hh][seg].astype(BF16), vs[seg], preferred_element_type=F32) for seg in segs)
           for hh in heads]
    o_ref[...] = jnp.where(low, acc[0] / l[0], acc[1] / l[1]).astype(BF16)


def band_prompt(proj, bias, batch, seq):
    t = BAND_Q
    nq = seq // t
    npair = C_C // LANES
    return pl.pallas_call(
        _band_prompt_kernel,
        grid=(npair, batch, nq),
        in_specs=[pl.BlockSpec((t, LANES), lambda hp, b, i: (b * nq + i, hp)),
                  pl.BlockSpec((seq, LANES), lambda hp, b, i: (b, npair + hp)),
                  pl.BlockSpec((seq, LANES), lambda hp, b, i: (b, 2 * npair + hp)),
                  pl.BlockSpec((2, t, BAND_SEGS * t), lambda hp, b, i: (hp, 0, 0))],
        out_specs=pl.BlockSpec((t, LANES), lambda hp, b, i: (b * nq + i, hp)),
        out_shape=jax.ShapeDtypeStruct((batch * seq, C_C), BF16),
        compiler_params=_cparams("parallel", "parallel", "parallel"),
        name="band_prompt",
    )(proj, proj, proj, bias)


def _band_sample_kernel(q_ref, kn_ref, vn_ref, kp_ref, vp_ref, biasp_ref, biasn_ref, o_ref):
    t = q_ref.shape[0]
    lane = lax.broadcasted_iota(jnp.int32, (t, LANES), 1)
    low = lane < HD_C
    for hp in range(C_C // LANES):
        sl = slice(hp * LANES, (hp + 1) * LANES)
        q = q_ref[:, sl] * HD_C ** -0.5
        kp = kp_ref[0, :, sl].astype(BF16)
        vp = vp_ref[0, :, sl].astype(BF16)
        kn = kn_ref[:, sl].astype(BF16)
        vn = vn_ref[:, sl].astype(BF16)
        outs = []
        for hh in range(2):
            qh = jnp.where(low if hh == 0 else jnp.logical_not(low), q, 0.0)
            s_p = _dot_nt(qh, kp) + biasp_ref[2 * hp + hh]
            s_n = _dot_nt(qh, kn) + biasn_ref[2 * hp + hh]
            m = jnp.maximum(jnp.max(s_p, axis=-1, keepdims=True), jnp.max(s_n, axis=-1, keepdims=True))
            p_p = jnp.exp(s_p - m)
            p_n = jnp.exp(s_n - m)
            l = jnp.sum(p_p, axis=-1, keepdims=True) + jnp.sum(p_n, axis=-1, keepdims=True)
            outs.append((_dot(p_p, vp) + _dot(p_n, vn)) / l)
        o_ref[:, sl] = jnp.where(low, outs[0], outs[1]).astype(BF16)


def band_sample(proj, k_past, v_past, bias_p, bias_n, batch, seq):
    past = k_past.shape[1]
    new = lambda col: pl.BlockSpec((seq, C_C), lambda b: (b, col))
    old = pl.BlockSpec((1, past, C_C), lambda b: (b, 0, 0))
    return pl.pallas_call(
        _band_sample_kernel,
        grid=(batch,),
        in_specs=[new(0), new(1), new(2), old, old,
                  _resident((H_C, seq, past), lambda b: (0, 0, 0)),
                  _resident((H_C, seq, seq), lambda b: (0, 0, 0))],
        out_specs=pl.BlockSpec((seq, C_C), lambda b: (b, 0)),
        out_shape=jax.ShapeDtypeStruct((batch * seq, C_C), BF16),
        compiler_params=_cparams("parallel"),
        name="band_sample",
    )(proj, proj, proj, k_past, v_past, bias_p, bias_n)


def _gla_kernel(q_ref, k_ref, v_ref, go_ref, gk_ref, wgk_ref, bgk_ref, gn_ref, s0_ref,
                o_ref, sout_ref, s_scr, *, chunk):
    c = pl.program_id(1)
    nc = pl.num_programs(1)
    L = chunk

    @pl.when(c == 0)
    def _():
        s_scr[...] = s0_ref[0]

    causal = (lax.broadcasted_iota(jnp.int32, (L, L), 0) >= lax.broadcasted_iota(jnp.int32, (L, L), 1))
    diag = (lax.broadcasted_iota(jnp.int32, (DK_D, DK_D), 0)
            == lax.broadcasted_iota(jnp.int32, (DK_D, DK_D), 1))
    H = range(H_D)
    ksl = [slice(h * DK_D, (h + 1) * DK_D) for h in H]
    vsl = [slice(h * DV_D, (h + 1) * DV_D) for h in H]
    x = _dot(gk_ref[...], wgk_ref[...]) + bgk_ref[...]
    b = _cumsum_rows(-_softplus(-x) / GATE_NORM)
    eb = jnp.exp(b)
    enb = jnp.exp(-b)
    b_end = b[L - 1:L, :]
    ed = jnp.exp(b_end - b)
    e_end = jnp.exp(b_end)
    qe = [q_ref[:, ksl[h]] * DK_D ** -0.5 * eb[:, ksl[h]] for h in H]
    vh = [v_ref[:, vsl[h]].astype(BF16) for h in H]
    att = [jnp.where(causal, _dot_nt(qe[h], k_ref[:, ksl[h]] * enb[:, ksl[h]]), 0.0) for h in H]
    s_prev = [s_scr[h] for h in H]
    o = [_dot(att[h], vh[h]) + _dot(qe[h], s_prev[h]) for h in H]
    kv = [_dot_tn(k_ref[:, ksl[h]] * ed[:, ksl[h]], vh[h]) for h in H]
    for h in H:
        e_col = jnp.sum(jnp.where(diag, jnp.broadcast_to(e_end[:, ksl[h]], (DK_D, DK_D)), 0.0),
                        axis=1, keepdims=True)
        s_scr[h] = s_prev[h] * e_col + kv[h]
    for h in H:
        on = o[h] * lax.rsqrt(jnp.mean(o[h] * o[h], axis=-1, keepdims=True) + EPS) * gn_ref[...]
        gate = go_ref[:, vsl[h]]
        o_ref[:, vsl[h]] = (on * (gate * _sigmoid(gate))).astype(BF16)

    @pl.when(c == nc - 1)
    def _():
        sout_ref[0] = s_scr[...]


COL_QD = 3 * C_C
COL_KD = COL_QD + H_D * DK_D
COL_VD = COL_KD + H_D * DK_D
COL_GO = COL_VD + H_D * DV_D
COL_GK = COL_GO + H_D * DV_D
COLS_ODD_EXT = COL_GK + LANES


def gla(proj, wgk, bgk, gnorm, s0, row0, batch, seq, chunk):
    nc = seq // chunk
    kw = H_D * DK_D
    vw = H_D * DV_D
    blk0 = row0 // chunk
    row = lambda col: (lambda b, c: (blk0 + b * nc + c, col))
    return pl.pallas_call(
        functools.partial(_gla_kernel, chunk=chunk),
        grid=(batch, nc),
        in_specs=[pl.BlockSpec((chunk, kw), row(COL_QD // kw)),
                  pl.BlockSpec((chunk, kw), row(COL_KD // kw)),
                  pl.BlockSpec((chunk, vw), row(COL_VD // vw)),
                  pl.BlockSpec((chunk, vw), row(COL_GO // vw)),
                  pl.BlockSpec((chunk, LANES), row(COL_GK // LANES)),
                  _resident((LANES, kw), lambda b, c: (0, 0)),
                  pl.BlockSpec((1, kw), lambda b, c: (0, 0)),
                  pl.BlockSpec((1, DV_D), lambda b, c: (0, 0)),
                  pl.BlockSpec((1, H_D, DK_D, DV_D), lambda b, c: (b, 0, 0, 0))],
        out_specs=[pl.BlockSpec((chunk, vw), lambda b, c: (b * nc + c, 0)),
                   pl.BlockSpec((1, H_D, DK_D, DV_D), lambda b, c: (b, 0, 0, 0))],
        out_shape=[jax.ShapeDtypeStruct((batch * seq, vw), BF16),
                   jax.ShapeDtypeStruct((batch, H_D, DK_D, DV_D), F32)],
        scratch_shapes=[pltpu.VMEM((H_D, DK_D, DV_D), F32)],
        compiler_params=_cparams("parallel", "arbitrary"),
        name="gla",
    )(proj, proj, proj, proj, proj, wgk, bgk.reshape(1, -1), gnorm.reshape(1, -1), s0)


FFN_TM = 512
FFN_TF = 512
HALO = 16


def _conv3(u_scr, cw_ref, cb_ref, cols, tm, prev1=None, prev2=None):
    u0 = u_scr[HALO:HALO + tm, cols]
    u1 = u_scr[HALO - 1:HALO - 1 + tm, cols]
    u2 = u_scr[HALO - 2:HALO - 2 + tm, cols]
    if prev1 is not None:
        u1 = prev1(u1)
        u2 = prev2(u2)
    return cb_ref[:, cols] + cw_ref[2:3, cols] * u0 + cw_ref[1:2, cols] * u1 + cw_ref[0:1, cols] * u2


def _ffn_prompt_kernel(x_ref, a1_ref, a2_ref, xh_ref, a1h_ref, a2h_ref, w1_ref, w2_ref, g_ref, gf_ref,
                       wg_ref, wv_ref, cwg_ref, cwv_ref, cbg_ref, cbv_ref, wd_ref,
                       o_ref, tg_ref, tv_ref, h_scr, *u_scrs, tm, blocks_per_seq, final_norm_out):
    i = pl.program_id(0)
    j = pl.program_id(1)
    nj = pl.num_programs(1)

    @pl.when(j == 0)
    def _():
        w1 = w1_ref[...]
        w2 = w2_ref[...]
        xm = x_ref[...] + _dot(a1_ref[...], w1) + _dot(a2_ref[...], w2)
        o_ref[...] = xm
        h_scr[HALO:HALO + tm, :] = _rms(xm, g_ref[...]).astype(BF16)
        keep = (i % blocks_per_seq != 0).astype(F32)
        xmh = xh_ref[...] + _dot(a1h_ref[...], w1) + _dot(a2h_ref[...], w2)
        h_scr[0:HALO, :] = (_rms(xmh, g_ref[...]) * keep).astype(BF16)

    h = h_scr[...]
    ug_scr, uv_scr = u_scrs
    ug_scr[...] = jnp.dot(h, wg_ref[...], preferred_element_type=F32)
    uv_scr[...] = jnp.dot(h, wv_ref[...], preferred_element_type=F32)
    acts = []
    for c in range(0, ug_scr.shape[1], LANES):
        lanes = slice(c, c + LANES)
        cg = _conv3(ug_scr, cwg_ref, cbg_ref, lanes, tm)
        cv = _conv3(uv_scr, cwv_ref, cbv_ref, lanes, tm)
        acts.append((cg * _sigmoid(cg) * cv).astype(BF16))
    o_ref[...] += jnp.dot(jnp.concatenate(acts, axis=1), wd_ref[...], preferred_element_type=F32)
    tg_ref[0] = ug_scr[HALO + tm - 8:HALO + tm, :]
    tv_ref[0] = uv_scr[HALO + tm - 8:HALO + tm, :]

    if final_norm_out:
        @pl.when(j == nj - 1)
        def _():
            o_ref[...] = _rms(o_ref[...], gf_ref[...])


def ffn_prompt(x, a1, a2, w_out, e, g, w_up, w_down, layer, conv_w, conv_b, seq, final_g=None):
    m, k = x.shape
    ka = a1.shape[1]
    tm, tf = FFN_TM, FFN_TF
    nj = D_FF // tf
    nblk = m // tm
    bps = seq // tm
    halo_blocks = tm // HALO
    cb = conv_b.reshape(1, -1)
    gf = (g if final_g is None else final_g).reshape(1, k)
    before = lambda i, j: (jnp.maximum(i * halo_blocks - 1, 0), 0)
    return pl.pallas_call(
        functools.partial(_ffn_prompt_kernel, tm=tm, blocks_per_seq=bps, final_norm_out=final_g is not None),
        grid=(nblk, nj),
        in_specs=[pl.BlockSpec((tm, k), lambda i, j: (i, 0)),
                  pl.BlockSpec((tm, ka), lambda i, j: (i, 0)),
                  pl.BlockSpec((tm, ka), lambda i, j: (i, 0)),
                  pl.BlockSpec((HALO, k), before),
                  pl.BlockSpec((HALO, ka), before),
                  pl.BlockSpec((HALO, ka), before),
                  _resident((None, ka, k), lambda i, j: (e, 0, 0)),
                  _resident((None, ka, k), lambda i, j: (e, 1, 0)),
                  pl.BlockSpec((1, k), lambda i, j: (0, 0)),
                  pl.BlockSpec((1, k), lambda i, j: (0, 0)),
                  pl.BlockSpec((None, k, tf), lambda i, j: (layer, 0, j)),
                  pl.BlockSpec((None, k, tf), lambda i, j: (layer, 0, nj + j)),
                  pl.BlockSpec((CONV_W, tf), lambda i, j: (0, j)),
                  pl.BlockSpec((CONV_W, tf), lambda i, j: (0, nj + j)),
                  pl.BlockSpec((1, tf), lambda i, j: (0, j)),
                  pl.BlockSpec((1, tf), lambda i, j: (0, nj + j)),
                  pl.BlockSpec((None, tf, k), lambda i, j: (layer, j, 0))],
        out_specs=[pl.BlockSpec((tm, k), lambda i, j: (i, 0)),
                   pl.BlockSpec((1, 8, tf), lambda i, j: (i, 0, j)),
                   pl.BlockSpec((1, 8, tf), lambda i, j: (i, 0, j))],
        out_shape=[jax.ShapeDtypeStruct((m, k), F32),
                   jax.ShapeDtypeStruct((nblk, 8, D_FF), F32),
                   jax.ShapeDtypeStruct((nblk, 8, D_FF), F32)],
        scratch_shapes=[pltpu.VMEM((tm + HALO, k), BF16)] + [pltpu.VMEM((tm + HALO, tf), F32)] * 2,
        compiler_params=_cparams("parallel", "arbitrary"),
        name="ffn_prompt",
    )(x, a1, a2, x, a1, a2, w_out, w_out, g.reshape(1, k), gf, w_up, w_up, conv_w, conv_w, cb, cb, w_down)


def _ffn_sample_kernel(x_ref, g_ref, gf_ref, wg_ref, wv_ref, cwg_ref, cwv_ref, cbg_ref, cbv_ref, wd_ref,
                       sg_ref, sv_ref,
                       o_ref, ug_ref, uv_ref, h_scr, acc_scr, ug_scr, uv_scr, *, tm, seq, final_norm_out):
    j = pl.program_id(0)
    nj = pl.num_programs(0)
    nseq = tm // seq
    tf = ug_scr.shape[1]
    pos = lax.broadcasted_iota(jnp.int32, (nseq, seq, 1), 1)
    cols = slice(0, tf)

    def before(s_ref, back):
        def fix(u):
            u = u.reshape(nseq, seq, tf)
            for p in range(back):
                u = jnp.where(pos == p, s_ref[:, CONV_W - 1 - back + p:CONV_W - back + p, :], u)
            return u.reshape(tm, tf)
        return fix

    @pl.when(j == 0)
    def _():
        h_scr[...] = _rms(x_ref[...], g_ref[...]).astype(BF16)
        acc_scr[...] = jnp.zeros_like(acc_scr)
        ug_scr[0:HALO, :] = jnp.zeros((HALO, ug_scr.shape[1]), F32)
        uv_scr[0:HALO, :] = jnp.zeros((HALO, uv_scr.shape[1]), F32)

    h = h_scr[...]
    ug = jnp.dot(h, wg_ref[...], preferred_element_type=F32)
    uv = jnp.dot(h, wv_ref[...], preferred_element_type=F32)
    ug_scr[HALO:HALO + tm, :] = ug
    uv_scr[HALO:HALO + tm, :] = uv
    ug_ref[...] = ug
    uv_ref[...] = uv
    cg = _conv3(ug_scr, cwg_ref, cbg_ref, cols, tm, before(sg_ref, 1), before(sg_ref, 2))
    cv = _conv3(uv_scr, cwv_ref, cbv_ref, cols, tm, before(sv_ref, 1), before(sv_ref, 2))
    act = cg * _sigmoid(cg) * cv
    acc_scr[...] += _dot(act, wd_ref[...])

    @pl.when(j == nj - 1)
    def _():
        out = x_ref[...] + acc_scr[...]
        o_ref[...] = _rms(out, gf_ref[...]) if final_norm_out else out


def ffn_sample(x, g, w_up, w_down, layer, conv_w, conv_b, state, seq, final_g=None):
    m, k = x.shape
    tm, tf = m, FFN_TF
    nj = D_FF // tf
    nseq = m // seq
    cb = conv_b.reshape(1, -1)
    gf = (g if final_g is None else final_g).reshape(1, k)
    gate = lambda shape: pl.BlockSpec(shape, lambda j: (0, j))
    val = lambda shape: pl.BlockSpec(shape, lambda j: (0, nj + j))
    carried = lambda off: pl.BlockSpec((nseq, CONV_W - 1, tf), lambda j: (0, 0, off + j))
    return pl.pallas_call(
        functools.partial(_ffn_sample_kernel, tm=tm, seq=seq, final_norm_out=final_g is not None),
        grid=(nj,),
        in_specs=[pl.BlockSpec((tm, k), lambda j: (0, 0)),
                  pl.BlockSpec((1, k), lambda j: (0, 0)),
                  pl.BlockSpec((1, k), lambda j: (0, 0)),
                  pl.BlockSpec((None, k, tf), lambda j: (layer, 0, j)),
                  pl.BlockSpec((None, k, tf), lambda j: (layer, 0, nj + j)),
                  gate((CONV_W, tf)), val((CONV_W, tf)),
                  gate((1, tf)), val((1, tf)),
                  pl.BlockSpec((None, tf, k), lambda j: (layer, j, 0)),
                  carried(0), carried(nj)],
        out_specs=[pl.BlockSpec((tm, k), lambda j: (0, 0)), gate((tm, tf)), gate((tm, tf))],
        out_shape=[jax.ShapeDtypeStruct((m, k), F32),
                   jax.ShapeDtypeStruct((m, D_FF), F32),
                   jax.ShapeDtypeStruct((m, D_FF), F32)],
        scratch_shapes=[pltpu.VMEM((tm, k), BF16), pltpu.VMEM((tm, k), F32),
                        pltpu.VMEM((tm + HALO, tf), F32), pltpu.VMEM((tm + HALO, tf), F32)],
        compiler_params=_cparams("arbitrary"),
        name="ffn_sample",
    )(x, g.reshape(1, k), gf, w_up, w_up, conv_w, conv_w, cb, cb, w_down, state, state)


def _rot_cols(w):
    half = QK_ROPE // 2
    return jnp.concatenate([-w[..., half:], w[..., :half]], axis=-1)


def _rope_tables(pos):
    half = QK_ROPE // 2
    inv = ROPE_THETA ** (-jnp.arange(half, dtype=F32) / half)
    ang = pos.astype(F32)[:, None] * inv[None, :]
    cos = jnp.cos(ang)
    sin = jnp.sin(ang)
    zeros = jnp.zeros((pos.shape[0], LANES - QK_ROPE), F32)
    return (jnp.concatenate([cos, cos, zeros], axis=1), jnp.concatenate([sin, sin, zeros], axis=1))


def _pair_state_in(s):
    b = s.shape[0]
    m = jnp.swapaxes(s, -1, -2).reshape(b, H_B // 2, 2, HD_B, HD_B)
    z = jnp.zeros_like(m[:, :, 0])
    top = jnp.concatenate([m[:, :, 0], z], axis=-1)
    bot = jnp.concatenate([z, m[:, :, 1]], axis=-1)
    return jnp.concatenate([top, bot], axis=-2)


def _pair_state_out(m):
    b = m.shape[0]
    h0 = m[:, :, :HD_B, :HD_B]
    h1 = m[:, :, HD_B:, HD_B:]
    s = jnp.stack([h0, h1], axis=2).reshape(b, H_B, HD_B, HD_B)
    return jnp.swapaxes(s, -1, -2)


def _band_bias_prompt(table):
    t = BAND_Q
    span = BAND_SEGS * t
    nd = t + span - 1
    d = (t - 1) - jnp.arange(nd + 1)
    w = table[:, jnp.clip(d + BAND, -REL_CLIP, REL_CLIP) + REL_CLIP]
    rows = jnp.tile(w, (1, t))[:, :t * nd].reshape(table.shape[0], t, nd)
    bias = rows[:, :, t - 1:t - 1 + span]
    q = jnp.arange(t)[:, None]
    j = jnp.arange(span)[None, :]
    lo = (q // CHUNK) * CHUNK
    allowed = (j >= lo) & (j < lo + BAND + CHUNK)
    return jnp.where(allowed[None], bias, NEG_BIG).astype(F32)


def _band_bias_sample(table, seq, past):
    q_pos = PAST_LEN + jnp.arange(seq)
    k_pos = PAST_LEN - past + jnp.arange(past + seq)
    idx = jnp.clip(q_pos[:, None] - k_pos[None, :], -REL_CLIP, REL_CLIP) + REL_CLIP
    bias = table[:, idx].astype(F32)
    return bias[:, :, :past], bias[:, :, past:]


def _last_rows(proj, nseq, seq):
    last = proj.reshape(nseq, seq, -1)[:, -1]
    return jnp.concatenate([last[:, :3 * C_B], last[:, COL_WA:]], axis=1)


def _even_layer(xp, xs, e, rope_p, rope_s, dims, caches, prm):
    bp, tp, bs, ts = dims
    cache_ckv, cache_kpe, state_rwkv, state_shift = caches
    w_in = prm["ev_w_in"][e]
    wa_cols = w_in[:, :Q_LORA + KV_LORA]
    w_kpe = w_in[:, Q_LORA + KV_LORA:Q_LORA + KV_LORA + QK_ROPE]
    w_b = w_in[:, Q_LORA + KV_LORA + QK_ROPE:]
    w_ext = jnp.concatenate([w_b[:, :3 * C_B], wa_cols, w_kpe, _rot_cols(w_kpe), w_b[:, 3 * C_B:]],
                            axis=1).astype(BF16)
    g_mix = prm["norm_mix"][2 * e]
    proj_p = norm_matmul(xp, g_mix, w_ext)
    proj_s = norm_matmul(xs, g_mix, w_ext)

    wq = prm["ev_w_uq"][e].reshape(Q_LORA, H_A, QK_NOPE + QK_ROPE)
    wq_pe = wq[:, :, QK_NOPE:]
    wq_ext = jnp.concatenate([wq[:, :, :QK_NOPE], wq_pe, _rot_cols(wq_pe)], axis=-1)
    wq_ext = wq_ext.reshape(Q_LORA, H_A * HEAD_W).astype(BF16)
    wkv = prm["ev_w_ukv"][e].astype(BF16)
    qn, kvn = prm["ev_q_norm"][e], prm["ev_kv_norm"][e]
    q_p, kx_p, v_p, ckv_p, kpe_p = mla_prep(proj_p, *rope_p, qn, kvn, wq_ext, wkv)
    q_s, _, _, ckv_s, kpe_s = mla_prep(proj_s, *rope_s, qn, kvn, wq_ext, wkv)
    o_a_p = mla_prompt(q_p, kx_p, v_p, bp, tp)
    o_a_s = mla_sample(q_s, cache_ckv[e], cache_kpe[e], ckv_s, kpe_s, wkv, bs, ts)

    mu = prm["ev_mu"][e]
    mu_parts = [mu[None, 0:C_B], mu[None, C_B:2 * C_B], mu[None, 2 * C_B:3 * C_B],
                mu[None, 3 * C_B:3 * C_B + LANES], mu[None, 3 * C_B + LANES:]]
    zw = jnp.zeros((W_LORA, C_B), F32)
    ww2p = jnp.concatenate([prm["ev_w_w2"][e], zw], axis=0).astype(BF16)
    wa2p = jnp.concatenate([zw, prm["ev_w_a2"][e]], axis=0).astype(BF16)
    prep_w = (mu_parts, prm["ev_w0"][e], ww2p, prm["ev_a0"][e], wa2p, prm["ev_w_g2"][e].astype(BF16),
              prm["ev_k_k"][e], prm["ev_k_a"][e])
    acts_p = rwkv_prep(proj_p, None, *prep_w, tp)
    acts_s = rwkv_prep(proj_s, jnp.repeat(state_shift[e], ts, axis=0), *prep_w, ts)
    r_k = prm["ev_r_k"][e].reshape(-1)
    lnw, lnb = prm["ev_lnx_w"][e], prm["ev_lnx_b"][e]
    m0_p = jnp.zeros((bp, H_B // 2, LANES, LANES), F32)
    o_b_p, m_p = rwkv_chunk(*acts_p, r_k, lnw, lnb, m0_p, 0, bp, tp, min(CHUNK, tp), RWKV_PAIRS_PROMPT)
    o_b_s, m_s = rwkv_chunk(*acts_s, r_k, lnw, lnb, _pair_state_in(state_rwkv[e]), 0, bs, ts,
                            min(CHUNK, ts), RWKV_PAIRS_SAMPLE)

    xs = out_proj(xs, o_a_s, o_b_s, prm["ev_w_out16"], e)
    outs = dict(
        ckv_p=ckv_p.reshape(bp, tp, KV_LORA), ckv_s=ckv_s.reshape(bs, ts, KV_LORA),
        kpe_p=kpe_p.reshape(bp, tp, QK_ROPE), kpe_s=kpe_s.reshape(bs, ts, QK_ROPE),
        rw_p=_pair_state_out(m_p), rw_s=_pair_state_out(m_s),
        sh_p=_last_rows(proj_p, bp, tp), sh_s=_last_rows(proj_s, bs, ts))
    return (o_a_p, o_b_p, prm["ev_w_out16"], e), xs, outs


def _odd_layer(xp, xs, o, dims, caches, prm):
    bp, tp, bs, ts = dims
    cache_k, cache_v, state_gla = caches
    band_past = cache_k.shape[2]
    w_in = prm["od_w_in"][o]
    c_gk = 3 * C_C + 2 * H_D * DK_D + H_D * DV_D
    w_ext = jnp.concatenate([w_in[:, :c_gk], w_in[:, c_gk + GK_LORA:], w_in[:, c_gk:c_gk + GK_LORA],
                             jnp.zeros((D_MODEL, LANES - GK_LORA), F32)], axis=1).astype(BF16)
    g_mix = prm["norm_mix"][2 * o + 1]
    proj_p = norm_matmul(xp, g_mix, w_ext)
    proj_s = norm_matmul(xs, g_mix, w_ext)

    table = prm["od_rel_bias"][o]
    o_c_p = band_prompt(proj_p, _band_bias_prompt(table), bp, tp)
    bias_p, bias_n = _band_bias_sample(table, ts, band_past)
    k_past = cache_k[o].reshape(bs, band_past, C_C)
    v_past = cache_v[o].reshape(bs, band_past, C_C)
    o_c_s = band_sample(proj_s, k_past, v_past, bias_p, bias_n, bs, ts)

    wgk = jnp.concatenate([prm["od_w_gk2"][o], jnp.zeros((LANES - GK_LORA, H_D * DK_D), F32)],
                          axis=0).astype(BF16)
    bgk, gn = prm["od_b_gk"][o], prm["od_gnorm"][o]
    s0_p = jnp.zeros((bp, H_D, DK_D, DV_D), F32)
    o_d_p, s_p = gla(proj_p, wgk, bgk, gn, s0_p, 0, bp, tp, min(CHUNK, tp))
    o_d_s, s_s = gla(proj_s, wgk, bgk, gn, state_gla[o], 0, bs, ts, min(CHUNK, ts))

    xs = out_proj(xs, o_c_s, o_d_s, prm["od_w_out16"], o)

    def tail_prompt(col):
        rows = proj_p.reshape(bp, tp, -1)[:, max(tp - band_past, 0):, col * C_C:(col + 1) * C_C]
        rows = rows.reshape(bp, -1, H_C, HD_C)
        return jnp.pad(rows, ((0, 0), (max(band_past - tp, 0), 0), (0, 0), (0, 0)))

    def tail_sample(cache, col):
        new = proj_s[:, col * C_C:(col + 1) * C_C].reshape(bs, ts, H_C, HD_C)
        return jnp.concatenate([cache[:, ts:], new], axis=1)

    outs = dict(bk_p=tail_prompt(1), bv_p=tail_prompt(2),
                bk_s=tail_sample(cache_k[o], 1), bv_s=tail_sample(cache_v[o], 2),
                gla_p=s_p, gla_s=s_s)
    return (o_c_p, o_d_p, prm["od_w_out16"], o), xs, outs


def _ffn_layer(xp, mix_p, xs, layer, dims, state_conv, prm, final_g):
    bp, tp, bs, ts = dims
    g = prm["norm_ffn"][layer]
    w_up, w_down = prm["ffn_w_up16"], prm["ffn_w_down16"]
    cw, cb = prm["ffn_conv_w"][layer], prm["ffn_conv_b"][layer]
    a1, a2, w_out, e = mix_p
    xp, tg, tv = ffn_prompt(xp, a1, a2, w_out, e, g, w_up, w_down, layer, cw, cb, tp, final_g)
    bps = tp // FFN_TM
    tails = jnp.concatenate([tg.reshape(bp, bps, 8, D_FF)[:, -1, 8 - (CONV_W - 1):],
                             tv.reshape(bp, bps, 8, D_FF)[:, -1, 8 - (CONV_W - 1):]], axis=-1)

    st = state_conv[layer]
    xs, ug, uv = ffn_sample(xs, g, w_up, w_down, layer, cw, cb, st, ts, final_g)
    keep = CONV_W - 1
    new = jnp.concatenate([ug.reshape(bs, ts, D_FF)[:, max(ts - keep, 0):],
                           uv.reshape(bs, ts, D_FF)[:, max(ts - keep, 0):]], axis=-1)
    ext = jnp.concatenate([st[:, ts:], new], axis=1)
    return xp, xs, tails, ext


@jax.jit
def kernel(x_prompt, x_sample, cache_mla_ckv, cache_mla_kpe, state_rwkv, state_rwkv_shift, cache_band_k, cache_band_v, state_gla, state_ffn_conv, norm_mix, norm_ffn, norm_final, ev_w_in, ev_q_norm, ev_w_uq, ev_kv_norm, ev_w_ukv, ev_mu, ev_w0, ev_w_w2, ev_a0, ev_w_a2, ev_w_g2, ev_k_k, ev_k_a, ev_r_k, ev_lnx_w, ev_lnx_b, ev_w_out, od_w_in, od_rel_bias, od_w_gk2, od_b_gk, od_gnorm, od_w_out, ffn_w_up, ffn_conv_w, ffn_conv_b, ffn_w_down):
    prm = dict(norm_mix=norm_mix, norm_ffn=norm_ffn, ev_w_in=ev_w_in, ev_q_norm=ev_q_norm, ev_w_uq=ev_w_uq,
               ev_kv_norm=ev_kv_norm, ev_w_ukv=ev_w_ukv, ev_mu=ev_mu, ev_w0=ev_w0, ev_w_w2=ev_w_w2,
               ev_a0=ev_a0, ev_w_a2=ev_w_a2, ev_w_g2=ev_w_g2, ev_k_k=ev_k_k, ev_k_a=ev_k_a, ev_r_k=ev_r_k,
               ev_lnx_w=ev_lnx_w, ev_lnx_b=ev_lnx_b, ev_w_out=ev_w_out, od_w_in=od_w_in,
               od_rel_bias=od_rel_bias, od_w_gk2=od_w_gk2, od_b_gk=od_b_gk, od_gnorm=od_gnorm,
               od_w_out=od_w_out, ffn_w_up=ffn_w_up, ffn_conv_w=ffn_conv_w, ffn_conv_b=ffn_conv_b,
               ffn_w_down=ffn_w_down)
    bp, tp, _ = x_prompt.shape
    bs, ts, _ = x_sample.shape
    dims = (bp, tp, bs, ts)
    depth = norm_mix.shape[0]
    xp = x_prompt.reshape(bp * tp, D_MODEL)
    xs = x_sample.reshape(bs * ts, D_MODEL)
    rope_p = _rope_tables(jnp.tile(jnp.arange(tp), bp))
    rope_s = _rope_tables(jnp.tile(PAST_LEN + jnp.arange(ts), bs))

    prm.update(ffn_w_up16=ffn_w_up.astype(BF16), ffn_w_down16=ffn_w_down.astype(BF16),
               ev_w_out16=ev_w_out.astype(BF16), od_w_out16=od_w_out.astype(BF16))

    ev, od, ffn_p, ffn_s = [], [], [], []
    for layer in range(depth):
        if layer % 2 == 0:
            mix_p, xs, outs = _even_layer(xp, xs, layer // 2, rope_p, rope_s, dims,
                                          (cache_mla_ckv, cache_mla_kpe, state_rwkv, state_rwkv_shift), prm)
            ev.append(outs)
        else:
            mix_p, xs, outs = _odd_layer(xp, xs, layer // 2, dims, (cache_band_k, cache_band_v, state_gla), prm)
            od.append(outs)
        final_g = norm_final if layer == depth - 1 else None
        xp, xs, tails, ext = _ffn_layer(xp, mix_p, xs, layer, dims, state_ffn_conv, prm, final_g)
        ffn_p.append(tails)
        ffn_s.append(ext)

    yp, ys = xp, xs
    stack = lambda lst, key: jnp.stack([d[key] for d in lst])
    return (yp.reshape(bp, tp, D_MODEL), ys.reshape(bs, ts, D_MODEL),
            stack(ev, "ckv_p"), stack(ev, "ckv_s"), stack(ev, "kpe_p"), stack(ev, "kpe_s"),
            stack(ev, "rw_p"), stack(ev, "rw_s"), stack(ev, "sh_p"), stack(ev, "sh_s"),
            stack(od, "bk_p"), stack(od, "bk_s"), stack(od, "bv_p"), stack(od, "bv_s"),
            stack(od, "gla_p"), stack(od, "gla_s"), jnp.stack(ffn_p), jnp.stack(ffn_s))
```

```python
import functools

import jax
import jax.numpy as jnp
from jax import lax
from jax.experimental import pallas as pl
from jax.experimental.pallas import tpu as pltpu

F32 = jnp.float32
BF16 = jnp.bfloat16

D_MODEL = 2048
CHUNK = 64
EPS = 1e-6
PAST_LEN = 1024

H_A = 8
QK_NOPE = 128
QK_ROPE = 64
V_A = 128
Q_LORA = 512
KV_LORA = 256
ROPE_THETA = 10000.0

H_B = 16
HD_B = 64
C_B = H_B * HD_B
W_LORA = 64
A_LORA = 64
G_LORA = 128
LNX_EPS = 64e-5

H_C = 16
HD_C = 64
C_C = H_C * HD_C
BAND = 8 * CHUNK
REL_CLIP = 128

H_D = 4
DK_D = 128
DV_D = 256
GK_LORA = 16
GATE_NORM = 16.0

D_FF = 5632
CONV_W = 3

LANES = 128
VMEM_LIMIT_BYTES = 60 * 2 ** 20
NEG_BIG = -1e30
ROW_TILE = 256
LOG2E = 1.4426950408889634
MLA_SCALE = (QK_NOPE + QK_ROPE) ** -0.5 * LOG2E
BAND_SCALE = HD_C ** -0.5 * LOG2E


def _cparams(*sem):
    return pltpu.CompilerParams(dimension_semantics=sem, vmem_limit_bytes=VMEM_LIMIT_BYTES)


def _resident(shape, index_map):
    return pl.BlockSpec(shape, index_map, pipeline_mode=pl.Buffered(1))


def _dot(a, b):
    return jnp.dot(a.astype(BF16), b.astype(BF16), preferred_element_type=F32)


def _dot_nt(a, b):
    return lax.dot_general(a.astype(BF16), b.astype(BF16), (((1,), (1,)), ((), ())),
                           preferred_element_type=F32)


def _dot_tn(a, b):
    return lax.dot_general(a.astype(BF16), b.astype(BF16), (((0,), (0,)), ((), ())),
                           preferred_element_type=F32)


def _rms(x, g):
    return x * lax.rsqrt(jnp.mean(x * x, axis=-1, keepdims=True) + EPS) * g


def _sigmoid(x):
    return 1.0 / (1.0 + jnp.exp(-x))


def _softplus(x):
    return jnp.maximum(x, 0.0) + jnp.log(1.0 + jnp.exp(-jnp.abs(x)))


def _cumsum_rows(x):
    n = x.shape[0]
    row = lax.broadcasted_iota(jnp.int32, x.shape, 0)
    s = 1
    while s < n:
        x = x + jnp.where(row >= s, pltpu.roll(x, s, 0), 0.0)
        s *= 2
    return x


def _norm_matmul_kernel(x_ref, g_ref, w_ref, o_ref):
    o_ref[...] = _dot(_rms(x_ref[...], g_ref[...]), w_ref[...])


def norm_matmul(x, g, w):
    m, k = x.shape
    n = w.shape[1]
    tm = ROW_TILE
    return pl.pallas_call(
        _norm_matmul_kernel,
        grid=(m // tm,),
        in_specs=[pl.BlockSpec((tm, k), lambda i: (i, 0)),
                  pl.BlockSpec((1, k), lambda i: (0, 0)),
                  _resident((k, n), lambda i: (0, 0))],
        out_specs=pl.BlockSpec((tm, n), lambda i: (i, 0)),
        out_shape=jax.ShapeDtypeStruct((m, n), F32),
        compiler_params=_cparams("parallel"),
        name="norm_matmul",
    )(x, g.reshape(1, k), w)


def _out_proj_kernel(res_ref, a1_ref, a2_ref, w1_ref, w2_ref, o_ref):
    o_ref[...] = res_ref[...] + _dot(a1_ref[...], w1_ref[...]) + _dot(a2_ref[...], w2_ref[...])


def out_proj(res, a1, a2, w_out, e):
    m, n = res.shape
    ka = a1.shape[1]
    tm = ROW_TILE
    return pl.pallas_call(
        _out_proj_kernel,
        grid=(m // tm,),
        in_specs=[pl.BlockSpec((tm, n), lambda i: (i, 0)),
                  pl.BlockSpec((tm, ka), lambda i: (i, 0)),
                  pl.BlockSpec((tm, ka), lambda i: (i, 0)),
                  _resident((None, ka, n), lambda i: (e, 0, 0)),
                  _resident((None, ka, n), lambda i: (e, 1, 0))],
        out_specs=pl.BlockSpec((tm, n), lambda i: (i, 0)),
        out_shape=jax.ShapeDtypeStruct((m, n), F32),
        compiler_params=_cparams("parallel"),
        name="out_proj",
    )(res, a1, a2, w_out, w_out)


COL_CQ = 3 * C_B
COL_CKV = COL_CQ + Q_LORA
COL_KPE = COL_CKV + KV_LORA
COL_WA = COL_KPE + 2 * QK_ROPE
COL_G = COL_WA + W_LORA + A_LORA
COLS_EVEN_EXT = COL_G + G_LORA
HEAD_W = 2 * LANES


def _mla_prep_kernel(cq_ref, ckv_ref, kpe_ref, cos_ref, sin_ref, qn_ref, kvn_ref, wq_ref, wkv_ref,
                     q_ref, kx_ref, v_ref, ckv_out_ref, kpe_out_ref):
    cos = cos_ref[...]
    sin = sin_ref[...]
    z = _dot(_rms(cq_ref[...], qn_ref[...]), wq_ref[...])
    ckvn = _rms(ckv_ref[...], kvn_ref[...])
    ckv_out_ref[...] = ckvn
    kp = kpe_ref[...]
    kr = kp * cos + pltpu.roll(kp, QK_ROPE, 1) * sin
    kpe_out_ref[...] = kr[:, :QK_ROPE]
    kr16 = kr.astype(BF16)
    kv = _dot(ckvn, wkv_ref[...])
    for h in range(H_A):
        c0 = h * HEAD_W
        q_ref[h, :, 0:LANES] = (z[:, c0:c0 + LANES] * MLA_SCALE).astype(BF16)
        t2 = z[:, c0 + LANES:c0 + HEAD_W]
        q_ref[h, :, LANES:HEAD_W] = ((t2 * cos + pltpu.roll(t2, QK_ROPE, 1) * sin) * MLA_SCALE).astype(BF16)
        kx_ref[h, :, 0:LANES] = kv[:, c0:c0 + LANES].astype(BF16)
        kx_ref[h, :, LANES:HEAD_W] = kr16
        v_ref[h] = kv[:, c0 + LANES:c0 + HEAD_W].astype(BF16)


def mla_prep(proj, cos, sin, q_norm, kv_norm, wq, wkv):
    m = proj.shape[0]
    tm = ROW_TILE
    return pl.pallas_call(
        _mla_prep_kernel,
        grid=(m // tm,),
        in_specs=[pl.BlockSpec((tm, Q_LORA), lambda i: (i, COL_CQ // Q_LORA)),
                  pl.BlockSpec((tm, KV_LORA), lambda i: (i, COL_CKV // KV_LORA)),
                  pl.BlockSpec((tm, LANES), lambda i: (i, COL_KPE // LANES)),
                  pl.BlockSpec((tm, LANES), lambda i: (i, 0)),
                  pl.BlockSpec((tm, LANES), lambda i: (i, 0)),
                  pl.BlockSpec((1, Q_LORA), lambda i: (0, 0)),
                  pl.BlockSpec((1, KV_LORA), lambda i: (0, 0)),
                  _resident((Q_LORA, H_A * HEAD_W), lambda i: (0, 0)),
                  _resident((KV_LORA, H_A * HEAD_W), lambda i: (0, 0))],
        out_specs=[pl.BlockSpec((H_A, tm, HEAD_W), lambda i: (0, i, 0)),
                   pl.BlockSpec((H_A, tm, HEAD_W), lambda i: (0, i, 0)),
                   pl.BlockSpec((H_A, tm, V_A), lambda i: (0, i, 0)),
                   pl.BlockSpec((tm, KV_LORA), lambda i: (i, 0)),
                   pl.BlockSpec((tm, QK_ROPE), lambda i: (i, 0))],
        out_shape=[jax.ShapeDtypeStruct((H_A, m, HEAD_W), BF16),
                   jax.ShapeDtypeStruct((H_A, m, HEAD_W), BF16),
                   jax.ShapeDtypeStruct((H_A, m, V_A), BF16),
                   jax.ShapeDtypeStruct((m, KV_LORA), F32),
                   jax.ShapeDtypeStruct((m, QK_ROPE), F32)],
        compiler_params=_cparams("parallel"),
        name="mla_prep",
    )(proj, proj, proj, cos, sin, q_norm.reshape(1, -1), kv_norm.reshape(1, -1), wq, wkv)


ATT_BLOCK = 512
MLA_HEADS_PER_STEP = 2


def _mla_prompt_kernel(q_ref, k_ref, v_ref, o_ref):
    t = ATT_BLOCK
    qi = pl.program_id(2)

    def block(j, carry, masked):
        start = pl.multiple_of(j * t, t)
        heads = range(MLA_HEADS_PER_STEP)
        s = [lax.dot_general(q_ref[g], k_ref[g, pl.ds(start, t), :], (((1,), (1,)), ((), ())),
                             preferred_element_type=F32) for g in heads]
        if masked:
            qc = lax.broadcasted_iota(jnp.int32, (t, t), 0) // CHUNK
            kc = lax.broadcasted_iota(jnp.int32, (t, t), 1) // CHUNK
            s = [jnp.where(kc <= qc, x, NEG_BIG) for x in s]
        m_new = [jnp.maximum(carry[g][0], jnp.max(s[g], axis=-1, keepdims=True)) for g in heads]
        p = [jnp.exp2(s[g] - m_new[g]) for g in heads]
        pv = [jnp.dot(p[g].astype(BF16), v_ref[g, pl.ds(start, t), :], preferred_element_type=F32)
              for g in heads]
        out = []
        for g in heads:
            m, l, acc = carry[g]
            alpha = jnp.exp2(m - m_new[g])
            out.append((m_new[g], alpha * l + jnp.sum(p[g], axis=-1, keepdims=True), alpha * acc + pv[g]))
        return tuple(out)

    init = tuple((jnp.full((t, 1), NEG_BIG, F32), jnp.zeros((t, 1), F32), jnp.zeros((t, V_A), F32))
                 for _ in range(MLA_HEADS_PER_STEP))
    carry = lax.fori_loop(0, qi, lambda j, c: block(j, c, False), init)
    carry = block(qi, carry, True)
    for g in range(MLA_HEADS_PER_STEP):
        m, l, acc = carry[g]
        o_ref[:, g * V_A:(g + 1) * V_A] = (acc / l).astype(BF16)


def mla_prompt(q, kx, v, batch, seq):
    t = ATT_BLOCK
    g = MLA_HEADS_PER_STEP
    assert seq % t == 0 and H_A % g == 0
    nq = seq // t
    return pl.pallas_call(
        _mla_prompt_kernel,
        grid=(batch, H_A // g, nq),
        in_specs=[pl.BlockSpec((g, t, HEAD_W), lambda b, h, i: (h, b * nq + i, 0)),
                  pl.BlockSpec((g, seq, HEAD_W), lambda b, h, i: (h, b, 0)),
                  pl.BlockSpec((g, seq, V_A), lambda b, h, i: (h, b, 0))],
        out_specs=pl.BlockSpec((t, g * V_A), lambda b, h, i: (b * nq + i, h)),
        out_shape=jax.ShapeDtypeStruct((batch * seq, H_A * V_A), BF16),
        compiler_params=_cparams("parallel", "parallel", "arbitrary"),
        name="mla_prompt",
    )(q, kx, v)


def _mla_sample_kernel(q_ref, ckvp_ref, kpep_ref, ckvn_ref, kpen_ref, wkv_ref, o_ref):
    seq = ckvn_ref.shape[0]
    ckvp = ckvp_ref[0].astype(BF16)
    ckvn = ckvn_ref[...].astype(BF16)
    kpep = kpep_ref[0].astype(BF16)
    kpen = kpen_ref[...].astype(BF16)
    heads = range(H_A)
    q_lat = jnp.concatenate([_dot_nt(q_ref[h, :, 0:QK_NOPE], wkv_ref[:, h * HEAD_W:h * HEAD_W + QK_NOPE])
                             for h in heads], axis=0)
    q_pe = jnp.concatenate([q_ref[h, :, LANES:LANES + QK_ROPE] for h in heads], axis=0)
    s_p = _dot_nt(q_lat, ckvp) + _dot_nt(q_pe, kpep)
    s_n = _dot_nt(q_lat, ckvn) + _dot_nt(q_pe, kpen)
    m = jnp.maximum(jnp.max(s_p, axis=-1, keepdims=True), jnp.max(s_n, axis=-1, keepdims=True))
    p_p = jnp.exp2(s_p - m)
    p_n = jnp.exp2(s_n - m)
    l = jnp.sum(p_p, axis=-1, keepdims=True) + jnp.sum(p_n, axis=-1, keepdims=True)
    o_lat = (_dot(p_p, ckvp) + _dot(p_n, ckvn)) / l
    for h in heads:
        o_ref[:, h * V_A:(h + 1) * V_A] = _dot(
            o_lat[h * seq:(h + 1) * seq], wkv_ref[:, h * HEAD_W + QK_NOPE:(h + 1) * HEAD_W]).astype(BF16)


def mla_sample(q, ckv_past, kpe_past, ckv_new, kpe_new, wkv, batch, seq):
    past = ckv_past.shape[1]
    return pl.pallas_call(
        _mla_sample_kernel,
        grid=(batch,),
        in_specs=[pl.BlockSpec((H_A, seq, HEAD_W), lambda b: (0, b, 0)),
                  pl.BlockSpec((1, past, KV_LORA), lambda b: (b, 0, 0)),
                  pl.BlockSpec((1, past, QK_ROPE), lambda b: (b, 0, 0)),
                  pl.BlockSpec((seq, KV_LORA), lambda b: (b, 0)),
                  pl.BlockSpec((seq, QK_ROPE), lambda b: (b, 0)),
                  _resident((KV_LORA, H_A * HEAD_W), lambda b: (0, 0))],
        out_specs=pl.BlockSpec((seq, H_A * V_A), lambda b: (b, 0)),
        out_shape=jax.ShapeDtypeStruct((batch * seq, H_A * V_A), BF16),
        compiler_params=_cparams("parallel"),
        name="mla_sample",
    )(q, ckv_past, kpe_past, ckv_new, kpe_new, wkv)


def _rwkv_prep_kernel(r_ref, k_ref, v_ref, wa_ref, g_ref, rp_ref, kp_ref, vp_ref, wap_ref, gp_ref,
                      mu_r_ref, mu_k_ref, mu_v_ref, mu_wa_ref, mu_g_ref,
                      w0_ref, ww2_ref, a0_ref, wa2_ref, wg2_ref, kk_ref, ka_ref,
                      r_out, lw_out, k_out, v_out, kk_out, a_out, g_out, *, tm, seq, first_rows):
    row = lax.broadcasted_iota(jnp.int32, (tm, 1), 0)
    if not first_rows:
        keep = ((pl.program_id(0) * tm) % seq != 0).astype(F32)

    def mix(p_ref, pp_ref, mu_ref):
        p = p_ref[...]
        rolled = pltpu.roll(p, 1, 0)
        if first_rows:
            prev = jnp.where(row % seq == 0, pp_ref[...], rolled)
        else:
            prev = jnp.where(row == 0, pp_ref[7:8, :] * keep, rolled)
        return p + (prev - p) * mu_ref[...]

    r_out[...] = mix(r_ref, rp_ref, mu_r_ref)
    v_out[...] = mix(v_ref, vp_ref, mu_v_ref)
    k = mix(k_ref, kp_ref, mu_k_ref)
    xwa = mix(wa_ref, wap_ref, mu_wa_ref)
    xg = mix(g_ref, gp_ref, mu_g_ref)
    w_log = -_softplus(-(w0_ref[...] + _dot(jnp.tanh(xwa), ww2_ref[...]))) - 0.5
    lw_out[...] = -jnp.exp(w_log)
    a = _sigmoid(a0_ref[...] + _dot(xwa, wa2_ref[...]))
    a_out[...] = a
    g_out[...] = _dot(_sigmoid(xg), wg2_ref[...])
    kk_out[...] = k * kk_ref[...]
    k_out[...] = k * (1.0 + (a - 1.0) * ka_ref[...])


def rwkv_prep(proj, first, mu_parts, w0, ww2p, a0, wa2p, wg2, k_k, k_a, seq):
    m = proj.shape[0]
    tm = ROW_TILE
    wide = lambda c: pl.BlockSpec((tm, C_B), lambda i: (i, c))
    narrow = lambda c: pl.BlockSpec((tm, LANES), lambda i: (i, c))
    vec = lambda n: pl.BlockSpec((1, n), lambda i: (0, 0))
    out = jax.ShapeDtypeStruct((m, C_B), F32)
    if first is None:
        assert seq % tm == 0
        before = lambda i: jnp.maximum(i * (tm // 8) - 1, 0)
        prev_specs = [pl.BlockSpec((8, C_B), lambda i, c=c: (before(i), c)) for c in range(3)]
        prev_specs += [pl.BlockSpec((8, LANES), lambda i, c=c: (before(i), c))
                       for c in (COL_WA // LANES, COL_G // LANES)]
        prev = proj
    else:
        assert tm % seq == 0
        prev_specs = [wide(0), wide(1), wide(2), narrow(3 * C_B // LANES), narrow(3 * C_B // LANES + 1)]
        prev = first
    return pl.pallas_call(
        functools.partial(_rwkv_prep_kernel, tm=tm, seq=seq, first_rows=first is not None),
        grid=(m // tm,),
        in_specs=[wide(0), wide(1), wide(2), narrow(COL_WA // LANES), narrow(COL_G // LANES),
                  *prev_specs,
                  vec(C_B), vec(C_B), vec(C_B), vec(LANES), vec(LANES),
                  vec(C_B), _resident((LANES, C_B), lambda i: (0, 0)),
                  vec(C_B), _resident((LANES, C_B), lambda i: (0, 0)),
                  _resident((G_LORA, C_B), lambda i: (0, 0)), vec(C_B), vec(C_B)],
        out_specs=[pl.BlockSpec((tm, C_B), lambda i: (i, 0))] * 7,
        out_shape=[out] * 7,
        compiler_params=_cparams("parallel"),
        name="rwkv_prep",
    )(proj, proj, proj, proj, proj, prev, prev, prev, prev, prev,
      *mu_parts, w0.reshape(1, -1), ww2p, a0.reshape(1, -1), wa2p, wg2,
      k_k.reshape(1, -1), k_a.reshape(1, -1))


RWKV_PAIRS = 8


def _rwkv_chunk_kernel(r_ref, lw_ref, k_ref, v_ref, kk_ref, a_ref, g_ref, rk_ref, lnw_ref, lnb_ref,
                       m0_ref, o_ref, mout_ref, m_scr, *, chunk, pairs, seqs):
    c = pl.program_id(2)
    nc = pl.num_programs(2)
    L = chunk
    L2 = 2 * L

    @pl.when(c == 0)
    def _():
        m_scr[...] = m0_ref[...]

    lane = lax.broadcasted_iota(jnp.int32, (L, LANES), 1)
    low = lane < HD_B
    row_p = lax.broadcasted_iota(jnp.int32, (L, L2), 0)
    col_p = lax.broadcasted_iota(jnp.int32, (L, L2), 1) % L
    strict = row_p > col_p
    incl = row_p >= col_p
    eye_p = (row_p == col_p).astype(F32)
    same_blk = ((lax.broadcasted_iota(jnp.int32, (L2, L2), 0) < L)
                == (lax.broadcasted_iota(jnp.int32, (L2, L2), 1) < L))
    r128 = lax.broadcasted_iota(jnp.int32, (LANES, LANES), 0)
    c128 = lax.broadcasted_iota(jnp.int32, (LANES, LANES), 1)
    same_head = (r128 < HD_B) == (c128 < HD_B)
    diag = r128 == c128

    def seg_sum(x):
        s0 = jnp.sum(jnp.where(low, x, 0.0), axis=-1, keepdims=True)
        s1 = jnp.sum(jnp.where(low, 0.0, x), axis=-1, keepdims=True)
        return jnp.where(low, s0, s1)

    def split_heads(x):
        return jnp.concatenate([jnp.where(low, x, 0.0), jnp.where(low, 0.0, x)], axis=0)

    def block_diag(x):
        return jnp.where(same_blk, jnp.concatenate([x, x], axis=0), 0.0)

    P = range(seqs * pairs)
    sqs = [p // pairs for p in P]
    sls = [slice((p % pairs) * LANES, (p % pairs + 1) * LANES) for p in P]

    def prep(b, sl):
        r = r_ref[b, :, sl]
        lw = lw_ref[b, :, sl]
        k = k_ref[b, :, sl]
        kkr = kk_ref[b, :, sl]
        kk = kkr / jnp.maximum(jnp.sqrt(seg_sum(kkr * kkr)), 1e-12)
        bvec = kk * a_ref[b, :, sl]
        ci = _cumsum_rows(lw)
        c_end = ci[L - 1:L, :]
        e_neg = jnp.exp(-ci)
        e_end = jnp.exp(c_end - ci)
        return dict(r=r, k=k, v=v_ref[b, :, sl], c_end=c_end,
                    at=-kk * jnp.exp(ci - lw), bt=bvec * e_neg, kt=k * e_neg, rt=r * jnp.exp(ci),
                    bh=bvec * e_end, kh=k * e_end)

    d = [prep(sqs[p], sls[p]) for p in P]
    sc = [_dot_nt(jnp.concatenate([x["at"], x["rt"]], axis=0),
                  jnp.concatenate([split_heads(x["bt"]), split_heads(x["kt"])], axis=0)) for x in d]
    nab = [jnp.where(strict, s[0:L, 0:L2], 0.0) for s in sc]
    nak = [jnp.where(strict, s[0:L, L2:2 * L2], 0.0) for s in sc]
    lrb = [jnp.where(incl, s[L:L2, 0:L2], 0.0) for s in sc]
    lrk = [jnp.where(incl, s[L:L2, L2:2 * L2], 0.0) for s in sc]
    v_split = [split_heads(x["v"]) for x in d]

    t_inv = [eye_p + n for n in nab]
    pw = [_dot(n, block_diag(n)) for n in nab]
    nakv = [_dot(nak[p], v_split[p]) for p in P]
    m = 2
    while m < L:
        if 2 * m < L:
            res = [_dot(pw[p], jnp.concatenate([block_diag(pw[p]), block_diag(t_inv[p])], axis=1)) for p in P]
            pw = [x[:, 0:L2] for x in res]
            t_inv = [t_inv[p] + res[p][:, L2:2 * L2] for p in P]
        else:
            t_inv = [t_inv[p] + _dot(pw[p], block_diag(t_inv[p])) for p in P]
        m *= 2
    yk = [_dot(lrk[p], v_split[p]) for p in P]
    hk = [_dot_tn(d[p]["kh"], d[p]["v"]) for p in P]

    wu = [_dot(t_inv[p], jnp.concatenate([split_heads(d[p]["at"]), split_heads(nakv[p])], axis=1)) for p in P]
    qy = [_dot(lrb[p], jnp.concatenate([split_heads(wu[p][:, 0:LANES]), split_heads(wu[p][:, LANES:])], axis=1))
          for p in P]
    gh = [_dot_tn(d[p]["bh"], wu[p]) for p in P]
    q = [d[p]["rt"] + qy[p][:, 0:LANES] for p in P]
    gmat = [jnp.where(diag, jnp.broadcast_to(jnp.exp(d[p]["c_end"]), (LANES, LANES)), 0.0)
            + jnp.where(same_head, gh[p][:, 0:LANES], 0.0) for p in P]
    ym = [_dot(jnp.concatenate([q[p], gmat[p]], axis=0), m_scr[sqs[p], p % pairs]) for p in P]
    for p in P:
        m_scr[sqs[p], p % pairs] = (ym[p][L:L + LANES]
                                    + jnp.where(same_head, gh[p][:, LANES:] + hk[p], 0.0))
    for p in P:
        sl = sls[p]
        y = ym[p][0:L] + qy[p][:, LANES:] + yk[p]
        mean = seg_sum(y) * (1.0 / HD_B)
        dev = y - mean
        var = seg_sum(dev * dev) * (1.0 / HD_B)
        yn = dev * lax.rsqrt(var + LNX_EPS) * lnw_ref[:, sl] + lnb_ref[:, sl]
        bonus = seg_sum(d[p]["r"] * d[p]["k"] * rk_ref[:, sl]) * d[p]["v"]
        o_ref[sqs[p], :, sl] = ((yn + bonus) * g_ref[sqs[p], :, sl]).astype(BF16)

    @pl.when(c == nc - 1)
    def _():
        mout_ref[...] = m_scr[...]


def rwkv_chunk(acts, r_k, lnx_w, lnx_b, m0, batch, seq, chunk, pairs, seqs):
    nc = seq // chunk
    npair = C_B // LANES
    w = pairs * LANES
    act = pl.BlockSpec((seqs, chunk, w), lambda b, pg, c: (b, c, pg))
    prm = pl.BlockSpec((1, w), lambda b, pg, c: (0, pg))
    st = pl.BlockSpec((seqs, pairs, LANES, LANES), lambda b, pg, c: (b, pg, 0, 0))
    o, m = pl.pallas_call(
        functools.partial(_rwkv_chunk_kernel, chunk=chunk, pairs=pairs, seqs=seqs),
        grid=(batch // seqs, npair // pairs, nc),
        in_specs=[act] * 7 + [prm] * 3 + [st],
        out_specs=[act, st],
        out_shape=[jax.ShapeDtypeStruct((batch, seq, C_B), BF16),
                   jax.ShapeDtypeStruct((batch, npair, LANES, LANES), F32)],
        scratch_shapes=[pltpu.VMEM((seqs, pairs, LANES, LANES), F32)],
        compiler_params=_cparams("parallel", "parallel", "arbitrary"),
        name="rwkv_chunk",
    )(*(t.reshape(batch, seq, C_B) for t in acts),
      r_k.reshape(1, -1), lnx_w.reshape(1, -1), lnx_b.reshape(1, -1), m0)
    return o.reshape(batch * seq, C_B), m


BAND_Q = 256
BAND_SEGS = BAND // BAND_Q + 1


def _band_prompt_kernel(q_ref, k_ref, v_ref, bias_ref, o_ref):
    qb = pl.program_id(2)
    t = BAND_Q
    q = q_ref[...] * BAND_SCALE
    lane = lax.broadcasted_iota(jnp.int32, (t, LANES), 1)
    low = lane < HD_C
    ks, vs, valid = [], [], []
    for seg in range(BAND_SEGS):
        kb = qb - (BAND_SEGS - 1) + seg
        start = pl.multiple_of(jnp.maximum(kb, 0) * t, t)
        ks.append(k_ref[pl.ds(start, t), :].astype(BF16))
        vs.append(v_ref[pl.ds(start, t), :].astype(BF16))
        valid.append(kb >= 0)
    heads = range(2)
    segs = range(BAND_SEGS)
    qh = [jnp.where(low if hh == 0 else jnp.logical_not(low), q, 0.0).astype(BF16) for hh in heads]
    s = [[jnp.where(valid[seg],
                    lax.dot_general(qh[hh], ks[seg], (((1,), (1,)), ((), ())), preferred_element_type=F32)
                    + bias_ref[hh, :, seg * t:(seg + 1) * t], NEG_BIG)
          for seg in segs] for hh in heads]
    m = [functools.reduce(jnp.maximum, [jnp.max(s[hh][seg], axis=-1, keepdims=True) for seg in segs])
         for hh in heads]
    p = [[jnp.exp2(s[hh][seg] - m[hh]) for seg in segs] for hh in heads]
    l = [sum(jnp.sum(p[hh][seg], axis=-1, keepdims=True) for seg in segs) for hh in heads]
    acc = [sum(jnp.dot(p[hh][seg].astype(BF16), vs[seg], preferred_element_type=F32) for seg in segs)
           for hh in heads]
    o_ref[...] = jnp.where(low, acc[0] / l[0], acc[1] / l[1]).astype(BF16)


def band_prompt(proj, bias, batch, seq):
    t = BAND_Q
    nq = seq // t
    npair = C_C // LANES
    return pl.pallas_call(
        _band_prompt_kernel,
        grid=(npair, batch, nq),
        in_specs=[pl.BlockSpec((t, LANES), lambda hp, b, i: (b * nq + i, hp)),
                  pl.BlockSpec((seq, LANES), lambda hp, b, i: (b, npair + hp)),
                  pl.BlockSpec((seq, LANES), lambda hp, b, i: (b, 2 * npair + hp)),
                  pl.BlockSpec((2, t, BAND_SEGS * t), lambda hp, b, i: (hp, 0, 0))],
        out_specs=pl.BlockSpec((t, LANES), lambda hp, b, i: (b * nq + i, hp)),
        out_shape=jax.ShapeDtypeStruct((batch * seq, C_C), BF16),
        compiler_params=_cparams("parallel", "parallel", "parallel"),
        name="band_prompt",
    )(proj, proj, proj, bias)


def _band_sample_kernel(q_ref, kn_ref, vn_ref, kp_ref, vp_ref, biasp_ref, biasn_ref, o_ref):
    t = q_ref.shape[0]
    lane = lax.broadcasted_iota(jnp.int32, (t, LANES), 1)
    low = lane < HD_C
    pairs = range(C_C // LANES)
    heads = range(H_C)
    sls = [slice(hp * LANES, (hp + 1) * LANES) for hp in pairs]
    kp = [kp_ref[0, :, sl].astype(BF16) for sl in sls]
    vp = [vp_ref[0, :, sl].astype(BF16) for sl in sls]
    kn = [kn_ref[:, sl].astype(BF16) for sl in sls]
    vn = [vn_ref[:, sl].astype(BF16) for sl in sls]
    q = [q_ref[:, sl] * BAND_SCALE for sl in sls]
    qh = [jnp.where(low if h % 2 == 0 else jnp.logical_not(low), q[h // 2], 0.0).astype(BF16) for h in heads]
    s_p = [_dot_nt(qh[h], kp[h // 2]) + biasp_ref[h] for h in heads]
    s_n = [_dot_nt(qh[h], kn[h // 2]) + biasn_ref[h] for h in heads]
    m = [jnp.maximum(jnp.max(s_p[h], axis=-1, keepdims=True), jnp.max(s_n[h], axis=-1, keepdims=True))
         for h in heads]
    p_p = [jnp.exp2(s_p[h] - m[h]) for h in heads]
    p_n = [jnp.exp2(s_n[h] - m[h]) for h in heads]
    l = [jnp.sum(p_p[h], axis=-1, keepdims=True) + jnp.sum(p_n[h], axis=-1, keepdims=True) for h in heads]
    o = [(_dot(p_p[h], vp[h // 2]) + _dot(p_n[h], vn[h // 2])) / l[h] for h in heads]
    for hp in pairs:
        o_ref[:, sls[hp]] = jnp.where(low, o[2 * hp], o[2 * hp + 1]).astype(BF16)


def band_sample(proj, k_past, v_past, bias_p, bias_n, batch, seq):
    past = k_past.shape[1]
    new = lambda col: pl.BlockSpec((seq, C_C), lambda b: (b, col))
    old = pl.BlockSpec((1, past, C_C), lambda b: (b, 0, 0))
    return pl.pallas_call(
        _band_sample_kernel,
        grid=(batch,),
        in_specs=[new(0), new(1), new(2), old, old,
                  _resident((H_C, seq, past), lambda b: (0, 0, 0)),
                  _resident((H_C, seq, seq), lambda b: (0, 0, 0))],
        out_specs=pl.BlockSpec((seq, C_C), lambda b: (b, 0)),
        out_shape=jax.ShapeDtypeStruct((batch * seq, C_C), BF16),
        compiler_params=_cparams("parallel"),
        name="band_sample",
    )(proj, proj, proj, k_past, v_past, bias_p, bias_n)


def _gla_kernel(q_ref, k_ref, v_ref, go_ref, gk_ref, wgk_ref, bgk_ref, gn_ref, s0_ref,
                o_ref, sout_ref, s_scr, *, chunk):
    c = pl.program_id(1)
    nc = pl.num_programs(1)
    L = chunk

    @pl.when(c == 0)
    def _():
        s_scr[...] = s0_ref[0]

    causal = (lax.broadcasted_iota(jnp.int32, (L, L), 0) >= lax.broadcasted_iota(jnp.int32, (L, L), 1))
    diag = (lax.broadcasted_iota(jnp.int32, (DK_D, DK_D), 0)
            == lax.broadcasted_iota(jnp.int32, (DK_D, DK_D), 1))
    H = range(H_D)
    ksl = [slice(h * DK_D, (h + 1) * DK_D) for h in H]
    vsl = [slice(h * DV_D, (h + 1) * DV_D) for h in H]
    x = _dot(gk_ref[...], wgk_ref[...]) + bgk_ref[...]
    b = _cumsum_rows(-_softplus(-x) / GATE_NORM)
    eb = jnp.exp(b)
    enb = jnp.exp(-b)
    b_end = b[L - 1:L, :]
    ed = jnp.exp(b_end - b)
    e_end = jnp.exp(b_end)
    qe = [q_ref[:, ksl[h]] * DK_D ** -0.5 * eb[:, ksl[h]] for h in H]
    vh = [v_ref[:, vsl[h]].astype(BF16) for h in H]
    att = [jnp.where(causal, _dot_nt(qe[h], k_ref[:, ksl[h]] * enb[:, ksl[h]]), 0.0) for h in H]
    s_prev = [s_scr[h] for h in H]
    o = [_dot(att[h], vh[h]) + _dot(qe[h], s_prev[h]) for h in H]
    kv = [_dot_tn(k_ref[:, ksl[h]] * ed[:, ksl[h]], vh[h]) for h in H]
    for h in H:
        e_col = jnp.sum(jnp.where(diag, jnp.broadcast_to(e_end[:, ksl[h]], (DK_D, DK_D)), 0.0),
                        axis=1, keepdims=True)
        s_scr[h] = s_prev[h] * e_col + kv[h]
    for h in H:
        on = o[h] * lax.rsqrt(jnp.mean(o[h] * o[h], axis=-1, keepdims=True) + EPS) * gn_ref[...]
        gate = go_ref[:, vsl[h]]
        o_ref[:, vsl[h]] = (on * (gate * _sigmoid(gate))).astype(BF16)

    @pl.when(c == nc - 1)
    def _():
        sout_ref[0] = s_scr[...]


COL_QD = 3 * C_C
COL_KD = COL_QD + H_D * DK_D
COL_VD = COL_KD + H_D * DK_D
COL_GO = COL_VD + H_D * DV_D
COL_GK = COL_GO + H_D * DV_D
COLS_ODD_EXT = COL_GK + LANES


def gla(proj, wgk, bgk, gnorm, s0, row0, batch, seq, chunk):
    nc = seq // chunk
    kw = H_D * DK_D
    vw = H_D * DV_D
    blk0 = row0 // chunk
    row = lambda col: (lambda b, c: (blk0 + b * nc + c, col))
    return pl.pallas_call(
        functools.partial(_gla_kernel, chunk=chunk),
        grid=(batch, nc),
        in_specs=[pl.BlockSpec((chunk, kw), row(COL_QD // kw)),
                  pl.BlockSpec((chunk, kw), row(COL_KD // kw)),
                  pl.BlockSpec((chunk, vw), row(COL_VD // vw)),
                  pl.BlockSpec((chunk, vw), row(COL_GO // vw)),
                  pl.BlockSpec((chunk, LANES), row(COL_GK // LANES)),
                  _resident((LANES, kw), lambda b, c: (0, 0)),
                  pl.BlockSpec((1, kw), lambda b, c: (0, 0)),
                  pl.BlockSpec((1, DV_D), lambda b, c: (0, 0)),
                  pl.BlockSpec((1, H_D, DK_D, DV_D), lambda b, c: (b, 0, 0, 0))],
        out_specs=[pl.BlockSpec((chunk, vw), lambda b, c: (b * nc + c, 0)),
                   pl.BlockSpec((1, H_D, DK_D, DV_D), lambda b, c: (b, 0, 0, 0))],
        out_shape=[jax.ShapeDtypeStruct((batch * seq, vw), BF16),
                   jax.ShapeDtypeStruct((batch, H_D, DK_D, DV_D), F32)],
        scratch_shapes=[pltpu.VMEM((H_D, DK_D, DV_D), F32)],
        compiler_params=_cparams("parallel", "arbitrary"),
        name="gla",
    )(proj, proj, proj, proj, proj, wgk, bgk.reshape(1, -1), gnorm.reshape(1, -1), s0)


FFN_TM = 512
FFN_TF = 512
HALO = 16


def _conv3(u_scr, cw_ref, cb_ref, cols, tm, prev1=None, prev2=None):
    u0 = u_scr[HALO:HALO + tm, cols]
    u1 = u_scr[HALO - 1:HALO - 1 + tm, cols]
    u2 = u_scr[HALO - 2:HALO - 2 + tm, cols]
    if prev1 is not None:
        u1 = prev1(u1)
        u2 = prev2(u2)
    return cb_ref[:, cols] + cw_ref[2:3, cols] * u0 + cw_ref[1:2, cols] * u1 + cw_ref[0:1, cols] * u2


def _ffn_prompt_kernel(x_ref, a1_ref, a2_ref, xh_ref, a1h_ref, a2h_ref, w1_ref, w2_ref, g_ref, gf_ref,
                       wg_ref, wv_ref, cwg_ref, cwv_ref, cbg_ref, cbv_ref, wd_ref,
                       o_ref, tg_ref, tv_ref, h_scr, *u_scrs, tm, blocks_per_seq, final_norm_out):
    i = pl.program_id(0)
    j = pl.program_id(1)
    nj = pl.num_programs(1)

    @pl.when(j == 0)
    def _():
        w1 = w1_ref[...]
        w2 = w2_ref[...]
        xm = x_ref[...] + _dot(a1_ref[...], w1) + _dot(a2_ref[...], w2)
        o_ref[...] = xm
        h_scr[HALO:HALO + tm, :] = _rms(xm, g_ref[...]).astype(BF16)
        keep = (i % blocks_per_seq != 0).astype(F32)
        xmh = xh_ref[...] + _dot(a1h_ref[...], w1) + _dot(a2h_ref[...], w2)
        h_scr[0:HALO, :] = (_rms(xmh, g_ref[...]) * keep).astype(BF16)

    h = h_scr[...]
    ug_scr, uv_scr = u_scrs
    ug_scr[...] = jnp.dot(h, wg_ref[...], preferred_element_type=F32)
    uv_scr[...] = jnp.dot(h, wv_ref[...], preferred_element_type=F32)
    acts = []
    for c in range(0, ug_scr.shape[1], LANES):
        lanes = slice(c, c + LANES)
        cg = _conv3(ug_scr, cwg_ref, cbg_ref, lanes, tm)
        cv = _conv3(uv_scr, cwv_ref, cbv_ref, lanes, tm)
        acts.append((cg * _sigmoid(cg) * cv).astype(BF16))
    o_ref[...] += jnp.dot(jnp.concatenate(acts, axis=1), wd_ref[...], preferred_element_type=F32)
    tg_ref[0] = ug_scr[HALO + tm - 8:HALO + tm, :]
    tv_ref[0] = uv_scr[HALO + tm - 8:HALO + tm, :]

    if final_norm_out:
        @pl.when(j == nj - 1)
        def _():
            o_ref[...] = _rms(o_ref[...], gf_ref[...])


def ffn_prompt(x, a1, a2, w_out, e, g, w_up, w_down, layer, conv_w, conv_b, seq, final_g=None):
    m, k = x.shape
    ka = a1.shape[1]
    tm, tf = FFN_TM, FFN_TF
    nj = D_FF // tf
    nblk = m // tm
    bps = seq // tm
    halo_blocks = tm // HALO
    cb = conv_b.reshape(1, -1)
    gf = (g if final_g is None else final_g).reshape(1, k)
    before = lambda i, j: (jnp.maximum(i * halo_blocks - 1, 0), 0)
    return pl.pallas_call(
        functools.partial(_ffn_prompt_kernel, tm=tm, blocks_per_seq=bps, final_norm_out=final_g is not None),
        grid=(nblk, nj),
        in_specs=[pl.BlockSpec((tm, k), lambda i, j: (i, 0)),
                  pl.BlockSpec((tm, ka), lambda i, j: (i, 0)),
                  pl.BlockSpec((tm, ka), lambda i, j: (i, 0)),
                  pl.BlockSpec((HALO, k), before),
                  pl.BlockSpec((HALO, ka), before),
                  pl.BlockSpec((HALO, ka), before),
                  _resident((None, ka, k), lambda i, j: (e, 0, 0)),
                  _resident((None, ka, k), lambda i, j: (e, 1, 0)),
                  pl.BlockSpec((1, k), lambda i, j: (0, 0)),
                  pl.BlockSpec((1, k), lambda i, j: (0, 0)),
                  pl.BlockSpec((None, k, tf), lambda i, j: (layer, 0, j)),
                  pl.BlockSpec((None, k, tf), lambda i, j: (layer, 0, nj + j)),
                  pl.BlockSpec((CONV_W, tf), lambda i, j: (0, j)),
                  pl.BlockSpec((CONV_W, tf), lambda i, j: (0, nj + j)),
                  pl.BlockSpec((1, tf), lambda i, j: (0, j)),
                  pl.BlockSpec((1, tf), lambda i, j: (0, nj + j)),
                  pl.BlockSpec((None, tf, k), lambda i, j: (layer, j, 0))],
        out_specs=[pl.BlockSpec((tm, k), lambda i, j: (i, 0)),
                   pl.BlockSpec((1, 8, tf), lambda i, j: (i, 0, j)),
                   pl.BlockSpec((1, 8, tf), lambda i, j: (i, 0, j))],
        out_shape=[jax.ShapeDtypeStruct((m, k), F32),
                   jax.ShapeDtypeStruct((nblk, 8, D_FF), F32),
                   jax.ShapeDtypeStruct((nblk, 8, D_FF), F32)],
        scratch_shapes=[pltpu.VMEM((tm + HALO, k), BF16)] + [pltpu.VMEM((tm + HALO, tf), F32)] * 2,
        compiler_params=_cparams("parallel", "arbitrary"),
        name="ffn_prompt",
    )(x, a1, a2, x, a1, a2, w_out, w_out, g.reshape(1, k), gf, w_up, w_up, conv_w, conv_w, cb, cb, w_down)


def _ffn_sample_kernel(x_ref, g_ref, gf_ref, wg_ref, wv_ref, cwg_ref, cwv_ref, cbg_ref, cbv_ref, wd_ref,
                       sg_ref, sv_ref,
                       o_ref, ug_ref, uv_ref, h_scr, acc_scr, ug_scr, uv_scr, *, tm, seq, final_norm_out):
    j = pl.program_id(0)
    nj = pl.num_programs(0)
    nseq = tm // seq
    tf = ug_scr.shape[1]
    pos = lax.broadcasted_iota(jnp.int32, (nseq, seq, 1), 1)
    cols = slice(0, tf)

    def before(s_ref, back):
        def fix(u):
            u = u.reshape(nseq, seq, tf)
            for p in range(back):
                u = jnp.where(pos == p, s_ref[:, CONV_W - 1 - back + p:CONV_W - back + p, :], u)
            return u.reshape(tm, tf)
        return fix

    @pl.when(j == 0)
    def _():
        h_scr[...] = _rms(x_ref[...], g_ref[...]).astype(BF16)
        acc_scr[...] = jnp.zeros_like(acc_scr)
        ug_scr[0:HALO, :] = jnp.zeros((HALO, ug_scr.shape[1]), F32)
        uv_scr[0:HALO, :] = jnp.zeros((HALO, uv_scr.shape[1]), F32)

    h = h_scr[...]
    ug = jnp.dot(h, wg_ref[...], preferred_element_type=F32)
    uv = jnp.dot(h, wv_ref[...], preferred_element_type=F32)
    ug_scr[HALO:HALO + tm, :] = ug
    uv_scr[HALO:HALO + tm, :] = uv
    ug_ref[...] = ug
    uv_ref[...] = uv
    cg = _conv3(ug_scr, cwg_ref, cbg_ref, cols, tm, before(sg_ref, 1), before(sg_ref, 2))
    cv = _conv3(uv_scr, cwv_ref, cbv_ref, cols, tm, before(sv_ref, 1), before(sv_ref, 2))
    act = cg * _sigmoid(cg) * cv
    acc_scr[...] += _dot(act, wd_ref[...])

    @pl.when(j == nj - 1)
    def _():
        out = x_ref[...] + acc_scr[...]
        o_ref[...] = _rms(out, gf_ref[...]) if final_norm_out else out


def ffn_sample(x, g, w_up, w_down, layer, conv_w, conv_b, state, seq, final_g=None):
    m, k = x.shape
    tm, tf = m, FFN_TF
    nj = D_FF // tf
    nseq = m // seq
    cb = conv_b.reshape(1, -1)
    gf = (g if final_g is None else final_g).reshape(1, k)
    gate = lambda shape: pl.BlockSpec(shape, lambda j: (0, j))
    val = lambda shape: pl.BlockSpec(shape, lambda j: (0, nj + j))
    carried = lambda off: pl.BlockSpec((nseq, CONV_W - 1, tf), lambda j: (0, 0, off + j))
    return pl.pallas_call(
        functools.partial(_ffn_sample_kernel, tm=tm, seq=seq, final_norm_out=final_g is not None),
        grid=(nj,),
        in_specs=[pl.BlockSpec((tm, k), lambda j: (0, 0)),
                  pl.BlockSpec((1, k), lambda j: (0, 0)),
                  pl.BlockSpec((1, k), lambda j: (0, 0)),
                  pl.BlockSpec((None, k, tf), lambda j: (layer, 0, j)),
                  pl.BlockSpec((None, k, tf), lambda j: (layer, 0, nj + j)),
                  gate((CONV_W, tf)), val((CONV_W, tf)),
                  gate((1, tf)), val((1, tf)),
                  pl.BlockSpec((None, tf, k), lambda j: (layer, j, 0)),
                  carried(0), carried(nj)],
        out_specs=[pl.BlockSpec((tm, k), lambda j: (0, 0)), gate((tm, tf)), gate((tm, tf))],
        out_shape=[jax.ShapeDtypeStruct((m, k), F32),
                   jax.ShapeDtypeStruct((m, D_FF), F32),
                   jax.ShapeDtypeStruct((m, D_FF), F32)],
        scratch_shapes=[pltpu.VMEM((tm, k), BF16), pltpu.VMEM((tm, k), F32),
                        pltpu.VMEM((tm + HALO, tf), F32), pltpu.VMEM((tm + HALO, tf), F32)],
        compiler_params=_cparams("arbitrary"),
        name="ffn_sample",
    )(x, g.reshape(1, k), gf, w_up, w_up, conv_w, conv_w, cb, cb, w_down, state, state)


def _rot_cols(w):
    half = QK_ROPE // 2
    return jnp.concatenate([-w[..., half:], w[..., :half]], axis=-1)


def _rope_tables(pos):
    half = QK_ROPE // 2
    inv = ROPE_THETA ** (-jnp.arange(half, dtype=F32) / half)
    ang = pos.astype(F32)[:, None] * inv[None, :]
    cos = jnp.cos(ang)
    sin = jnp.sin(ang)
    zeros = jnp.zeros((pos.shape[0], LANES - QK_ROPE), F32)
    return (jnp.concatenate([cos, cos, zeros], axis=1), jnp.concatenate([sin, sin, zeros], axis=1))


def _pair_state_in(s):
    b = s.shape[0]
    m = jnp.swapaxes(s, -1, -2).reshape(b, H_B // 2, 2, HD_B, HD_B)
    z = jnp.zeros_like(m[:, :, 0])
    top = jnp.concatenate([m[:, :, 0], z], axis=-1)
    bot = jnp.concatenate([z, m[:, :, 1]], axis=-1)
    return jnp.concatenate([top, bot], axis=-2)


def _pair_state_out(m):
    b = m.shape[0]
    h0 = m[:, :, :HD_B, :HD_B]
    h1 = m[:, :, HD_B:, HD_B:]
    s = jnp.stack([h0, h1], axis=2).reshape(b, H_B, HD_B, HD_B)
    return jnp.swapaxes(s, -1, -2)


def _band_bias_prompt(table):
    t = BAND_Q
    span = BAND_SEGS * t
    nd = t + span - 1
    d = (t - 1) - jnp.arange(nd + 1)
    w = table[:, jnp.clip(d + BAND, -REL_CLIP, REL_CLIP) + REL_CLIP]
    rows = jnp.tile(w, (1, t))[:, :t * nd].reshape(table.shape[0], t, nd)
    bias = rows[:, :, t - 1:t - 1 + span]
    q = jnp.arange(t)[:, None]
    j = jnp.arange(span)[None, :]
    lo = (q // CHUNK) * CHUNK
    allowed = (j >= lo) & (j < lo + BAND + CHUNK)
    return jnp.where(allowed[None], bias * LOG2E, NEG_BIG).astype(F32)


def _band_bias_sample(table, seq, past):
    q_pos = PAST_LEN + jnp.arange(seq)
    k_pos = PAST_LEN - past + jnp.arange(past + seq)
    idx = jnp.clip(q_pos[:, None] - k_pos[None, :], -REL_CLIP, REL_CLIP) + REL_CLIP
    bias = table[:, idx].astype(F32) * LOG2E
    return bias[:, :, :past], bias[:, :, past:]


def _last_rows(proj, nseq, seq):
    last = proj.reshape(nseq, seq, -1)[:, -1]
    return jnp.concatenate([last[:, :3 * C_B], last[:, COL_WA:]], axis=1)


def _even_layer(xp, xs, e, rope_p, rope_s, dims, caches, prm):
    bp, tp, bs, ts = dims
    cache_ckv, cache_kpe, state_rwkv, state_shift = caches
    w_in = prm["ev_w_in"][e]
    wa_cols = w_in[:, :Q_LORA + KV_LORA]
    w_kpe = w_in[:, Q_LORA + KV_LORA:Q_LORA + KV_LORA + QK_ROPE]
    w_b = w_in[:, Q_LORA + KV_LORA + QK_ROPE:]
    w_ext = jnp.concatenate([w_b[:, :3 * C_B], wa_cols, w_kpe, _rot_cols(w_kpe), w_b[:, 3 * C_B:]],
                            axis=1).astype(BF16)
    g_mix = prm["norm_mix"][2 * e]
    proj_p = norm_matmul(xp, g_mix, w_ext)
    proj_s = norm_matmul(xs, g_mix, w_ext)

    wq = prm["ev_w_uq"][e].reshape(Q_LORA, H_A, QK_NOPE + QK_ROPE)
    wq_pe = wq[:, :, QK_NOPE:]
    wq_ext = jnp.concatenate([wq[:, :, :QK_NOPE], wq_pe, _rot_cols(wq_pe)], axis=-1)
    wq_ext = wq_ext.reshape(Q_LORA, H_A * HEAD_W).astype(BF16)
    wkv = prm["ev_w_ukv"][e].astype(BF16)
    qn, kvn = prm["ev_q_norm"][e], prm["ev_kv_norm"][e]
    q_p, kx_p, v_p, ckv_p, kpe_p = mla_prep(proj_p, *rope_p, qn, kvn, wq_ext, wkv)
    q_s, _, _, ckv_s, kpe_s = mla_prep(proj_s, *rope_s, qn, kvn, wq_ext, wkv)
    o_a_p = mla_prompt(q_p, kx_p, v_p, bp, tp)
    o_a_s = mla_sample(q_s, cache_ckv[e], cache_kpe[e], ckv_s, kpe_s, wkv, bs, ts)

    mu = prm["ev_mu"][e]
    mu_parts = [mu[None, 0:C_B], mu[None, C_B:2 * C_B], mu[None, 2 * C_B:3 * C_B],
                mu[None, 3 * C_B:3 * C_B + LANES], mu[None, 3 * C_B + LANES:]]
    zw = jnp.zeros((W_LORA, C_B), F32)
    ww2p = jnp.concatenate([prm["ev_w_w2"][e], zw], axis=0).astype(BF16)
    wa2p = jnp.concatenate([zw, prm["ev_w_a2"][e]], axis=0).astype(BF16)
    prep_w = (mu_parts, prm["ev_w0"][e], ww2p, prm["ev_a0"][e], wa2p, prm["ev_w_g2"][e].astype(BF16),
              prm["ev_k_k"][e], prm["ev_k_a"][e])
    acts_p = rwkv_prep(proj_p, None, *prep_w, tp)
    acts_s = rwkv_prep(proj_s, jnp.repeat(state_shift[e], ts, axis=0), *prep_w, ts)
    r_k = prm["ev_r_k"][e].reshape(-1)
    lnw, lnb = prm["ev_lnx_w"][e], prm["ev_lnx_b"][e]
    m0_p = jnp.zeros((bp, H_B // 2, LANES, LANES), F32)
    o_b_p, m_p = rwkv_chunk(acts_p, r_k, lnw, lnb, m0_p, bp, tp, min(CHUNK, tp), RWKV_PAIRS, bp)
    o_b_s, m_s = rwkv_chunk(acts_s, r_k, lnw, lnb, _pair_state_in(state_rwkv[e]), bs, ts,
                            min(CHUNK, ts), RWKV_PAIRS, 2)

    xs = out_proj(xs, o_a_s, o_b_s, prm["ev_w_out16"], e)
    outs = dict(
        ckv_p=ckv_p.reshape(bp, tp, KV_LORA), ckv_s=ckv_s.reshape(bs, ts, KV_LORA),
        kpe_p=kpe_p.reshape(bp, tp, QK_ROPE), kpe_s=kpe_s.reshape(bs, ts, QK_ROPE),
        rw_p=_pair_state_out(m_p), rw_s=_pair_state_out(m_s),
        sh_p=_last_rows(proj_p, bp, tp), sh_s=_last_rows(proj_s, bs, ts))
    return (o_a_p, o_b_p, prm["ev_w_out16"], e), xs, outs


def _odd_layer(xp, xs, o, dims, caches, prm):
    bp, tp, bs, ts = dims
    cache_k, cache_v, state_gla = caches
    band_past = cache_k.shape[2]
    w_in = prm["od_w_in"][o]
    c_gk = 3 * C_C + 2 * H_D * DK_D + H_D * DV_D
    w_ext = jnp.concatenate([w_in[:, :c_gk], w_in[:, c_gk + GK_LORA:], w_in[:, c_gk:c_gk + GK_LORA],
                             jnp.zeros((D_MODEL, LANES - GK_LORA), F32)], axis=1).astype(BF16)
    g_mix = prm["norm_mix"][2 * o + 1]
    proj_p = norm_matmul(xp, g_mix, w_ext)
    proj_s = norm_matmul(xs, g_mix, w_ext)

    table = prm["od_rel_bias"][o]
    o_c_p = band_prompt(proj_p, _band_bias_prompt(table), bp, tp)
    bias_p, bias_n = _band_bias_sample(table, ts, band_past)
    k_past = cache_k[o].reshape(bs, band_past, C_C)
    v_past = cache_v[o].reshape(bs, band_past, C_C)
    o_c_s = band_sample(proj_s, k_past, v_past, bias_p, bias_n, bs, ts)

    wgk = jnp.concatenate([prm["od_w_gk2"][o], jnp.zeros((LANES - GK_LORA, H_D * DK_D), F32)],
                          axis=0).astype(BF16)
    bgk, gn = prm["od_b_gk"][o], prm["od_gnorm"][o]
    s0_p = jnp.zeros((bp, H_D, DK_D, DV_D), F32)
    o_d_p, s_p = gla(proj_p, wgk, bgk, gn, s0_p, 0, bp, tp, min(CHUNK, tp))
    o_d_s, s_s = gla(proj_s, wgk, bgk, gn, state_gla[o], 0, bs, ts, min(CHUNK, ts))

    xs = out_proj(xs, o_c_s, o_d_s, prm["od_w_out16"], o)

    def tail_prompt(col):
        rows = proj_p.reshape(bp, tp, -1)[:, max(tp - band_past, 0):, col * C_C:(col + 1) * C_C]
        rows = rows.reshape(bp, -1, H_C, HD_C)
        return jnp.pad(rows, ((0, 0), (max(band_past - tp, 0), 0), (0, 0), (0, 0)))

    def tail_sample(cache, col):
        new = proj_s[:, col * C_C:(col + 1) * C_C].reshape(bs, ts, H_C, HD_C)
        return jnp.concatenate([cache[:, ts:], new], axis=1)

    outs = dict(bk_p=tail_prompt(1), bv_p=tail_prompt(2),
                bk_s=tail_sample(cache_k[o], 1), bv_s=tail_sample(cache_v[o], 2),
                gla_p=s_p, gla_s=s_s)
    return (o_c_p, o_d_p, prm["od_w_out16"], o), xs, outs


def _ffn_layer(xp, mix_p, xs, layer, dims, state_conv, prm, final_g):
    bp, tp, bs, ts = dims
    g = prm["norm_ffn"][layer]
    w_up, w_down = prm["ffn_w_up16"], prm["ffn_w_down16"]
    cw, cb = prm["ffn_conv_w"][layer], prm["ffn_conv_b"][layer]
    a1, a2, w_out, e = mix_p
    xp, tg, tv = ffn_prompt(xp, a1, a2, w_out, e, g, w_up, w_down, layer, cw, cb, tp, final_g)
    bps = tp // FFN_TM
    tails = jnp.concatenate([tg.reshape(bp, bps, 8, D_FF)[:, -1, 8 - (CONV_W - 1):],
                             tv.reshape(bp, bps, 8, D_FF)[:, -1, 8 - (CONV_W - 1):]], axis=-1)

    st = state_conv[layer]
    xs, ug, uv = ffn_sample(xs, g, w_up, w_down, layer, cw, cb, st, ts, final_g)
    keep = CONV_W - 1
    new = jnp.concatenate([ug.reshape(bs, ts, D_FF)[:, max(ts - keep, 0):],
                           uv.reshape(bs, ts, D_FF)[:, max(ts - keep, 0):]], axis=-1)
    ext = jnp.concatenate([st[:, ts:], new], axis=1)
    return xp, xs, tails, ext


@jax.jit
def kernel(x_prompt, x_sample, cache_mla_ckv, cache_mla_kpe, state_rwkv, state_rwkv_shift, cache_band_k, cache_band_v, state_gla, state_ffn_conv, norm_mix, norm_ffn, norm_final, ev_w_in, ev_q_norm, ev_w_uq, ev_kv_norm, ev_w_ukv, ev_mu, ev_w0, ev_w_w2, ev_a0, ev_w_a2, ev_w_g2, ev_k_k, ev_k_a, ev_r_k, ev_lnx_w, ev_lnx_b, ev_w_out, od_w_in, od_rel_bias, od_w_gk2, od_b_gk, od_gnorm, od_w_out, ffn_w_up, ffn_conv_w, ffn_conv_b, ffn_w_down):
    prm = dict(norm_mix=norm_mix, norm_ffn=norm_ffn, ev_w_in=ev_w_in, ev_q_norm=ev_q_norm, ev_w_uq=ev_w_uq,
               ev_kv_norm=ev_kv_norm, ev_w_ukv=ev_w_ukv, ev_mu=ev_mu, ev_w0=ev_w0, ev_w_w2=ev_w_w2,
               ev_a0=ev_a0, ev_w_a2=ev_w_a2, ev_w_g2=ev_w_g2, ev_k_k=ev_k_k, ev_k_a=ev_k_a, ev_r_k=ev_r_k,
               ev_lnx_w=ev_lnx_w, ev_lnx_b=ev_lnx_b, ev_w_out=ev_w_out, od_w_in=od_w_in,
               od_rel_bias=od_rel_bias, od_w_gk2=od_w_gk2, od_b_gk=od_b_gk, od_gnorm=od_gnorm,
               od_w_out=od_w_out, ffn_w_up=ffn_w_up, ffn_conv_w=ffn_conv_w, ffn_conv_b=ffn_conv_b,
               ffn_w_down=ffn_w_down)
    bp, tp, _ = x_prompt.shape
    bs, ts, _ = x_sample.shape
    dims = (bp, tp, bs, ts)
    depth = norm_mix.shape[0]
    xp = x_prompt.reshape(bp * tp, D_MODEL)
    xs = x_sample.reshape(bs * ts, D_MODEL)
    rope_p = _rope_tables(jnp.tile(jnp.arange(tp), bp))
    rope_s = _rope_tables(jnp.tile(PAST_LEN + jnp.arange(ts), bs))

    prm.update(ffn_w_up16=ffn_w_up.astype(BF16), ffn_w_down16=ffn_w_down.astype(BF16),
               ev_w_out16=ev_w_out.astype(BF16), od_w_out16=od_w_out.astype(BF16))

    ev, od, ffn_p, ffn_s = [], [], [], []
    for layer in range(depth):
        if layer % 2 == 0:
            mix_p, xs, outs = _even_layer(xp, xs, layer // 2, rope_p, rope_s, dims,
                                          (cache_mla_ckv, cache_mla_kpe, state_rwkv, state_rwkv_shift), prm)
            ev.append(outs)
        else:
            mix_p, xs, outs = _odd_layer(xp, xs, layer // 2, dims, (cache_band_k, cache_band_v, state_gla), prm)
            od.append(outs)
        final_g = norm_final if layer == depth - 1 else None
        xp, xs, tails, ext = _ffn_layer(xp, mix_p, xs, layer, dims, state_ffn_conv, prm, final_g)
        ffn_p.append(tails)
        ffn_s.append(ext)

    yp, ys = xp, xs
    stack = lambda lst, key: jnp.stack([d[key] for d in lst])
    return (yp.reshape(bp, tp, D_MODEL), ys.reshape(bs, ts, D_MODEL),
            stack(ev, "ckv_p"), stack(ev, "ckv_s"), stack(ev, "kpe_p"), stack(ev, "kpe_s"),
            stack(ev, "rw_p"), stack(ev, "rw_s"), stack(ev, "sh_p"), stack(ev, "sh_s"),
            stack(od, "bk_p"), stack(od, "bk_s"), stack(od, "bv_p"), stack(od, "bv_s"),
            stack(od, "gla_p"), stack(od, "gla_s"), jnp.stack(ffn_p), jnp.stack(ffn_s))
```

```python
import functools

import jax
import jax.numpy as jnp
from jax import lax
from jax.experimental import pallas as pl
from jax.experimental.pallas import tpu as pltpu

F32 = jnp.float32
BF16 = jnp.bfloat16

D_MODEL = 2048
CHUNK = 64
EPS = 1e-6
PAST_LEN = 1024

H_A = 8
QK_NOPE = 128
QK_ROPE = 64
V_A = 128
Q_LORA = 512
KV_LORA = 256
ROPE_THETA = 10000.0

H_B = 16
HD_B = 64
C_B = H_B * HD_B
W_LORA = 64
A_LORA = 64
G_LORA = 128
LNX_EPS = 64e-5

H_C = 16
HD_C = 64
C_C = H_C * HD_C
BAND = 8 * CHUNK
REL_CLIP = 128

H_D = 4
DK_D = 128
DV_D = 256
GK_LORA = 16
GATE_NORM = 16.0

D_FF = 5632
CONV_W = 3

LANES = 128
VMEM_LIMIT_BYTES = 60 * 2 ** 20
NEG_BIG = -1e30
ROW_TILE = 256
LOG2E = 1.4426950408889634
MLA_SCALE = (QK_NOPE + QK_ROPE) ** -0.5 * LOG2E
BAND_SCALE = HD_C ** -0.5 * LOG2E


def _cparams(*sem):
    return pltpu.CompilerParams(dimension_semantics=sem, vmem_limit_bytes=VMEM_LIMIT_BYTES)


def _resident(shape, index_map):
    return pl.BlockSpec(shape, index_map, pipeline_mode=pl.Buffered(1))


def _dot(a, b):
    return jnp.dot(a.astype(BF16), b.astype(BF16), preferred_element_type=F32)


def _dot_nt(a, b):
    return lax.dot_general(a.astype(BF16), b.astype(BF16), (((1,), (1,)), ((), ())),
                           preferred_element_type=F32)


def _dot_tn(a, b):
    return lax.dot_general(a.astype(BF16), b.astype(BF16), (((0,), (0,)), ((), ())),
                           preferred_element_type=F32)


def _rms(x, g):
    return x * lax.rsqrt(jnp.mean(x * x, axis=-1, keepdims=True) + EPS) * g


def _sigmoid(x):
    return 1.0 / (1.0 + jnp.exp(-x))


def _softplus(x):
    return jnp.maximum(x, 0.0) + jnp.log(1.0 + jnp.exp(-jnp.abs(x)))


def _cumsum_rows(x):
    n = x.shape[0]
    row = lax.broadcasted_iota(jnp.int32, x.shape, 0)
    s = 1
    while s < n:
        x = x + jnp.where(row >= s, pltpu.roll(x, s, 0), 0.0)
        s *= 2
    return x


def _norm_matmul_kernel(x_ref, g_ref, w_ref, o_ref):
    o_ref[...] = _dot(_rms(x_ref[...], g_ref[...]), w_ref[...])


def norm_matmul(x, g, w):
    m, k = x.shape
    n = w.shape[1]
    tm = ROW_TILE
    return pl.pallas_call(
        _norm_matmul_kernel,
        grid=(m // tm,),
        in_specs=[pl.BlockSpec((tm, k), lambda i: (i, 0)),
                  pl.BlockSpec((1, k), lambda i: (0, 0)),
                  _resident((k, n), lambda i: (0, 0))],
        out_specs=pl.BlockSpec((tm, n), lambda i: (i, 0)),
        out_shape=jax.ShapeDtypeStruct((m, n), F32),
        compiler_params=_cparams("parallel"),
        name="norm_matmul",
    )(x, g.reshape(1, k), w)


def _out_proj_kernel(res_ref, a1_ref, a2_ref, w1_ref, w2_ref, o_ref):
    o_ref[...] = res_ref[...] + _dot(a1_ref[...], w1_ref[...]) + _dot(a2_ref[...], w2_ref[...])


def out_proj(res, a1, a2, w_out, e):
    m, n = res.shape
    ka = a1.shape[1]
    tm = ROW_TILE
    return pl.pallas_call(
        _out_proj_kernel,
        grid=(m // tm,),
        in_specs=[pl.BlockSpec((tm, n), lambda i: (i, 0)),
                  pl.BlockSpec((tm, ka), lambda i: (i, 0)),
                  pl.BlockSpec((tm, ka), lambda i: (i, 0)),
                  _resident((None, ka, n), lambda i: (e, 0, 0)),
                  _resident((None, ka, n), lambda i: (e, 1, 0))],
        out_specs=pl.BlockSpec((tm, n), lambda i: (i, 0)),
        out_shape=jax.ShapeDtypeStruct((m, n), F32),
        compiler_params=_cparams("parallel"),
        name="out_proj",
    )(res, a1, a2, w_out, w_out)


COL_CQ = 3 * C_B
COL_CKV = COL_CQ + Q_LORA
COL_KPE = COL_CKV + KV_LORA
COL_WA = COL_KPE + 2 * QK_ROPE
COL_G = COL_WA + W_LORA + A_LORA
COLS_EVEN_EXT = COL_G + G_LORA
HEAD_W = 2 * LANES


def _mla_prep_kernel(cq_ref, ckv_ref, kpe_ref, cos_ref, sin_ref, qn_ref, kvn_ref, wq_ref, wkv_ref,
                     q_ref, ckv_out_ref, kpe_out_ref, *kv_refs):
    cos = cos_ref[...]
    sin = sin_ref[...]
    z = _dot(_rms(cq_ref[...], qn_ref[...]), wq_ref[...])
    ckvn = _rms(ckv_ref[...], kvn_ref[...])
    ckv_out_ref[...] = ckvn
    kp = kpe_ref[...]
    kr = kp * cos + pltpu.roll(kp, QK_ROPE, 1) * sin
    kpe_out_ref[...] = kr[:, :QK_ROPE]
    for h in range(H_A):
        c0 = h * HEAD_W
        q_ref[h, :, 0:LANES] = (z[:, c0:c0 + LANES] * MLA_SCALE).astype(BF16)
        t2 = z[:, c0 + LANES:c0 + HEAD_W]
        q_ref[h, :, LANES:HEAD_W] = ((t2 * cos + pltpu.roll(t2, QK_ROPE, 1) * sin) * MLA_SCALE).astype(BF16)
    if kv_refs:
        kx_ref, vt_ref = kv_refs
        kr16 = kr.astype(BF16)
        kv = _dot(ckvn, wkv_ref[...])
        for h in range(H_A):
            c0 = h * HEAD_W
            kx_ref[h, :, 0:LANES] = kv[:, c0:c0 + LANES].astype(BF16)
            kx_ref[h, :, LANES:HEAD_W] = kr16
            vt_ref[h, 0] = kv[:, c0 + LANES:c0 + HEAD_W].T.astype(BF16)


def mla_prep(proj, cos, sin, q_norm, kv_norm, wq, wkv, expand):
    m = proj.shape[0]
    tm = ATT_K if expand else ROW_TILE
    kv_specs, kv_shapes = [], []
    if expand:
        kv_specs = [pl.BlockSpec((H_A, tm, HEAD_W), lambda i: (0, i, 0)),
                    pl.BlockSpec((H_A, 1, V_A, tm), lambda i: (0, i, 0, 0))]
        kv_shapes = [jax.ShapeDtypeStruct((H_A, m, HEAD_W), BF16),
                     jax.ShapeDtypeStruct((H_A, m // tm, V_A, tm), BF16)]
    return pl.pallas_call(
        _mla_prep_kernel,
        grid=(m // tm,),
        in_specs=[pl.BlockSpec((tm, Q_LORA), lambda i: (i, COL_CQ // Q_LORA)),
                  pl.BlockSpec((tm, KV_LORA), lambda i: (i, COL_CKV // KV_LORA)),
                  pl.BlockSpec((tm, LANES), lambda i: (i, COL_KPE // LANES)),
                  pl.BlockSpec((tm, LANES), lambda i: (i, 0)),
                  pl.BlockSpec((tm, LANES), lambda i: (i, 0)),
                  pl.BlockSpec((1, Q_LORA), lambda i: (0, 0)),
                  pl.BlockSpec((1, KV_LORA), lambda i: (0, 0)),
                  _resident((Q_LORA, H_A * HEAD_W), lambda i: (0, 0)),
                  _resident((KV_LORA, H_A * HEAD_W), lambda i: (0, 0))],
        out_specs=[pl.BlockSpec((H_A, tm, HEAD_W), lambda i: (0, i, 0)),
                   pl.BlockSpec((tm, KV_LORA), lambda i: (i, 0)),
                   pl.BlockSpec((tm, QK_ROPE), lambda i: (i, 0))] + kv_specs,
        out_shape=[jax.ShapeDtypeStruct((H_A, m, HEAD_W), BF16),
                   jax.ShapeDtypeStruct((m, KV_LORA), F32),
                   jax.ShapeDtypeStruct((m, QK_ROPE), F32)] + kv_shapes,
        compiler_params=_cparams("parallel"),
        name="mla_prep",
    )(proj, proj, proj, cos, sin, q_norm.reshape(1, -1), kv_norm.reshape(1, -1), wq, wkv)


ATT_Q = 512
ATT_K = 512
MLA_HEADS_PER_STEP = 4


def _mla_prompt_kernel(q_ref, k_ref, vt_ref, o_ref):
    t, tk = ATT_Q, ATT_K
    qi = pl.program_id(2)
    n_full = (qi * t) // tk

    def block(j, carry, masked):
        start = pl.multiple_of(j * tk, tk)
        heads = range(MLA_HEADS_PER_STEP)
        s = [lax.dot_general(k_ref[g, pl.ds(start, tk), :], q_ref[g], (((1,), (1,)), ((), ())),
                             preferred_element_type=F32) for g in heads]
        if masked:
            kc = (lax.broadcasted_iota(jnp.int32, (tk, t), 0) + start) // CHUNK
            qc = (lax.broadcasted_iota(jnp.int32, (tk, t), 1) + qi * t) // CHUNK
            s = [jnp.where(kc <= qc, x, NEG_BIG) for x in s]
        m_new = [jnp.maximum(carry[g][0], jnp.max(s[g], axis=0, keepdims=True)) for g in heads]
        p = [jnp.exp2(s[g] - m_new[g]) for g in heads]
        pv = [jnp.dot(vt_ref[g, j], p[g].astype(BF16), preferred_element_type=F32) for g in heads]
        out = []
        for g in heads:
            m, l, acc = carry[g]
            alpha = jnp.exp2(m - m_new[g])
            out.append((m_new[g], alpha * l + jnp.sum(p[g], axis=0, keepdims=True), alpha * acc + pv[g]))
        return tuple(out)

    init = tuple((jnp.full((1, t), NEG_BIG, F32), jnp.zeros((1, t), F32), jnp.zeros((V_A, t), F32))
                 for _ in range(MLA_HEADS_PER_STEP))
    carry = lax.fori_loop(0, n_full, lambda j, c: block(j, c, False), init)
    carry = block(n_full, carry, True)
    for g in range(MLA_HEADS_PER_STEP):
        m, l, acc = carry[g]
        o_ref[:, g * V_A:(g + 1) * V_A] = (acc / l).T.astype(BF16)


def mla_prompt(q, kx, vt, batch, seq):
    t = ATT_Q
    g = MLA_HEADS_PER_STEP
    assert seq % ATT_K == 0 and ATT_K % t == 0 and H_A % g == 0
    nq = seq // t
    nk = seq // ATT_K
    return pl.pallas_call(
        _mla_prompt_kernel,
        grid=(batch, H_A // g, nq),
        in_specs=[pl.BlockSpec((g, t, HEAD_W), lambda b, h, i: (h, b * nq + i, 0)),
                  pl.BlockSpec((g, seq, HEAD_W), lambda b, h, i: (h, b, 0)),
                  pl.BlockSpec((g, nk, V_A, ATT_K), lambda b, h, i: (h, b, 0, 0))],
        out_specs=pl.BlockSpec((t, g * V_A), lambda b, h, i: (b * nq + i, h)),
        out_shape=jax.ShapeDtypeStruct((batch * seq, H_A * V_A), BF16),
        compiler_params=_cparams("parallel", "parallel", "arbitrary"),
        name="mla_prompt",
    )(q, kx, vt)


def _mla_sample_kernel(q_ref, ckvp_ref, kpep_ref, ckvn_ref, kpen_ref, wkv_ref, o_ref):
    seq = ckvn_ref.shape[0]
    ckvp = ckvp_ref[0].astype(BF16)
    ckvn = ckvn_ref[...].astype(BF16)
    kpep = kpep_ref[0].astype(BF16)
    kpen = kpen_ref[...].astype(BF16)
    heads = range(H_A)
    q_lat = jnp.concatenate([_dot_nt(q_ref[h, :, 0:QK_NOPE], wkv_ref[:, h * HEAD_W:h * HEAD_W + QK_NOPE])
                             for h in heads], axis=0)
    q_pe = jnp.concatenate([q_ref[h, :, LANES:LANES + QK_ROPE] for h in heads], axis=0)
    s_p = _dot_nt(q_lat, ckvp) + _dot_nt(q_pe, kpep)
    s_n = _dot_nt(q_lat, ckvn) + _dot_nt(q_pe, kpen)
    m = jnp.maximum(jnp.max(s_p, axis=-1, keepdims=True), jnp.max(s_n, axis=-1, keepdims=True))
    p_p = jnp.exp2(s_p - m)
    p_n = jnp.exp2(s_n - m)
    l = jnp.sum(p_p, axis=-1, keepdims=True) + jnp.sum(p_n, axis=-1, keepdims=True)
    o_lat = (_dot(p_p, ckvp) + _dot(p_n, ckvn)) / l
    for h in heads:
        o_ref[:, h * V_A:(h + 1) * V_A] = _dot(
            o_lat[h * seq:(h + 1) * seq], wkv_ref[:, h * HEAD_W + QK_NOPE:(h + 1) * HEAD_W]).astype(BF16)


def mla_sample(q, ckv_past, kpe_past, ckv_new, kpe_new, wkv, batch, seq):
    past = ckv_past.shape[1]
    return pl.pallas_call(
        _mla_sample_kernel,
        grid=(batch,),
        in_specs=[pl.BlockSpec((H_A, seq, HEAD_W), lambda b: (0, b, 0)),
                  pl.BlockSpec((1, past, KV_LORA), lambda b: (b, 0, 0)),
                  pl.BlockSpec((1, past, QK_ROPE), lambda b: (b, 0, 0)),
                  pl.BlockSpec((seq, KV_LORA), lambda b: (b, 0)),
                  pl.BlockSpec((seq, QK_ROPE), lambda b: (b, 0)),
                  _resident((KV_LORA, H_A * HEAD_W), lambda b: (0, 0))],
        out_specs=pl.BlockSpec((seq, H_A * V_A), lambda b: (b, 0)),
        out_shape=jax.ShapeDtypeStruct((batch * seq, H_A * V_A), BF16),
        compiler_params=_cparams("parallel"),
        name="mla_sample",
    )(q, ckv_past, kpe_past, ckv_new, kpe_new, wkv)


def _rwkv_prep_kernel(r_ref, k_ref, v_ref, wa_ref, g_ref, rp_ref, kp_ref, vp_ref, wap_ref, gp_ref,
                      mu_r_ref, mu_k_ref, mu_v_ref, mu_wa_ref, mu_g_ref,
                      w0_ref, ww2_ref, a0_ref, wa2_ref, wg2_ref, kk_ref, ka_ref,
                      r_out, lw_out, k_out, v_out, kk_out, a_out, g_out, *, tm, seq, first_rows):
    row = lax.broadcasted_iota(jnp.int32, (tm, 1), 0)
    if not first_rows:
        keep = ((pl.program_id(0) * tm) % seq != 0).astype(F32)

    def mix(p_ref, pp_ref, mu_ref):
        p = p_ref[...]
        rolled = pltpu.roll(p, 1, 0)
        if first_rows:
            prev = jnp.where(row % seq == 0, pp_ref[...], rolled)
        else:
            prev = jnp.where(row == 0, pp_ref[7:8, :] * keep, rolled)
        return p + (prev - p) * mu_ref[...]

    r_out[...] = mix(r_ref, rp_ref, mu_r_ref)
    v_out[...] = mix(v_ref, vp_ref, mu_v_ref)
    k = mix(k_ref, kp_ref, mu_k_ref)
    xwa = mix(wa_ref, wap_ref, mu_wa_ref)
    xg = mix(g_ref, gp_ref, mu_g_ref)
    w_log = -_softplus(-(w0_ref[...] + _dot(jnp.tanh(xwa), ww2_ref[...]))) - 0.5
    lw_out[...] = -jnp.exp(w_log)
    a = _sigmoid(a0_ref[...] + _dot(xwa, wa2_ref[...]))
    a_out[...] = a
    g_out[...] = _dot(_sigmoid(xg), wg2_ref[...])
    kk_out[...] = k * kk_ref[...]
    k_out[...] = k * (1.0 + (a - 1.0) * ka_ref[...])


def rwkv_prep(proj, first, mu_parts, w0, ww2p, a0, wa2p, wg2, k_k, k_a, seq):
    m = proj.shape[0]
    tm = ROW_TILE
    wide = lambda c: pl.BlockSpec((tm, C_B), lambda i: (i, c))
    narrow = lambda c: pl.BlockSpec((tm, LANES), lambda i: (i, c))
    vec = lambda n: pl.BlockSpec((1, n), lambda i: (0, 0))
    out = jax.ShapeDtypeStruct((m, C_B), F32)
    if first is None:
        assert seq % tm == 0
        before = lambda i: jnp.maximum(i * (tm // 8) - 1, 0)
        prev_specs = [pl.BlockSpec((8, C_B), lambda i, c=c: (before(i), c)) for c in range(3)]
        prev_specs += [pl.BlockSpec((8, LANES), lambda i, c=c: (before(i), c))
                       for c in (COL_WA // LANES, COL_G // LANES)]
        prev = proj
    else:
        assert tm % seq == 0
        prev_specs = [wide(0), wide(1), wide(2), narrow(3 * C_B // LANES), narrow(3 * C_B // LANES + 1)]
        prev = first
    return pl.pallas_call(
        functools.partial(_rwkv_prep_kernel, tm=tm, seq=seq, first_rows=first is not None),
        grid=(m // tm,),
        in_specs=[wide(0), wide(1), wide(2), narrow(COL_WA // LANES), narrow(COL_G // LANES),
                  *prev_specs,
                  vec(C_B), vec(C_B), vec(C_B), vec(LANES), vec(LANES),
                  vec(C_B), _resident((LANES, C_B), lambda i: (0, 0)),
                  vec(C_B), _resident((LANES, C_B), lambda i: (0, 0)),
                  _resident((G_LORA, C_B), lambda i: (0, 0)), vec(C_B), vec(C_B)],
        out_specs=[pl.BlockSpec((tm, C_B), lambda i: (i, 0))] * 7,
        out_shape=[out] * 7,
        compiler_params=_cparams("parallel"),
        name="rwkv_prep",
    )(proj, proj, proj, proj, proj, prev, prev, prev, prev, prev,
      *mu_parts, w0.reshape(1, -1), ww2p, a0.reshape(1, -1), wa2p, wg2,
      k_k.reshape(1, -1), k_a.reshape(1, -1))


RWKV_PAIRS = 8


def _rwkv_chunk_kernel(r_ref, lw_ref, k_ref, v_ref, kk_ref, a_ref, g_ref, rk_ref, lnw_ref, lnb_ref,
                       m0_ref, o_ref, mout_ref, m_scr, *, chunk, pairs, seqs):
    c = pl.program_id(2)
    nc = pl.num_programs(2)
    L = chunk
    L2 = 2 * L

    @pl.when(c == 0)
    def _():
        m_scr[...] = m0_ref[...]

    lane = lax.broadcasted_iota(jnp.int32, (L, LANES), 1)
    low = lane < HD_B
    row_p = lax.broadcasted_iota(jnp.int32, (L, L2), 0)
    col_p = lax.broadcasted_iota(jnp.int32, (L, L2), 1) % L
    strict = row_p > col_p
    incl = row_p >= col_p
    eye_p = (row_p == col_p).astype(F32)
    same_blk = ((lax.broadcasted_iota(jnp.int32, (L2, L2), 0) < L)
                == (lax.broadcasted_iota(jnp.int32, (L2, L2), 1) < L))
    r128 = lax.broadcasted_iota(jnp.int32, (LANES, LANES), 0)
    c128 = lax.broadcasted_iota(jnp.int32, (LANES, LANES), 1)
    same_head = (r128 < HD_B) == (c128 < HD_B)
    diag = r128 == c128

    def seg_sum(x):
        s0 = jnp.sum(jnp.where(low, x, 0.0), axis=-1, keepdims=True)
        s1 = jnp.sum(jnp.where(low, 0.0, x), axis=-1, keepdims=True)
        return jnp.where(low, s0, s1)

    def split_heads(x):
        return jnp.concatenate([jnp.where(low, x, 0.0), jnp.where(low, 0.0, x)], axis=0)

    def block_diag(x):
        return jnp.where(same_blk, jnp.concatenate([x, x], axis=0), 0.0)

    P = range(seqs * pairs)
    sqs = [p // pairs for p in P]
    sls = [slice((p % pairs) * LANES, (p % pairs + 1) * LANES) for p in P]

    def prep(b, sl):
        r = r_ref[b, :, sl]
        lw = lw_ref[b, :, sl]
        k = k_ref[b, :, sl]
        kkr = kk_ref[b, :, sl]
        kk = kkr / jnp.maximum(jnp.sqrt(seg_sum(kkr * kkr)), 1e-12)
        bvec = kk * a_ref[b, :, sl]
        ci = _cumsum_rows(lw)
        c_end = ci[L - 1:L, :]
        e_neg = jnp.exp(-ci)
        e_end = jnp.exp(c_end - ci)
        return dict(r=r, k=k, v=v_ref[b, :, sl], c_end=c_end,
                    at=-kk * jnp.exp(ci - lw), bt=bvec * e_neg, kt=k * e_neg, rt=r * jnp.exp(ci),
                    bh=bvec * e_end, kh=k * e_end)

    d = [prep(sqs[p], sls[p]) for p in P]
    sc = [_dot_nt(jnp.concatenate([x["at"], x["rt"]], axis=0),
                  jnp.concatenate([split_heads(x["bt"]), split_heads(x["kt"])], axis=0)) for x in d]
    nab = [jnp.where(strict, s[0:L, 0:L2], 0.0) for s in sc]
    nak = [jnp.where(strict, s[0:L, L2:2 * L2], 0.0) for s in sc]
    lrb = [jnp.where(incl, s[L:L2, 0:L2], 0.0) for s in sc]
    lrk = [jnp.where(incl, s[L:L2, L2:2 * L2], 0.0) for s in sc]
    v_split = [split_heads(x["v"]) for x in d]

    t_inv = [eye_p + n for n in nab]
    pw = [_dot(n, block_diag(n)) for n in nab]
    nakv = [_dot(nak[p], v_split[p]) for p in P]
    m = 2
    while m < L:
        if 2 * m < L:
            res = [_dot(pw[p], jnp.concatenate([block_diag(pw[p]), block_diag(t_inv[p])], axis=1)) for p in P]
            pw = [x[:, 0:L2] for x in res]
            t_inv = [t_inv[p] + res[p][:, L2:2 * L2] for p in P]
        else:
            t_inv = [t_inv[p] + _dot(pw[p], block_diag(t_inv[p])) for p in P]
        m *= 2
    yk = [_dot(lrk[p], v_split[p]) for p in P]
    hk = [_dot_tn(d[p]["kh"], d[p]["v"]) for p in P]

    wu = [_dot(t_inv[p], jnp.concatenate([split_heads(d[p]["at"]), split_heads(nakv[p])], axis=1)) for p in P]
    qy = [_dot(lrb[p], jnp.concatenate([split_heads(wu[p][:, 0:LANES]), split_heads(wu[p][:, LANES:])], axis=1))
          for p in P]
    gh = [_dot_tn(d[p]["bh"], wu[p]) for p in P]
    q = [d[p]["rt"] + qy[p][:, 0:LANES] for p in P]
    gmat = [jnp.where(diag, jnp.broadcast_to(jnp.exp(d[p]["c_end"]), (LANES, LANES)), 0.0)
            + jnp.where(same_head, gh[p][:, 0:LANES], 0.0) for p in P]
    ym = [_dot(jnp.concatenate([q[p], gmat[p]], axis=0), m_scr[sqs[p], p % pairs]) for p in P]
    for p in P:
        m_scr[sqs[p], p % pairs] = (ym[p][L:L + LANES]
                                    + jnp.where(same_head, gh[p][:, LANES:] + hk[p], 0.0))
    for p in P:
        sl = sls[p]
        y = ym[p][0:L] + qy[p][:, LANES:] + yk[p]
        mean = seg_sum(y) * (1.0 / HD_B)
        dev = y - mean
        var = seg_sum(dev * dev) * (1.0 / HD_B)
        yn = dev * lax.rsqrt(var + LNX_EPS) * lnw_ref[:, sl] + lnb_ref[:, sl]
        bonus = seg_sum(d[p]["r"] * d[p]["k"] * rk_ref[:, sl]) * d[p]["v"]
        o_ref[sqs[p], :, sl] = ((yn + bonus) * g_ref[sqs[p], :, sl]).astype(BF16)

    @pl.when(c == nc - 1)
    def _():
        mout_ref[...] = m_scr[...]


def rwkv_chunk(acts, r_k, lnx_w, lnx_b, m0, batch, seq, chunk, pairs, seqs):
    nc = seq // chunk
    npair = C_B // LANES
    w = pairs * LANES
    act = pl.BlockSpec((seqs, chunk, w), lambda b, pg, c: (b, c, pg))
    prm = pl.BlockSpec((1, w), lambda b, pg, c: (0, pg))
    st = pl.BlockSpec((seqs, pairs, LANES, LANES), lambda b, pg, c: (b, pg, 0, 0))
    o, m = pl.pallas_call(
        functools.partial(_rwkv_chunk_kernel, chunk=chunk, pairs=pairs, seqs=seqs),
        grid=(batch // seqs, npair // pairs, nc),
        in_specs=[act] * 7 + [prm] * 3 + [st],
        out_specs=[act, st],
        out_shape=[jax.ShapeDtypeStruct((batch, seq, C_B), BF16),
                   jax.ShapeDtypeStruct((batch, npair, LANES, LANES), F32)],
        scratch_shapes=[pltpu.VMEM((seqs, pairs, LANES, LANES), F32)],
        compiler_params=_cparams("parallel", "parallel", "arbitrary"),
        name="rwkv_chunk",
    )(*(t.reshape(batch, seq, C_B) for t in acts),
      r_k.reshape(1, -1), lnx_w.reshape(1, -1), lnx_b.reshape(1, -1), m0)
    return o.reshape(batch * seq, C_B), m


BAND_Q = 256
BAND_SEGS = BAND // BAND_Q + 1
BAND_PAIRS = 4


def _band_prompt_kernel(q_ref, k_ref, v_ref, bias_ref, o_ref):
    qb = pl.program_id(2)
    t = BAND_Q
    lane = lax.broadcasted_iota(jnp.int32, (t, LANES), 1)
    low = lane < HD_C
    row_low = lax.broadcasted_iota(jnp.int32, (LANES, t), 0) < HD_C
    pairs = range(BAND_PAIRS)
    heads = range(2 * BAND_PAIRS)
    segs = range(BAND_SEGS)
    sls = [slice(pr * LANES, (pr + 1) * LANES) for pr in pairs]
    starts, valid = [], []
    for seg in segs:
        kb = qb - (BAND_SEGS - 1) + seg
        starts.append(pl.multiple_of(jnp.maximum(kb, 0) * t, t))
        valid.append(kb >= 0)
    ks = [[k_ref[pl.ds(starts[seg], t), sl].astype(BF16) for seg in segs] for sl in sls]
    vt = [[v_ref[pl.ds(starts[seg], t), sl].T.astype(BF16) for seg in segs] for sl in sls]
    q = [q_ref[:, sl] * BAND_SCALE for sl in sls]
    qh = [jnp.where(low if h % 2 == 0 else jnp.logical_not(low), q[h // 2], 0.0).astype(BF16) for h in heads]
    s = [[jnp.where(valid[seg],
                    lax.dot_general(ks[h // 2][seg], qh[h], (((1,), (1,)), ((), ())),
                                    preferred_element_type=F32)
                    + bias_ref[h, seg * t:(seg + 1) * t, :], NEG_BIG)
          for seg in segs] for h in heads]
    m = [functools.reduce(jnp.maximum, [jnp.max(s[h][seg], axis=0, keepdims=True) for seg in segs])
         for h in heads]
    p = [[jnp.exp2(s[h][seg] - m[h]) for seg in segs] for h in heads]
    l = [sum(jnp.sum(p[h][seg], axis=0, keepdims=True) for seg in segs) for h in heads]
    acc = [sum(jnp.dot(vt[h // 2][seg], p[h][seg].astype(BF16), preferred_element_type=F32) for seg in segs)
           for h in heads]
    for pr in pairs:
        o_ref[:, sls[pr]] = jnp.where(row_low, acc[2 * pr] / l[2 * pr],
                                      acc[2 * pr + 1] / l[2 * pr + 1]).T.astype(BF16)


def band_prompt(proj, bias, batch, seq):
    t = BAND_Q
    nq = seq // t
    w = BAND_PAIRS * LANES
    ngrp = C_C // w
    return pl.pallas_call(
        _band_prompt_kernel,
        grid=(ngrp, batch, nq),
        in_specs=[pl.BlockSpec((t, w), lambda hp, b, i: (b * nq + i, hp)),
                  pl.BlockSpec((seq, w), lambda hp, b, i: (b, ngrp + hp)),
                  pl.BlockSpec((seq, w), lambda hp, b, i: (b, 2 * ngrp + hp)),
                  pl.BlockSpec((2 * BAND_PAIRS, BAND_SEGS * t, t), lambda hp, b, i: (hp, 0, 0))],
        out_specs=pl.BlockSpec((t, w), lambda hp, b, i: (b * nq + i, hp)),
        out_shape=jax.ShapeDtypeStruct((batch * seq, C_C), BF16),
        compiler_params=_cparams("parallel", "parallel", "parallel"),
        name="band_prompt",
    )(proj, proj, proj, bias)


def _band_sample_kernel(q_ref, kn_ref, vn_ref, kp_ref, vp_ref, biasp_ref, biasn_ref, o_ref):
    t = q_ref.shape[0]
    lane = lax.broadcasted_iota(jnp.int32, (t, LANES), 1)
    low = lane < HD_C
    pairs = range(C_C // LANES)
    heads = range(H_C)
    sls = [slice(hp * LANES, (hp + 1) * LANES) for hp in pairs]
    kp = [kp_ref[0, :, sl].astype(BF16) for sl in sls]
    vp = [vp_ref[0, :, sl].astype(BF16) for sl in sls]
    kn = [kn_ref[:, sl].astype(BF16) for sl in sls]
    vn = [vn_ref[:, sl].astype(BF16) for sl in sls]
    q = [q_ref[:, sl] * BAND_SCALE for sl in sls]
    qh = [jnp.where(low if h % 2 == 0 else jnp.logical_not(low), q[h // 2], 0.0).astype(BF16) for h in heads]
    s_p = [_dot_nt(qh[h], kp[h // 2]) + biasp_ref[h] for h in heads]
    s_n = [_dot_nt(qh[h], kn[h // 2]) + biasn_ref[h] for h in heads]
    m = [jnp.maximum(jnp.max(s_p[h], axis=-1, keepdims=True), jnp.max(s_n[h], axis=-1, keepdims=True))
         for h in heads]
    p_p = [jnp.exp2(s_p[h] - m[h]) for h in heads]
    p_n = [jnp.exp2(s_n[h] - m[h]) for h in heads]
    l = [jnp.sum(p_p[h], axis=-1, keepdims=True) + jnp.sum(p_n[h], axis=-1, keepdims=True) for h in heads]
    o = [(_dot(p_p[h], vp[h // 2]) + _dot(p_n[h], vn[h // 2])) / l[h] for h in heads]
    for hp in pairs:
        o_ref[:, sls[hp]] = jnp.where(low, o[2 * hp], o[2 * hp + 1]).astype(BF16)


def band_sample(proj, k_past, v_past, bias_p, bias_n, batch, seq):
    past = k_past.shape[1]
    new = lambda col: pl.BlockSpec((seq, C_C), lambda b: (b, col))
    old = pl.BlockSpec((1, past, C_C), lambda b: (b, 0, 0))
    return pl.pallas_call(
        _band_sample_kernel,
        grid=(batch,),
        in_specs=[new(0), new(1), new(2), old, old,
                  _resident((H_C, seq, past), lambda b: (0, 0, 0)),
                  _resident((H_C, seq, seq), lambda b: (0, 0, 0))],
        out_specs=pl.BlockSpec((seq, C_C), lambda b: (b, 0)),
        out_shape=jax.ShapeDtypeStruct((batch * seq, C_C), BF16),
        compiler_params=_cparams("parallel"),
        name="band_sample",
    )(proj, proj, proj, k_past, v_past, bias_p, bias_n)


def _gla_kernel(q_ref, k_ref, v_ref, go_ref, gk_ref, wgk_ref, bgk_ref, gn_ref, s0_ref,
                o_ref, sout_ref, s_scr, *, chunk):
    c = pl.program_id(1)
    nc = pl.num_programs(1)
    L = chunk

    @pl.when(c == 0)
    def _():
        s_scr[...] = s0_ref[0]

    causal = (lax.broadcasted_iota(jnp.int32, (L, L), 0) >= lax.broadcasted_iota(jnp.int32, (L, L), 1))
    diag = (lax.broadcasted_iota(jnp.int32, (DK_D, DK_D), 0)
            == lax.broadcasted_iota(jnp.int32, (DK_D, DK_D), 1))
    H = range(H_D)
    ksl = [slice(h * DK_D, (h + 1) * DK_D) for h in H]
    vsl = [slice(h * DV_D, (h + 1) * DV_D) for h in H]
    x = _dot(gk_ref[...], wgk_ref[...]) + bgk_ref[...]
    b = _cumsum_rows(-_softplus(-x) / GATE_NORM)
    eb = jnp.exp(b)
    enb = jnp.exp(-b)
    b_end = b[L - 1:L, :]
    ed = jnp.exp(b_end - b)
    e_end = jnp.exp(b_end)
    qe = [q_ref[:, ksl[h]] * DK_D ** -0.5 * eb[:, ksl[h]] for h in H]
    vh = [v_ref[:, vsl[h]].astype(BF16) for h in H]
    att = [jnp.where(causal, _dot_nt(qe[h], k_ref[:, ksl[h]] * enb[:, ksl[h]]), 0.0) for h in H]
    s_prev = [s_scr[h] for h in H]
    o = [_dot(att[h], vh[h]) + _dot(qe[h], s_prev[h]) for h in H]
    kv = [_dot_tn(k_ref[:, ksl[h]] * ed[:, ksl[h]], vh[h]) for h in H]
    for h in H:
        e_col = jnp.sum(jnp.where(diag, jnp.broadcast_to(e_end[:, ksl[h]], (DK_D, DK_D)), 0.0),
                        axis=1, keepdims=True)
        s_scr[h] = s_prev[h] * e_col + kv[h]
    for h in H:
        on = o[h] * lax.rsqrt(jnp.mean(o[h] * o[h], axis=-1, keepdims=True) + EPS) * gn_ref[...]
        gate = go_ref[:, vsl[h]]
        o_ref[:, vsl[h]] = (on * (gate * _sigmoid(gate))).astype(BF16)

    @pl.when(c == nc - 1)
    def _():
        sout_ref[0] = s_scr[...]


COL_QD = 3 * C_C
COL_KD = COL_QD + H_D * DK_D
COL_VD = COL_KD + H_D * DK_D
COL_GO = COL_VD + H_D * DV_D
COL_GK = COL_GO + H_D * DV_D
COLS_ODD_EXT = COL_GK + LANES


def gla(proj, wgk, bgk, gnorm, s0, row0, batch, seq, chunk):
    nc = seq // chunk
    kw = H_D * DK_D
    vw = H_D * DV_D
    blk0 = row0 // chunk
    row = lambda col: (lambda b, c: (blk0 + b * nc + c, col))
    return pl.pallas_call(
        functools.partial(_gla_kernel, chunk=chunk),
        grid=(batch, nc),
        in_specs=[pl.BlockSpec((chunk, kw), row(COL_QD // kw)),
                  pl.BlockSpec((chunk, kw), row(COL_KD // kw)),
                  pl.BlockSpec((chunk, vw), row(COL_VD // vw)),
                  pl.BlockSpec((chunk, vw), row(COL_GO // vw)),
                  pl.BlockSpec((chunk, LANES), row(COL_GK // LANES)),
                  _resident((LANES, kw), lambda b, c: (0, 0)),
                  pl.BlockSpec((1, kw), lambda b, c: (0, 0)),
                  pl.BlockSpec((1, DV_D), lambda b, c: (0, 0)),
                  pl.BlockSpec((1, H_D, DK_D, DV_D), lambda b, c: (b, 0, 0, 0))],
        out_specs=[pl.BlockSpec((chunk, vw), lambda b, c: (b * nc + c, 0)),
                   pl.BlockSpec((1, H_D, DK_D, DV_D), lambda b, c: (b, 0, 0, 0))],
        out_shape=[jax.ShapeDtypeStruct((batch * seq, vw), BF16),
                   jax.ShapeDtypeStruct((batch, H_D, DK_D, DV_D), F32)],
        scratch_shapes=[pltpu.VMEM((H_D, DK_D, DV_D), F32)],
        compiler_params=_cparams("parallel", "arbitrary"),
        name="gla",
    )(proj, proj, proj, proj, proj, wgk, bgk.reshape(1, -1), gnorm.reshape(1, -1), s0)


FFN_TM = 512
FFN_TF = 512
HALO = 16


def _conv3(u_scr, cw_ref, cb_ref, cols, tm, prev1=None, prev2=None):
    u0 = u_scr[HALO:HALO + tm, cols]
    u1 = u_scr[HALO - 1:HALO - 1 + tm, cols]
    u2 = u_scr[HALO - 2:HALO - 2 + tm, cols]
    if prev1 is not None:
        u1 = prev1(u1)
        u2 = prev2(u2)
    return cb_ref[:, cols] + cw_ref[2:3, cols] * u0 + cw_ref[1:2, cols] * u1 + cw_ref[0:1, cols] * u2


def _ffn_prompt_kernel(x_ref, a1_ref, a2_ref, xh_ref, a1h_ref, a2h_ref, w1_ref, w2_ref, g_ref, gf_ref,
                       wg_ref, wv_ref, cwg_ref, cwv_ref, cbg_ref, cbv_ref, wd_ref,
                       o_ref, tg_ref, tv_ref, h_scr, *u_scrs, tm, blocks_per_seq, final_norm_out):
    i = pl.program_id(0)
    j = pl.program_id(1)
    nj = pl.num_programs(1)

    @pl.when(j == 0)
    def _():
        w1 = w1_ref[...]
        w2 = w2_ref[...]
        xm = x_ref[...] + _dot(a1_ref[...], w1) + _dot(a2_ref[...], w2)
        o_ref[...] = xm
        h_scr[HALO:HALO + tm, :] = _rms(xm, g_ref[...]).astype(BF16)
        keep = (i % blocks_per_seq != 0).astype(F32)
        xmh = xh_ref[...] + _dot(a1h_ref[...], w1) + _dot(a2h_ref[...], w2)
        h_scr[0:HALO, :] = (_rms(xmh, g_ref[...]) * keep).astype(BF16)

    h = h_scr[...]
    ug_scr, uv_scr = u_scrs
    ug_scr[...] = jnp.dot(h, wg_ref[...], preferred_element_type=F32)
    uv_scr[...] = jnp.dot(h, wv_ref[...], preferred_element_type=F32)
    acts = []
    for c in range(0, ug_scr.shape[1], LANES):
        lanes = slice(c, c + LANES)
        cg = _conv3(ug_scr, cwg_ref, cbg_ref, lanes, tm)
        cv = _conv3(uv_scr, cwv_ref, cbv_ref, lanes, tm)
        acts.append((cg * _sigmoid(cg) * cv).astype(BF16))
    o_ref[...] += jnp.dot(jnp.concatenate(acts, axis=1), wd_ref[...], preferred_element_type=F32)
    tg_ref[0] = ug_scr[HALO + tm - 8:HALO + tm, :]
    tv_ref[0] = uv_scr[HALO + tm - 8:HALO + tm, :]

    if final_norm_out:
        @pl.when(j == nj - 1)
        def _():
            o_ref[...] = _rms(o_ref[...], gf_ref[...])


def ffn_prompt(x, a1, a2, w_out, e, g, w_up, w_down, layer, conv_w, conv_b, seq, final_g=None):
    m, k = x.shape
    ka = a1.shape[1]
    tm, tf = FFN_TM, FFN_TF
    nj = D_FF // tf
    nblk = m // tm
    bps = seq // tm
    halo_blocks = tm // HALO
    cb = conv_b.reshape(1, -1)
    gf = (g if final_g is None else final_g).reshape(1, k)
    before = lambda i, j: (jnp.maximum(i * halo_blocks - 1, 0), 0)
    return pl.pallas_call(
        functools.partial(_ffn_prompt_kernel, tm=tm, blocks_per_seq=bps, final_norm_out=final_g is not None),
        grid=(nblk, nj),
        in_specs=[pl.BlockSpec((tm, k), lambda i, j: (i, 0)),
                  pl.BlockSpec((tm, ka), lambda i, j: (i, 0)),
                  pl.BlockSpec((tm, ka), lambda i, j: (i, 0)),
                  pl.BlockSpec((HALO, k), before),
                  pl.BlockSpec((HALO, ka), before),
                  pl.BlockSpec((HALO, ka), before),
                  _resident((None, ka, k), lambda i, j: (e, 0, 0)),
                  _resident((None, ka, k), lambda i, j: (e, 1, 0)),
                  pl.BlockSpec((1, k), lambda i, j: (0, 0)),
                  pl.BlockSpec((1, k), lambda i, j: (0, 0)),
                  pl.BlockSpec((None, k, tf), lambda i, j: (layer, 0, j)),
                  pl.BlockSpec((None, k, tf), lambda i, j: (layer, 0, nj + j)),
                  pl.BlockSpec((CONV_W, tf), lambda i, j: (0, j)),
                  pl.BlockSpec((CONV_W, tf), lambda i, j: (0, nj + j)),
                  pl.BlockSpec((1, tf), lambda i, j: (0, j)),
                  pl.BlockSpec((1, tf), lambda i, j: (0, nj + j)),
                  pl.BlockSpec((None, tf, k), lambda i, j: (layer, j, 0))],
        out_specs=[pl.BlockSpec((tm, k), lambda i, j: (i, 0)),
                   pl.BlockSpec((1, 8, tf), lambda i, j: (i, 0, j)),
                   pl.BlockSpec((1, 8, tf), lambda i, j: (i, 0, j))],
        out_shape=[jax.ShapeDtypeStruct((m, k), F32),
                   jax.ShapeDtypeStruct((nblk, 8, D_FF), F32),
                   jax.ShapeDtypeStruct((nblk, 8, D_FF), F32)],
        scratch_shapes=[pltpu.VMEM((tm + HALO, k), BF16)] + [pltpu.VMEM((tm + HALO, tf), F32)] * 2,
        compiler_params=_cparams("parallel", "arbitrary"),
        name="ffn_prompt",
    )(x, a1, a2, x, a1, a2, w_out, w_out, g.reshape(1, k), gf, w_up, w_up, conv_w, conv_w, cb, cb, w_down)


def _ffn_sample_kernel(x_ref, g_ref, gf_ref, wg_ref, wv_ref, cwg_ref, cwv_ref, cbg_ref, cbv_ref, wd_ref,
                       sg_ref, sv_ref,
                       o_ref, ug_ref, uv_ref, h_scr, acc_scr, ug_scr, uv_scr, *, tm, seq, final_norm_out):
    j = pl.program_id(0)
    nj = pl.num_programs(0)
    nseq = tm // seq
    tf = ug_scr.shape[1]
    pos = lax.broadcasted_iota(jnp.int32, (nseq, seq, 1), 1)
    cols = slice(0, tf)

    def before(s_ref, back):
        def fix(u):
            u = u.reshape(nseq, seq, tf)
            for p in range(back):
                u = jnp.where(pos == p, s_ref[:, CONV_W - 1 - back + p:CONV_W - back + p, :], u)
            return u.reshape(tm, tf)
        return fix

    @pl.when(j == 0)
    def _():
        h_scr[...] = _rms(x_ref[...], g_ref[...]).astype(BF16)
        acc_scr[...] = jnp.zeros_like(acc_scr)
        ug_scr[0:HALO, :] = jnp.zeros((HALO, ug_scr.shape[1]), F32)
        uv_scr[0:HALO, :] = jnp.zeros((HALO, uv_scr.shape[1]), F32)

    h = h_scr[...]
    ug = jnp.dot(h, wg_ref[...], preferred_element_type=F32)
    uv = jnp.dot(h, wv_ref[...], preferred_element_type=F32)
    ug_scr[HALO:HALO + tm, :] = ug
    uv_scr[HALO:HALO + tm, :] = uv
    ug_ref[...] = ug
    uv_ref[...] = uv
    cg = _conv3(ug_scr, cwg_ref, cbg_ref, cols, tm, before(sg_ref, 1), before(sg_ref, 2))
    cv = _conv3(uv_scr, cwv_ref, cbv_ref, cols, tm, before(sv_ref, 1), before(sv_ref, 2))
    act = cg * _sigmoid(cg) * cv
    acc_scr[...] += _dot(act, wd_ref[...])

    @pl.when(j == nj - 1)
    def _():
        out = x_ref[...] + acc_scr[...]
        o_ref[...] = _rms(out, gf_ref[...]) if final_norm_out else out


def ffn_sample(x, g, w_up, w_down, layer, conv_w, conv_b, state, seq, final_g=None):
    m, k = x.shape
    tm, tf = m, FFN_TF
    nj = D_FF // tf
    nseq = m // seq
    cb = conv_b.reshape(1, -1)
    gf = (g if final_g is None else final_g).reshape(1, k)
    gate = lambda shape: pl.BlockSpec(shape, lambda j: (0, j))
    val = lambda shape: pl.BlockSpec(shape, lambda j: (0, nj + j))
    carried = lambda off: pl.BlockSpec((nseq, CONV_W - 1, tf), lambda j: (0, 0, off + j))
    return pl.pallas_call(
        functools.partial(_ffn_sample_kernel, tm=tm, seq=seq, final_norm_out=final_g is not None),
        grid=(nj,),
        in_specs=[pl.BlockSpec((tm, k), lambda j: (0, 0)),
                  pl.BlockSpec((1, k), lambda j: (0, 0)),
                  pl.BlockSpec((1, k), lambda j: (0, 0)),
                  pl.BlockSpec((None, k, tf), lambda j: (layer, 0, j)),
                  pl.BlockSpec((None, k, tf), lambda j: (layer, 0, nj + j)),
                  gate((CONV_W, tf)), val((CONV_W, tf)),
                  gate((1, tf)), val((1, tf)),
                  pl.BlockSpec((None, tf, k), lambda j: (layer, j, 0)),
                  carried(0), carried(nj)],
        out_specs=[pl.BlockSpec((tm, k), lambda j: (0, 0)), gate((tm, tf)), gate((tm, tf))],
        out_shape=[jax.ShapeDtypeStruct((m, k), F32),
                   jax.ShapeDtypeStruct((m, D_FF), F32),
                   jax.ShapeDtypeStruct((m, D_FF), F32)],
        scratch_shapes=[pltpu.VMEM((tm, k), BF16), pltpu.VMEM((tm, k), F32),
                        pltpu.VMEM((tm + HALO, tf), F32), pltpu.VMEM((tm + HALO, tf), F32)],
        compiler_params=_cparams("arbitrary"),
        name="ffn_sample",
    )(x, g.reshape(1, k), gf, w_up, w_up, conv_w, conv_w, cb, cb, w_down, state, state)


def _rot_cols(w):
    half = QK_ROPE // 2
    return jnp.concatenate([-w[..., half:], w[..., :half]], axis=-1)


def _rope_tables(pos):
    half = QK_ROPE // 2
    inv = ROPE_THETA ** (-jnp.arange(half, dtype=F32) / half)
    ang = pos.astype(F32)[:, None] * inv[None, :]
    cos = jnp.cos(ang)
    sin = jnp.sin(ang)
    zeros = jnp.zeros((pos.shape[0], LANES - QK_ROPE), F32)
    return (jnp.concatenate([cos, cos, zeros], axis=1), jnp.concatenate([sin, sin, zeros], axis=1))


def _pair_state_in(s):
    b = s.shape[0]
    m = jnp.swapaxes(s, -1, -2).reshape(b, H_B // 2, 2, HD_B, HD_B)
    z = jnp.zeros_like(m[:, :, 0])
    top = jnp.concatenate([m[:, :, 0], z], axis=-1)
    bot = jnp.concatenate([z, m[:, :, 1]], axis=-1)
    return jnp.concatenate([top, bot], axis=-2)


def _pair_state_out(m):
    b = m.shape[0]
    h0 = m[:, :, :HD_B, :HD_B]
    h1 = m[:, :, HD_B:, HD_B:]
    s = jnp.stack([h0, h1], axis=2).reshape(b, H_B, HD_B, HD_B)
    return jnp.swapaxes(s, -1, -2)


def _band_bias_prompt(table):
    t = BAND_Q
    span = BAND_SEGS * t
    nd = t + span - 1
    d = (t - 1) - jnp.arange(nd + 1)
    w = table[:, jnp.clip(d + BAND, -REL_CLIP, REL_CLIP) + REL_CLIP]
    rows = jnp.tile(w, (1, t))[:, :t * nd].reshape(table.shape[0], t, nd)
    bias = rows[:, :, t - 1:t - 1 + span]
    q = jnp.arange(t)[:, None]
    j = jnp.arange(span)[None, :]
    lo = (q // CHUNK) * CHUNK
    allowed = (j >= lo) & (j < lo + BAND + CHUNK)
    return jnp.swapaxes(jnp.where(allowed[None], bias * LOG2E, NEG_BIG), 1, 2).astype(F32)


def _band_bias_sample(table, seq, past):
    q_pos = PAST_LEN + jnp.arange(seq)
    k_pos = PAST_LEN - past + jnp.arange(past + seq)
    idx = jnp.clip(q_pos[:, None] - k_pos[None, :], -REL_CLIP, REL_CLIP) + REL_CLIP
    bias = table[:, idx].astype(F32) * LOG2E
    return bias[:, :, :past], bias[:, :, past:]


def _last_rows(proj, nseq, seq):
    last = proj.reshape(nseq, seq, -1)[:, -1]
    return jnp.concatenate([last[:, :3 * C_B], last[:, COL_WA:]], axis=1)


def _even_layer(xp, xs, e, rope_p, rope_s, dims, caches, prm):
    bp, tp, bs, ts = dims
    cache_ckv, cache_kpe, state_rwkv, state_shift = caches
    w_in = prm["ev_w_in"][e]
    wa_cols = w_in[:, :Q_LORA + KV_LORA]
    w_kpe = w_in[:, Q_LORA + KV_LORA:Q_LORA + KV_LORA + QK_ROPE]
    w_b = w_in[:, Q_LORA + KV_LORA + QK_ROPE:]
    w_ext = jnp.concatenate([w_b[:, :3 * C_B], wa_cols, w_kpe, _rot_cols(w_kpe), w_b[:, 3 * C_B:]],
                            axis=1).astype(BF16)
    g_mix = prm["norm_mix"][2 * e]
    proj_p = norm_matmul(xp, g_mix, w_ext)
    proj_s = norm_matmul(xs, g_mix, w_ext)

    wq = prm["ev_w_uq"][e].reshape(Q_LORA, H_A, QK_NOPE + QK_ROPE)
    wq_pe = wq[:, :, QK_NOPE:]
    wq_ext = jnp.concatenate([wq[:, :, :QK_NOPE], wq_pe, _rot_cols(wq_pe)], axis=-1)
    wq_ext = wq_ext.reshape(Q_LORA, H_A * HEAD_W).astype(BF16)
    wkv = prm["ev_w_ukv"][e].astype(BF16)
    qn, kvn = prm["ev_q_norm"][e], prm["ev_kv_norm"][e]
    q_p, ckv_p, kpe_p, kx_p, vt_p = mla_prep(proj_p, *rope_p, qn, kvn, wq_ext, wkv, True)
    q_s, ckv_s, kpe_s = mla_prep(proj_s, *rope_s, qn, kvn, wq_ext, wkv, False)
    o_a_p = mla_prompt(q_p, kx_p, vt_p, bp, tp)
    o_a_s = mla_sample(q_s, cache_ckv[e], cache_kpe[e], ckv_s, kpe_s, wkv, bs, ts)

    mu = prm["ev_mu"][e]
    mu_parts = [mu[None, 0:C_B], mu[None, C_B:2 * C_B], mu[None, 2 * C_B:3 * C_B],
                mu[None, 3 * C_B:3 * C_B + LANES], mu[None, 3 * C_B + LANES:]]
    zw = jnp.zeros((W_LORA, C_B), F32)
    ww2p = jnp.concatenate([prm["ev_w_w2"][e], zw], axis=0).astype(BF16)
    wa2p = jnp.concatenate([zw, prm["ev_w_a2"][e]], axis=0).astype(BF16)
    prep_w = (mu_parts, prm["ev_w0"][e], ww2p, prm["ev_a0"][e], wa2p, prm["ev_w_g2"][e].astype(BF16),
              prm["ev_k_k"][e], prm["ev_k_a"][e])
    acts_p = rwkv_prep(proj_p, None, *prep_w, tp)
    acts_s = rwkv_prep(proj_s, jnp.repeat(state_shift[e], ts, axis=0), *prep_w, ts)
    r_k = prm["ev_r_k"][e].reshape(-1)
    lnw, lnb = prm["ev_lnx_w"][e], prm["ev_lnx_b"][e]
    m0_p = jnp.zeros((bp, H_B // 2, LANES, LANES), F32)
    o_b_p, m_p = rwkv_chunk(acts_p, r_k, lnw, lnb, m0_p, bp, tp, min(CHUNK, tp), RWKV_PAIRS, bp)
    o_b_s, m_s = rwkv_chunk(acts_s, r_k, lnw, lnb, _pair_state_in(state_rwkv[e]), bs, ts,
                            min(CHUNK, ts), RWKV_PAIRS, 2)

    xs = out_proj(xs, o_a_s, o_b_s, prm["ev_w_out16"], e)
    outs = dict(
        ckv_p=ckv_p.reshape(bp, tp, KV_LORA), ckv_s=ckv_s.reshape(bs, ts, KV_LORA),
        kpe_p=kpe_p.reshape(bp, tp, QK_ROPE), kpe_s=kpe_s.reshape(bs, ts, QK_ROPE),
        rw_p=_pair_state_out(m_p), rw_s=_pair_state_out(m_s),
        sh_p=_last_rows(proj_p, bp, tp), sh_s=_last_rows(proj_s, bs, ts))
    return (o_a_p, o_b_p, prm["ev_w_out16"], e), xs, outs


def _odd_layer(xp, xs, o, dims, caches, prm):
    bp, tp, bs, ts = dims
    cache_k, cache_v, state_gla = caches
    band_past = cache_k.shape[2]
    w_in = prm["od_w_in"][o]
    c_gk = 3 * C_C + 2 * H_D * DK_D + H_D * DV_D
    w_ext = jnp.concatenate([w_in[:, :c_gk], w_in[:, c_gk + GK_LORA:], w_in[:, c_gk:c_gk + GK_LORA],
                             jnp.zeros((D_MODEL, LANES - GK_LORA), F32)], axis=1).astype(BF16)
    g_mix = prm["norm_mix"][2 * o + 1]
    proj_p = norm_matmul(xp, g_mix, w_ext)
    proj_s = norm_matmul(xs, g_mix, w_ext)

    table = prm["od_rel_bias"][o]
    o_c_p = band_prompt(proj_p, _band_bias_prompt(table), bp, tp)
    bias_p, bias_n = _band_bias_sample(table, ts, band_past)
    k_past = cache_k[o].reshape(bs, band_past, C_C)
    v_past = cache_v[o].reshape(bs, band_past, C_C)
    o_c_s = band_sample(proj_s, k_past, v_past, bias_p, bias_n, bs, ts)

    wgk = jnp.concatenate([prm["od_w_gk2"][o], jnp.zeros((LANES - GK_LORA, H_D * DK_D), F32)],
                          axis=0).astype(BF16)
    bgk, gn = prm["od_b_gk"][o], prm["od_gnorm"][o]
    s0_p = jnp.zeros((bp, H_D, DK_D, DV_D), F32)
    o_d_p, s_p = gla(proj_p, wgk, bgk, gn, s0_p, 0, bp, tp, min(CHUNK, tp))
    o_d_s, s_s = gla(proj_s, wgk, bgk, gn, state_gla[o], 0, bs, ts, min(CHUNK, ts))

    xs = out_proj(xs, o_c_s, o_d_s, prm["od_w_out16"], o)

    def tail_prompt(col):
        rows = proj_p.reshape(bp, tp, -1)[:, max(tp - band_past, 0):, col * C_C:(col + 1) * C_C]
        rows = rows.reshape(bp, -1, H_C, HD_C)
        return jnp.pad(rows, ((0, 0), (max(band_past - tp, 0), 0), (0, 0), (0, 0)))

    def tail_sample(cache, col):
        new = proj_s[:, col * C_C:(col + 1) * C_C].reshape(bs, ts, H_C, HD_C)
        return jnp.concatenate([cache[:, ts:], new], axis=1)

    outs = dict(bk_p=tail_prompt(1), bv_p=tail_prompt(2),
                bk_s=tail_sample(cache_k[o], 1), bv_s=tail_sample(cache_v[o], 2),
                gla_p=s_p, gla_s=s_s)
    return (o_c_p, o_d_p, prm["od_w_out16"], o), xs, outs


def _ffn_layer(xp, mix_p, xs, layer, dims, state_conv, prm, final_g):
    bp, tp, bs, ts = dims
    g = prm["norm_ffn"][layer]
    w_up, w_down = prm["ffn_w_up16"], prm["ffn_w_down16"]
    cw, cb = prm["ffn_conv_w"][layer], prm["ffn_conv_b"][layer]
    a1, a2, w_out, e = mix_p
    xp, tg, tv = ffn_prompt(xp, a1, a2, w_out, e, g, w_up, w_down, layer, cw, cb, tp, final_g)
    bps = tp // FFN_TM
    tails = jnp.concatenate([tg.reshape(bp, bps, 8, D_FF)[:, -1, 8 - (CONV_W - 1):],
                             tv.reshape(bp, bps, 8, D_FF)[:, -1, 8 - (CONV_W - 1):]], axis=-1)

    st = state_conv[layer]
    xs, ug, uv = ffn_sample(xs, g, w_up, w_down, layer, cw, cb, st, ts, final_g)
    keep = CONV_W - 1
    new = jnp.concatenate([ug.reshape(bs, ts, D_FF)[:, max(ts - keep, 0):],
                           uv.reshape(bs, ts, D_FF)[:, max(ts - keep, 0):]], axis=-1)
    ext = jnp.concatenate([st[:, ts:], new], axis=1)
    return xp, xs, tails, ext


@jax.jit
def kernel(x_prompt, x_sample, cache_mla_ckv, cache_mla_kpe, state_rwkv, state_rwkv_shift, cache_band_k, cache_band_v, state_gla, state_ffn_conv, norm_mix, norm_ffn, norm_final, ev_w_in, ev_q_norm, ev_w_uq, ev_kv_norm, ev_w_ukv, ev_mu, ev_w0, ev_w_w2, ev_a0, ev_w_a2, ev_w_g2, ev_k_k, ev_k_a, ev_r_k, ev_lnx_w, ev_lnx_b, ev_w_out, od_w_in, od_rel_bias, od_w_gk2, od_b_gk, od_gnorm, od_w_out, ffn_w_up, ffn_conv_w, ffn_conv_b, ffn_w_down):
    prm = dict(norm_mix=norm_mix, norm_ffn=norm_ffn, ev_w_in=ev_w_in, ev_q_norm=ev_q_norm, ev_w_uq=ev_w_uq,
               ev_kv_norm=ev_kv_norm, ev_w_ukv=ev_w_ukv, ev_mu=ev_mu, ev_w0=ev_w0, ev_w_w2=ev_w_w2,
               ev_a0=ev_a0, ev_w_a2=ev_w_a2, ev_w_g2=ev_w_g2, ev_k_k=ev_k_k, ev_k_a=ev_k_a, ev_r_k=ev_r_k,
               ev_lnx_w=ev_lnx_w, ev_lnx_b=ev_lnx_b, ev_w_out=ev_w_out, od_w_in=od_w_in,
               od_rel_bias=od_rel_bias, od_w_gk2=od_w_gk2, od_b_gk=od_b_gk, od_gnorm=od_gnorm,
               od_w_out=od_w_out, ffn_w_up=ffn_w_up, ffn_conv_w=ffn_conv_w, ffn_conv_b=ffn_conv_b,
               ffn_w_down=ffn_w_down)
    bp, tp, _ = x_prompt.shape
    bs, ts, _ = x_sample.shape
    dims = (bp, tp, bs, ts)
    depth = norm_mix.shape[0]
    xp = x_prompt.reshape(bp * tp, D_MODEL)
    xs = x_sample.reshape(bs * ts, D_MODEL)
    rope_p = _rope_tables(jnp.tile(jnp.arange(tp), bp))
    rope_s = _rope_tables(jnp.tile(PAST_LEN + jnp.arange(ts), bs))

    prm.update(ffn_w_up16=ffn_w_up.astype(BF16), ffn_w_down16=ffn_w_down.astype(BF16),
               ev_w_out16=ev_w_out.astype(BF16), od_w_out16=od_w_out.astype(BF16))

    ev, od, ffn_p, ffn_s = [], [], [], []
    for layer in range(depth):
        if layer % 2 == 0:
            mix_p, xs, outs = _even_layer(xp, xs, layer // 2, rope_p, rope_s, dims,
                                          (cache_mla_ckv, cache_mla_kpe, state_rwkv, state_rwkv_shift), prm)
            ev.append(outs)
        else:
            mix_p, xs, outs = _odd_layer(xp, xs, layer // 2, dims, (cache_band_k, cache_band_v, state_gla), prm)
            od.append(outs)
        final_g = norm_final if layer == depth - 1 else None
        xp, xs, tails, ext = _ffn_layer(xp, mix_p, xs, layer, dims, state_ffn_conv, prm, final_g)
        ffn_p.append(tails)
        ffn_s.append(ext)

    yp, ys = xp, xs
    stack = lambda lst, key: jnp.stack([d[key] for d in lst])
    return (yp.reshape(bp, tp, D_MODEL), ys.reshape(bs, ts, D_MODEL),
            stack(ev, "ckv_p"), stack(ev, "ckv_s"), stack(ev, "kpe_p"), stack(ev, "kpe_s"),
            stack(ev, "rw_p"), stack(ev, "rw_s"), stack(ev, "sh_p"), stack(ev, "sh_s"),
            stack(od, "bk_p"), stack(od, "bk_s"), stack(od, "bv_p"), stack(od, "bv_s"),
            stack(od, "gla_p"), stack(od, "gla_s"), jnp.stack(ffn_p), jnp.stack(ffn_s))
```

```python
import functools

import jax
import jax.numpy as jnp
from jax import lax
from jax.experimental import pallas as pl
from jax.experimental.pallas import tpu as pltpu

F32 = jnp.float32
BF16 = jnp.bfloat16

D_MODEL = 2048
CHUNK = 64
EPS = 1e-6
PAST_LEN = 1024

H_A = 8
QK_NOPE = 128
QK_ROPE = 64
V_A = 128
Q_LORA = 512
KV_LORA = 256
ROPE_THETA = 10000.0

H_B = 16
HD_B = 64
C_B = H_B * HD_B
W_LORA = 64
A_LORA = 64
G_LORA = 128
LNX_EPS = 64e-5

H_C = 16
HD_C = 64
C_C = H_C * HD_C
BAND = 8 * CHUNK
REL_CLIP = 128

H_D = 4
DK_D = 128
DV_D = 256
GK_LORA = 16
GATE_NORM = 16.0

D_FF = 5632
CONV_W = 3

LANES = 128
VMEM_LIMIT_BYTES = 60 * 2 ** 20
NEG_BIG = -1e30
ROW_TILE = 256
LOG2E = 1.4426950408889634
MLA_SCALE = (QK_NOPE + QK_ROPE) ** -0.5 * LOG2E
BAND_SCALE = HD_C ** -0.5 * LOG2E


def _cparams(*sem):
    return pltpu.CompilerParams(dimension_semantics=sem, vmem_limit_bytes=VMEM_LIMIT_BYTES)


def _resident(shape, index_map):
    return pl.BlockSpec(shape, index_map, pipeline_mode=pl.Buffered(1))


def _dot(a, b):
    return jnp.dot(a.astype(BF16), b.astype(BF16), preferred_element_type=F32)


def _dot_nt(a, b):
    return lax.dot_general(a.astype(BF16), b.astype(BF16), (((1,), (1,)), ((), ())),
                           preferred_element_type=F32)


def _dot_tn(a, b):
    return lax.dot_general(a.astype(BF16), b.astype(BF16), (((0,), (0,)), ((), ())),
                           preferred_element_type=F32)


def _rms(x, g):
    return x * lax.rsqrt(jnp.mean(x * x, axis=-1, keepdims=True) + EPS) * g


def _sigmoid(x):
    return 1.0 / (1.0 + jnp.exp(-x))


def _softplus(x):
    return jnp.maximum(x, 0.0) + jnp.log(1.0 + jnp.exp(-jnp.abs(x)))


def _cumsum_rows(x):
    n = x.shape[0]
    row = lax.broadcasted_iota(jnp.int32, x.shape, 0)
    s = 1
    while s < n:
        x = x + jnp.where(row >= s, pltpu.roll(x, s, 0), 0.0)
        s *= 2
    return x


def _norm_matmul_kernel(x_ref, g_ref, w_ref, o_ref):
    o_ref[...] = _dot(_rms(x_ref[...], g_ref[...]), w_ref[...])


def norm_matmul(x, g, w):
    m, k = x.shape
    n = w.shape[1]
    tm = ROW_TILE
    return pl.pallas_call(
        _norm_matmul_kernel,
        grid=(m // tm,),
        in_specs=[pl.BlockSpec((tm, k), lambda i: (i, 0)),
                  pl.BlockSpec((1, k), lambda i: (0, 0)),
                  _resident((k, n), lambda i: (0, 0))],
        out_specs=pl.BlockSpec((tm, n), lambda i: (i, 0)),
        out_shape=jax.ShapeDtypeStruct((m, n), F32),
        compiler_params=_cparams("parallel"),
        name="norm_matmul",
    )(x, g.reshape(1, k), w)


def _out_proj_kernel(res_ref, a1_ref, a2_ref, w1_ref, w2_ref, o_ref):
    o_ref[...] = res_ref[...] + _dot(a1_ref[...], w1_ref[...]) + _dot(a2_ref[...], w2_ref[...])


def out_proj(res, a1, a2, w_out, e):
    m, n = res.shape
    ka = a1.shape[1]
    tm = ROW_TILE
    return pl.pallas_call(
        _out_proj_kernel,
        grid=(m // tm,),
        in_specs=[pl.BlockSpec((tm, n), lambda i: (i, 0)),
                  pl.BlockSpec((tm, ka), lambda i: (i, 0)),
                  pl.BlockSpec((tm, ka), lambda i: (i, 0)),
                  _resident((None, ka, n), lambda i: (e, 0, 0)),
                  _resident((None, ka, n), lambda i: (e, 1, 0))],
        out_specs=pl.BlockSpec((tm, n), lambda i: (i, 0)),
        out_shape=jax.ShapeDtypeStruct((m, n), F32),
        compiler_params=_cparams("parallel"),
        name="out_proj",
    )(res, a1, a2, w_out, w_out)


COL_CQ = 3 * C_B
COL_CKV = COL_CQ + Q_LORA
COL_KPE = COL_CKV + KV_LORA
COL_WA = COL_KPE + 2 * QK_ROPE
COL_G = COL_WA + W_LORA + A_LORA
COLS_EVEN_EXT = COL_G + G_LORA
HEAD_W = 2 * LANES


def _mla_prep_kernel(cq_ref, ckv_ref, kpe_ref, cos_ref, sin_ref, qn_ref, kvn_ref, wq_ref, wkv_ref,
                     q_ref, ckv_out_ref, kpe_out_ref, *kv_refs):
    cos = cos_ref[...]
    sin = sin_ref[...]
    z = _dot(_rms(cq_ref[...], qn_ref[...]), wq_ref[...])
    ckvn = _rms(ckv_ref[...], kvn_ref[...])
    ckv_out_ref[...] = ckvn
    kp = kpe_ref[...]
    kr = kp * cos + pltpu.roll(kp, QK_ROPE, 1) * sin
    kpe_out_ref[...] = kr[:, :QK_ROPE]
    for h in range(H_A):
        c0 = h * HEAD_W
        q_ref[h, :, 0:LANES] = (z[:, c0:c0 + LANES] * MLA_SCALE).astype(BF16)
        t2 = z[:, c0 + LANES:c0 + HEAD_W]
        q_ref[h, :, LANES:HEAD_W] = ((t2 * cos + pltpu.roll(t2, QK_ROPE, 1) * sin) * MLA_SCALE).astype(BF16)
    if kv_refs:
        kx_ref, vt_ref = kv_refs
        kr16 = kr.astype(BF16)
        kv = _dot(ckvn, wkv_ref[...])
        for h in range(H_A):
            c0 = h * HEAD_W
            kx_ref[h, :, 0:LANES] = kv[:, c0:c0 + LANES].astype(BF16)
            kx_ref[h, :, LANES:HEAD_W] = kr16
            vt_ref[h, 0] = kv[:, c0 + LANES:c0 + HEAD_W].T.astype(BF16)


def mla_prep(proj, cos, sin, q_norm, kv_norm, wq, wkv, expand):
    m = proj.shape[0]
    tm = ATT_K if expand else ROW_TILE
    kv_specs, kv_shapes = [], []
    if expand:
        kv_specs = [pl.BlockSpec((H_A, tm, HEAD_W), lambda i: (0, i, 0)),
                    pl.BlockSpec((H_A, 1, V_A, tm), lambda i: (0, i, 0, 0))]
        kv_shapes = [jax.ShapeDtypeStruct((H_A, m, HEAD_W), BF16),
                     jax.ShapeDtypeStruct((H_A, m // tm, V_A, tm), BF16)]
    return pl.pallas_call(
        _mla_prep_kernel,
        grid=(m // tm,),
        in_specs=[pl.BlockSpec((tm, Q_LORA), lambda i: (i, COL_CQ // Q_LORA)),
                  pl.BlockSpec((tm, KV_LORA), lambda i: (i, COL_CKV // KV_LORA)),
                  pl.BlockSpec((tm, LANES), lambda i: (i, COL_KPE // LANES)),
                  pl.BlockSpec((tm, LANES), lambda i: (i, 0)),
                  pl.BlockSpec((tm, LANES), lambda i: (i, 0)),
                  pl.BlockSpec((1, Q_LORA), lambda i: (0, 0)),
                  pl.BlockSpec((1, KV_LORA), lambda i: (0, 0)),
                  _resident((Q_LORA, H_A * HEAD_W), lambda i: (0, 0)),
                  _resident((KV_LORA, H_A * HEAD_W), lambda i: (0, 0))],
        out_specs=[pl.BlockSpec((H_A, tm, HEAD_W), lambda i: (0, i, 0)),
                   pl.BlockSpec((tm, KV_LORA), lambda i: (i, 0)),
                   pl.BlockSpec((tm, QK_ROPE), lambda i: (i, 0))] + kv_specs,
        out_shape=[jax.ShapeDtypeStruct((H_A, m, HEAD_W), BF16),
                   jax.ShapeDtypeStruct((m, KV_LORA), F32),
                   jax.ShapeDtypeStruct((m, QK_ROPE), F32)] + kv_shapes,
        compiler_params=_cparams("parallel"),
        name="mla_prep",
    )(proj, proj, proj, cos, sin, q_norm.reshape(1, -1), kv_norm.reshape(1, -1), wq, wkv)


ATT_Q = 512
ATT_K = 512
MLA_HEADS_PER_STEP = 4


def _mla_prompt_kernel(q_ref, k_ref, vt_ref, o_ref):
    t, tk = ATT_Q, ATT_K
    qi = pl.program_id(2)
    n_full = (qi * t) // tk

    def block(j, carry, masked):
        start = pl.multiple_of(j * tk, tk)
        heads = range(MLA_HEADS_PER_STEP)
        s = [lax.dot_general(k_ref[g, pl.ds(start, tk), :], q_ref[g], (((1,), (1,)), ((), ())),
                             preferred_element_type=F32) for g in heads]
        if masked:
            kc = (lax.broadcasted_iota(jnp.int32, (tk, t), 0) + start) // CHUNK
            qc = (lax.broadcasted_iota(jnp.int32, (tk, t), 1) + qi * t) // CHUNK
            s = [jnp.where(kc <= qc, x, NEG_BIG) for x in s]
        m_new = [jnp.maximum(carry[g][0], jnp.max(s[g], axis=0, keepdims=True)) for g in heads]
        p = [jnp.exp2(s[g] - m_new[g]) for g in heads]
        pv = [jnp.dot(vt_ref[g, j], p[g].astype(BF16), preferred_element_type=F32) for g in heads]
        out = []
        for g in heads:
            m, l, acc = carry[g]
            alpha = jnp.exp2(m - m_new[g])
            out.append((m_new[g], alpha * l + jnp.sum(p[g], axis=0, keepdims=True), alpha * acc + pv[g]))
        return tuple(out)

    init = tuple((jnp.full((1, t), NEG_BIG, F32), jnp.zeros((1, t), F32), jnp.zeros((V_A, t), F32))
                 for _ in range(MLA_HEADS_PER_STEP))
    carry = lax.fori_loop(0, n_full, lambda j, c: block(j, c, False), init)
    carry = block(n_full, carry, True)
    for g in range(MLA_HEADS_PER_STEP):
        m, l, acc = carry[g]
        o_ref[:, g * V_A:(g + 1) * V_A] = (acc / l).T.astype(BF16)


def mla_prompt(q, kx, vt, batch, seq):
    t = ATT_Q
    g = MLA_HEADS_PER_STEP
    assert seq % ATT_K == 0 and ATT_K % t == 0 and H_A % g == 0
    nq = seq // t
    nk = seq // ATT_K
    return pl.pallas_call(
        _mla_prompt_kernel,
        grid=(batch, H_A // g, nq),
        in_specs=[pl.BlockSpec((g, t, HEAD_W), lambda b, h, i: (h, b * nq + i, 0)),
                  pl.BlockSpec((g, seq, HEAD_W), lambda b, h, i: (h, b, 0)),
                  pl.BlockSpec((g, nk, V_A, ATT_K), lambda b, h, i: (h, b, 0, 0))],
        out_specs=pl.BlockSpec((t, g * V_A), lambda b, h, i: (b * nq + i, h)),
        out_shape=jax.ShapeDtypeStruct((batch * seq, H_A * V_A), BF16),
        compiler_params=_cparams("parallel", "parallel", "arbitrary"),
        name="mla_prompt",
    )(q, kx, vt)


def _mla_sample_kernel(q_ref, ckvp_ref, kpep_ref, ckvn_ref, kpen_ref, wkv_ref, o_ref):
    seq = ckvn_ref.shape[0]
    ckvp = ckvp_ref[0].astype(BF16)
    ckvn = ckvn_ref[...].astype(BF16)
    kpep = kpep_ref[0].astype(BF16)
    kpen = kpen_ref[...].astype(BF16)
    heads = range(H_A)
    q_lat = jnp.concatenate([_dot_nt(q_ref[h, :, 0:QK_NOPE], wkv_ref[:, h * HEAD_W:h * HEAD_W + QK_NOPE])
                             for h in heads], axis=0)
    q_pe = jnp.concatenate([q_ref[h, :, LANES:LANES + QK_ROPE] for h in heads], axis=0)
    s_p = _dot_nt(q_lat, ckvp) + _dot_nt(q_pe, kpep)
    s_n = _dot_nt(q_lat, ckvn) + _dot_nt(q_pe, kpen)
    m = jnp.maximum(jnp.max(s_p, axis=-1, keepdims=True), jnp.max(s_n, axis=-1, keepdims=True))
    p_p = jnp.exp2(s_p - m)
    p_n = jnp.exp2(s_n - m)
    l = jnp.sum(p_p, axis=-1, keepdims=True) + jnp.sum(p_n, axis=-1, keepdims=True)
    o_lat = (_dot(p_p, ckvp) + _dot(p_n, ckvn)) / l
    for h in heads:
        o_ref[:, h * V_A:(h + 1) * V_A] = _dot(
            o_lat[h * seq:(h + 1) * seq], wkv_ref[:, h * HEAD_W + QK_NOPE:(h + 1) * HEAD_W]).astype(BF16)


def mla_sample(q, ckv_past, kpe_past, e, ckv_new, kpe_new, wkv, batch, seq):
    past = ckv_past.shape[2]
    return pl.pallas_call(
        _mla_sample_kernel,
        grid=(batch,),
        in_specs=[pl.BlockSpec((H_A, seq, HEAD_W), lambda b: (0, b, 0)),
                  pl.BlockSpec((None, 1, past, KV_LORA), lambda b: (e, b, 0, 0)),
                  pl.BlockSpec((None, 1, past, QK_ROPE), lambda b: (e, b, 0, 0)),
                  pl.BlockSpec((seq, KV_LORA), lambda b: (b, 0)),
                  pl.BlockSpec((seq, QK_ROPE), lambda b: (b, 0)),
                  _resident((KV_LORA, H_A * HEAD_W), lambda b: (0, 0))],
        out_specs=pl.BlockSpec((seq, H_A * V_A), lambda b: (b, 0)),
        out_shape=jax.ShapeDtypeStruct((batch * seq, H_A * V_A), BF16),
        compiler_params=_cparams("parallel"),
        name="mla_sample",
    )(q, ckv_past, kpe_past, ckv_new, kpe_new, wkv)


def _rwkv_prep_kernel(r_ref, k_ref, v_ref, wa_ref, g_ref, rp_ref, kp_ref, vp_ref, wap_ref, gp_ref,
                      mu_r_ref, mu_k_ref, mu_v_ref, mu_wa_ref, mu_g_ref,
                      w0_ref, ww2_ref, a0_ref, wa2_ref, wg2_ref, kk_ref, ka_ref,
                      r_out, lw_out, k_out, v_out, kk_out, a_out, g_out, *, tm, seq, first_rows):
    row = lax.broadcasted_iota(jnp.int32, (tm, 1), 0)
    if not first_rows:
        keep = ((pl.program_id(0) * tm) % seq != 0).astype(F32)

    def mix(p_ref, pp_ref, mu_ref):
        p = p_ref[...]
        rolled = pltpu.roll(p, 1, 0)
        if first_rows:
            prev = jnp.where(row % seq == 0, pp_ref[...], rolled)
        else:
            prev = jnp.where(row == 0, pp_ref[7:8, :] * keep, rolled)
        return p + (prev - p) * mu_ref[...]

    r_out[...] = mix(r_ref, rp_ref, mu_r_ref)
    v_out[...] = mix(v_ref, vp_ref, mu_v_ref)
    k = mix(k_ref, kp_ref, mu_k_ref)
    xwa = mix(wa_ref, wap_ref, mu_wa_ref)
    xg = mix(g_ref, gp_ref, mu_g_ref)
    w_log = -_softplus(-(w0_ref[...] + _dot(jnp.tanh(xwa), ww2_ref[...]))) - 0.5
    lw_out[...] = -jnp.exp(w_log)
    a = _sigmoid(a0_ref[...] + _dot(xwa, wa2_ref[...]))
    a_out[...] = a
    g_out[...] = _dot(_sigmoid(xg), wg2_ref[...])
    kk_out[...] = k * kk_ref[...]
    k_out[...] = k * (1.0 + (a - 1.0) * ka_ref[...])


def rwkv_prep(proj, first, mu_parts, w0, ww2p, a0, wa2p, wg2, k_k, k_a, seq):
    m = proj.shape[0]
    tm = ROW_TILE
    wide = lambda c: pl.BlockSpec((tm, C_B), lambda i: (i, c))
    narrow = lambda c: pl.BlockSpec((tm, LANES), lambda i: (i, c))
    vec = lambda n: pl.BlockSpec((1, n), lambda i: (0, 0))
    out = jax.ShapeDtypeStruct((m, C_B), F32)
    if first is None:
        assert seq % tm == 0
        before = lambda i: jnp.maximum(i * (tm // 8) - 1, 0)
        prev_specs = [pl.BlockSpec((8, C_B), lambda i, c=c: (before(i), c)) for c in range(3)]
        prev_specs += [pl.BlockSpec((8, LANES), lambda i, c=c: (before(i), c))
                       for c in (COL_WA // LANES, COL_G // LANES)]
        prev = proj
    else:
        assert tm % seq == 0
        prev_specs = [wide(0), wide(1), wide(2), narrow(3 * C_B // LANES), narrow(3 * C_B // LANES + 1)]
        prev = first
    return pl.pallas_call(
        functools.partial(_rwkv_prep_kernel, tm=tm, seq=seq, first_rows=first is not None),
        grid=(m // tm,),
        in_specs=[wide(0), wide(1), wide(2), narrow(COL_WA // LANES), narrow(COL_G // LANES),
                  *prev_specs,
                  vec(C_B), vec(C_B), vec(C_B), vec(LANES), vec(LANES),
                  vec(C_B), _resident((LANES, C_B), lambda i: (0, 0)),
                  vec(C_B), _resident((LANES, C_B), lambda i: (0, 0)),
                  _resident((G_LORA, C_B), lambda i: (0, 0)), vec(C_B), vec(C_B)],
        out_specs=[pl.BlockSpec((tm, C_B), lambda i: (i, 0))] * 7,
        out_shape=[out] * 7,
        compiler_params=_cparams("parallel"),
        name="rwkv_prep",
    )(proj, proj, proj, proj, proj, prev, prev, prev, prev, prev,
      *mu_parts, w0.reshape(1, -1), ww2p, a0.reshape(1, -1), wa2p, wg2,
      k_k.reshape(1, -1), k_a.reshape(1, -1))


RWKV_PAIRS = 8


def _rwkv_chunk_kernel(r_ref, lw_ref, k_ref, v_ref, kk_ref, a_ref, g_ref, rk_ref, lnw_ref, lnb_ref,
                       m0_ref, o_ref, mout_ref, m_scr, *, chunk, pairs, seqs):
    c = pl.program_id(2)
    nc = pl.num_programs(2)
    L = chunk
    L2 = 2 * L

    @pl.when(c == 0)
    def _():
        m_scr[...] = m0_ref[...]

    lane = lax.broadcasted_iota(jnp.int32, (L, LANES), 1)
    low = lane < HD_B
    row_p = lax.broadcasted_iota(jnp.int32, (L, L2), 0)
    col_p = lax.broadcasted_iota(jnp.int32, (L, L2), 1) % L
    strict = row_p > col_p
    incl = row_p >= col_p
    eye_p = (row_p == col_p).astype(F32)
    same_blk = ((lax.broadcasted_iota(jnp.int32, (L2, L2), 0) < L)
                == (lax.broadcasted_iota(jnp.int32, (L2, L2), 1) < L))
    r128 = lax.broadcasted_iota(jnp.int32, (LANES, LANES), 0)
    c128 = lax.broadcasted_iota(jnp.int32, (LANES, LANES), 1)
    same_head = (r128 < HD_B) == (c128 < HD_B)
    diag = r128 == c128

    def seg_sum(x):
        s0 = jnp.sum(jnp.where(low, x, 0.0), axis=-1, keepdims=True)
        s1 = jnp.sum(jnp.where(low, 0.0, x), axis=-1, keepdims=True)
        return jnp.where(low, s0, s1)

    def split_heads(x):
        return jnp.concatenate([jnp.where(low, x, 0.0), jnp.where(low, 0.0, x)], axis=0)

    def block_diag(x):
        return jnp.where(same_blk, jnp.concatenate([x, x], axis=0), 0.0)

    P = range(seqs * pairs)
    sqs = [p // pairs for p in P]
    sls = [slice((p % pairs) * LANES, (p % pairs + 1) * LANES) for p in P]

    def prep(b, sl):
        r = r_ref[b, :, sl]
        lw = lw_ref[b, :, sl]
        k = k_ref[b, :, sl]
        kkr = kk_ref[b, :, sl]
        kk = kkr / jnp.maximum(jnp.sqrt(seg_sum(kkr * kkr)), 1e-12)
        bvec = kk * a_ref[b, :, sl]
        ci = _cumsum_rows(lw)
        c_end = ci[L - 1:L, :]
        e_neg = jnp.exp(-ci)
        e_end = jnp.exp(c_end - ci)
        return dict(r=r, k=k, v=v_ref[b, :, sl], c_end=c_end,
                    at=-kk * jnp.exp(ci - lw), bt=bvec * e_neg, kt=k * e_neg, rt=r * jnp.exp(ci),
                    bh=bvec * e_end, kh=k * e_end)

    d = [prep(sqs[p], sls[p]) for p in P]
    sc = [_dot_nt(jnp.concatenate([x["at"], x["rt"]], axis=0),
                  jnp.concatenate([split_heads(x["bt"]), split_heads(x["kt"])], axis=0)) for x in d]
    nab = [jnp.where(strict, s[0:L, 0:L2], 0.0) for s in sc]
    nak = [jnp.where(strict, s[0:L, L2:2 * L2], 0.0) for s in sc]
    lrb = [jnp.where(incl, s[L:L2, 0:L2], 0.0) for s in sc]
    lrk = [jnp.where(incl, s[L:L2, L2:2 * L2], 0.0) for s in sc]
    v_split = [split_heads(x["v"]) for x in d]

    t_inv = [eye_p + n for n in nab]
    pw = [_dot(n, block_diag(n)) for n in nab]
    nakv = [_dot(nak[p], v_split[p]) for p in P]
    m = 2
    while m < L:
        if 2 * m < L:
            res = [_dot(pw[p], jnp.concatenate([block_diag(pw[p]), block_diag(t_inv[p])], axis=1)) for p in P]
            pw = [x[:, 0:L2] for x in res]
            t_inv = [t_inv[p] + res[p][:, L2:2 * L2] for p in P]
        else:
            t_inv = [t_inv[p] + _dot(pw[p], block_diag(t_inv[p])) for p in P]
        m *= 2
    yk = [_dot(lrk[p], v_split[p]) for p in P]
    hk = [_dot_tn(d[p]["kh"], d[p]["v"]) for p in P]

    wu = [_dot(t_inv[p], jnp.concatenate([split_heads(d[p]["at"]), split_heads(nakv[p])], axis=1)) for p in P]
    qy = [_dot(lrb[p], jnp.concatenate([split_heads(wu[p][:, 0:LANES]), split_heads(wu[p][:, LANES:])], axis=1))
          for p in P]
    gh = [_dot_tn(d[p]["bh"], wu[p]) for p in P]
    q = [d[p]["rt"] + qy[p][:, 0:LANES] for p in P]
    gmat = [jnp.where(diag, jnp.broadcast_to(jnp.exp(d[p]["c_end"]), (LANES, LANES)), 0.0)
            + jnp.where(same_head, gh[p][:, 0:LANES], 0.0) for p in P]
    ym = [_dot(jnp.concatenate([q[p], gmat[p]], axis=0), m_scr[sqs[p], p % pairs]) for p in P]
    for p in P:
        m_scr[sqs[p], p % pairs] = (ym[p][L:L + LANES]
                                    + jnp.where(same_head, gh[p][:, LANES:] + hk[p], 0.0))
    for p in P:
        sl = sls[p]
        y = ym[p][0:L] + qy[p][:, LANES:] + yk[p]
        mean = seg_sum(y) * (1.0 / HD_B)
        dev = y - mean
        var = seg_sum(dev * dev) * (1.0 / HD_B)
        yn = dev * lax.rsqrt(var + LNX_EPS) * lnw_ref[:, sl] + lnb_ref[:, sl]
        bonus = seg_sum(d[p]["r"] * d[p]["k"] * rk_ref[:, sl]) * d[p]["v"]
        o_ref[sqs[p], :, sl] = ((yn + bonus) * g_ref[sqs[p], :, sl]).astype(BF16)

    @pl.when(c == nc - 1)
    def _():
        mout_ref[...] = m_scr[...]


def rwkv_chunk(acts, r_k, lnx_w, lnx_b, m0, batch, seq, chunk, pairs, seqs):
    nc = seq // chunk
    npair = C_B // LANES
    w = pairs * LANES
    act = pl.BlockSpec((seqs, chunk, w), lambda b, pg, c: (b, c, pg))
    prm = pl.BlockSpec((1, w), lambda b, pg, c: (0, pg))
    st = pl.BlockSpec((seqs, pairs, LANES, LANES), lambda b, pg, c: (b, pg, 0, 0))
    o, m = pl.pallas_call(
        functools.partial(_rwkv_chunk_kernel, chunk=chunk, pairs=pairs, seqs=seqs),
        grid=(batch // seqs, npair // pairs, nc),
        in_specs=[act] * 7 + [prm] * 3 + [st],
        out_specs=[act, st],
        out_shape=[jax.ShapeDtypeStruct((batch, seq, C_B), BF16),
                   jax.ShapeDtypeStruct((batch, npair, LANES, LANES), F32)],
        scratch_shapes=[pltpu.VMEM((seqs, pairs, LANES, LANES), F32)],
        compiler_params=_cparams("parallel", "parallel", "arbitrary"),
        name="rwkv_chunk",
    )(*(t.reshape(batch, seq, C_B) for t in acts),
      r_k.reshape(1, -1), lnx_w.reshape(1, -1), lnx_b.reshape(1, -1), m0)
    return o.reshape(batch * seq, C_B), m


BAND_Q = 256
BAND_SEGS = BAND // BAND_Q + 1
BAND_PAIRS = 4


def _band_prompt_kernel(q_ref, k_ref, v_ref, bias_ref, o_ref):
    qb = pl.program_id(2)
    t = BAND_Q
    lane = lax.broadcasted_iota(jnp.int32, (t, LANES), 1)
    low = lane < HD_C
    row_low = lax.broadcasted_iota(jnp.int32, (LANES, t), 0) < HD_C
    pairs = range(BAND_PAIRS)
    heads = range(2 * BAND_PAIRS)
    segs = range(BAND_SEGS)
    sls = [slice(pr * LANES, (pr + 1) * LANES) for pr in pairs]
    starts, valid = [], []
    for seg in segs:
        kb = qb - (BAND_SEGS - 1) + seg
        starts.append(pl.multiple_of(jnp.maximum(kb, 0) * t, t))
        valid.append(kb >= 0)
    ks = [[k_ref[pl.ds(starts[seg], t), sl].astype(BF16) for seg in segs] for sl in sls]
    vt = [[v_ref[pl.ds(starts[seg], t), sl].T.astype(BF16) for seg in segs] for sl in sls]
    q = [q_ref[:, sl] * BAND_SCALE for sl in sls]
    qh = [jnp.where(low if h % 2 == 0 else jnp.logical_not(low), q[h // 2], 0.0).astype(BF16) for h in heads]
    s = [[jnp.where(valid[seg],
                    lax.dot_general(ks[h // 2][seg], qh[h], (((1,), (1,)), ((), ())),
                                    preferred_element_type=F32)
                    + bias_ref[h, seg * t:(seg + 1) * t, :], NEG_BIG)
          for seg in segs] for h in heads]
    m = [functools.reduce(jnp.maximum, [jnp.max(s[h][seg], axis=0, keepdims=True) for seg in segs])
         for h in heads]
    p = [[jnp.exp2(s[h][seg] - m[h]) for seg in segs] for h in heads]
    l = [sum(jnp.sum(p[h][seg], axis=0, keepdims=True) for seg in segs) for h in heads]
    acc = [sum(jnp.dot(vt[h // 2][seg], p[h][seg].astype(BF16), preferred_element_type=F32) for seg in segs)
           for h in heads]
    for pr in pairs:
        o_ref[:, sls[pr]] = jnp.where(row_low, acc[2 * pr] / l[2 * pr],
                                      acc[2 * pr + 1] / l[2 * pr + 1]).T.astype(BF16)


def band_prompt(proj, bias, batch, seq):
    t = BAND_Q
    nq = seq // t
    w = BAND_PAIRS * LANES
    ngrp = C_C // w
    return pl.pallas_call(
        _band_prompt_kernel,
        grid=(ngrp, batch, nq),
        in_specs=[pl.BlockSpec((t, w), lambda hp, b, i: (b * nq + i, hp)),
                  pl.BlockSpec((seq, w), lambda hp, b, i: (b, ngrp + hp)),
                  pl.BlockSpec((seq, w), lambda hp, b, i: (b, 2 * ngrp + hp)),
                  pl.BlockSpec((2 * BAND_PAIRS, BAND_SEGS * t, t), lambda hp, b, i: (hp, 0, 0))],
        out_specs=pl.BlockSpec((t, w), lambda hp, b, i: (b * nq + i, hp)),
        out_shape=jax.ShapeDtypeStruct((batch * seq, C_C), BF16),
        compiler_params=_cparams("parallel", "parallel", "parallel"),
        name="band_prompt",
    )(proj, proj, proj, bias)


def _band_sample_kernel(q_ref, kn_ref, vn_ref, kp_ref, vp_ref, biasp_ref, biasn_ref, o_ref):
    t = q_ref.shape[0]
    lane = lax.broadcasted_iota(jnp.int32, (t, LANES), 1)
    low = lane < HD_C
    pairs = range(C_C // LANES)
    heads = range(H_C)
    sls = [slice(hp * LANES, (hp + 1) * LANES) for hp in pairs]
    kp = [kp_ref[0, :, sl].astype(BF16) for sl in sls]
    vp = [vp_ref[0, :, sl].astype(BF16) for sl in sls]
    kn = [kn_ref[:, sl].astype(BF16) for sl in sls]
    vn = [vn_ref[:, sl].astype(BF16) for sl in sls]
    q = [q_ref[:, sl] * BAND_SCALE for sl in sls]
    qh = [jnp.where(low if h % 2 == 0 else jnp.logical_not(low), q[h // 2], 0.0).astype(BF16) for h in heads]
    s_p = [_dot_nt(qh[h], kp[h // 2]) + biasp_ref[h] for h in heads]
    s_n = [_dot_nt(qh[h], kn[h // 2]) + biasn_ref[h] for h in heads]
    m = [jnp.maximum(jnp.max(s_p[h], axis=-1, keepdims=True), jnp.max(s_n[h], axis=-1, keepdims=True))
         for h in heads]
    p_p = [jnp.exp2(s_p[h] - m[h]) for h in heads]
    p_n = [jnp.exp2(s_n[h] - m[h]) for h in heads]
    l = [jnp.sum(p_p[h], axis=-1, keepdims=True) + jnp.sum(p_n[h], axis=-1, keepdims=True) for h in heads]
    o = [(_dot(p_p[h], vp[h // 2]) + _dot(p_n[h], vn[h // 2])) / l[h] for h in heads]
    for hp in pairs:
        o_ref[:, sls[hp]] = jnp.where(low, o[2 * hp], o[2 * hp + 1]).astype(BF16)


def band_sample(proj, k_past, v_past, o, bias_p, bias_n, batch, seq):
    past = k_past.shape[2]
    new = lambda col: pl.BlockSpec((seq, C_C), lambda b: (b, col))
    old = pl.BlockSpec((None, 1, past, C_C), lambda b: (o, b, 0, 0))
    return pl.pallas_call(
        _band_sample_kernel,
        grid=(batch,),
        in_specs=[new(0), new(1), new(2), old, old,
                  _resident((H_C, seq, past), lambda b: (0, 0, 0)),
                  _resident((H_C, seq, seq), lambda b: (0, 0, 0))],
        out_specs=pl.BlockSpec((seq, C_C), lambda b: (b, 0)),
        out_shape=jax.ShapeDtypeStruct((batch * seq, C_C), BF16),
        compiler_params=_cparams("parallel"),
        name="band_sample",
    )(proj, proj, proj, k_past, v_past, bias_p, bias_n)


def _gla_kernel(q_ref, k_ref, v_ref, go_ref, gk_ref, wgk_ref, bgk_ref, gn_ref, s0_ref,
                o_ref, sout_ref, s_scr, *, chunk, seqs):
    c = pl.program_id(1)
    nc = pl.num_programs(1)
    L = chunk

    @pl.when(c == 0)
    def _():
        s_scr[...] = s0_ref[...]

    causal = (lax.broadcasted_iota(jnp.int32, (L, L), 0) >= lax.broadcasted_iota(jnp.int32, (L, L), 1))
    diag = (lax.broadcasted_iota(jnp.int32, (DK_D, DK_D), 0)
            == lax.broadcasted_iota(jnp.int32, (DK_D, DK_D), 1))
    S = range(seqs)
    P = [(sq, h) for sq in S for h in range(H_D)]
    ksl = [slice(h * DK_D, (h + 1) * DK_D) for h in range(H_D)]
    vsl = [slice(h * DV_D, (h + 1) * DV_D) for h in range(H_D)]
    b = [_cumsum_rows(-_softplus(-(_dot(gk_ref[sq], wgk_ref[...]) + bgk_ref[...])) / GATE_NORM) for sq in S]
    eb = [jnp.exp(x) for x in b]
    enb = [jnp.exp(-x) for x in b]
    ed = [jnp.exp(x[L - 1:L, :] - x) for x in b]
    e_end = [jnp.exp(x[L - 1:L, :]) for x in b]
    qe = [q_ref[sq, :, ksl[h]] * DK_D ** -0.5 * eb[sq][:, ksl[h]] for sq, h in P]
    vh = [v_ref[sq, :, vsl[h]].astype(BF16) for sq, h in P]
    att = [jnp.where(causal, _dot_nt(qe[i], k_ref[sq, :, ksl[h]] * enb[sq][:, ksl[h]]), 0.0)
           for i, (sq, h) in enumerate(P)]
    s_prev = [s_scr[sq, h] for sq, h in P]
    o = [_dot(att[i], vh[i]) + _dot(qe[i], s_prev[i]) for i in range(len(P))]
    kv = [_dot_tn(k_ref[sq, :, ksl[h]] * ed[sq][:, ksl[h]], vh[i]) for i, (sq, h) in enumerate(P)]
    for i, (sq, h) in enumerate(P):
        e_col = jnp.sum(jnp.where(diag, jnp.broadcast_to(e_end[sq][:, ksl[h]], (DK_D, DK_D)), 0.0),
                        axis=1, keepdims=True)
        s_scr[sq, h] = s_prev[i] * e_col + kv[i]
    for i, (sq, h) in enumerate(P):
        on = o[i] * lax.rsqrt(jnp.mean(o[i] * o[i], axis=-1, keepdims=True) + EPS) * gn_ref[...]
        gate = go_ref[sq, :, vsl[h]]
        o_ref[sq, :, vsl[h]] = (on * (gate * _sigmoid(gate))).astype(BF16)

    @pl.when(c == nc - 1)
    def _():
        sout_ref[...] = s_scr[...]


COL_QD = 3 * C_C
COL_KD = COL_QD + H_D * DK_D
COL_VD = COL_KD + H_D * DK_D
COL_GO = COL_VD + H_D * DV_D
COL_GK = COL_GO + H_D * DV_D
COLS_ODD_EXT = COL_GK + LANES


def gla(proj, wgk, bgk, gnorm, s0, batch, seq, chunk, seqs):
    nc = seq // chunk
    kw = H_D * DK_D
    vw = H_D * DV_D
    proj3 = proj.reshape(batch, seq, proj.shape[1])
    act = lambda w, col: pl.BlockSpec((seqs, chunk, w), lambda b, c: (b, c, col))
    st = pl.BlockSpec((seqs, H_D, DK_D, DV_D), lambda b, c: (b, 0, 0, 0))
    o, s = pl.pallas_call(
        functools.partial(_gla_kernel, chunk=chunk, seqs=seqs),
        grid=(batch // seqs, nc),
        in_specs=[act(kw, COL_QD // kw), act(kw, COL_KD // kw), act(vw, COL_VD // vw), act(vw, COL_GO // vw),
                  act(LANES, COL_GK // LANES),
                  _resident((LANES, kw), lambda b, c: (0, 0)),
                  pl.BlockSpec((1, kw), lambda b, c: (0, 0)),
                  pl.BlockSpec((1, DV_D), lambda b, c: (0, 0)),
                  st],
        out_specs=[act(vw, 0), st],
        out_shape=[jax.ShapeDtypeStruct((batch, seq, vw), BF16),
                   jax.ShapeDtypeStruct((batch, H_D, DK_D, DV_D), F32)],
        scratch_shapes=[pltpu.VMEM((seqs, H_D, DK_D, DV_D), F32)],
        compiler_params=_cparams("parallel", "arbitrary"),
        name="gla",
    )(proj3, proj3, proj3, proj3, proj3, wgk, bgk.reshape(1, -1), gnorm.reshape(1, -1), s0)
    return o.reshape(batch * seq, vw), s


FFN_TM = 512
FFN_TF = 512
HALO = 16


def _conv3(u_scr, cw_ref, cb_ref, cols, tm, prev1=None, prev2=None):
    u0 = u_scr[HALO:HALO + tm, cols]
    u1 = u_scr[HALO - 1:HALO - 1 + tm, cols]
    u2 = u_scr[HALO - 2:HALO - 2 + tm, cols]
    if prev1 is not None:
        u1 = prev1(u1)
        u2 = prev2(u2)
    return cb_ref[:, cols] + cw_ref[2:3, cols] * u0 + cw_ref[1:2, cols] * u1 + cw_ref[0:1, cols] * u2


def _ffn_prompt_kernel(x_ref, a1_ref, a2_ref, xh_ref, a1h_ref, a2h_ref, w1_ref, w2_ref, g_ref, gf_ref,
                       wg_ref, wv_ref, cwg_ref, cwv_ref, cbg_ref, cbv_ref, wd_ref,
                       o_ref, tg_ref, tv_ref, h_scr, *u_scrs, tm, blocks_per_seq, final_norm_out):
    i = pl.program_id(0)
    j = pl.program_id(1)
    nj = pl.num_programs(1)

    @pl.when(j == 0)
    def _():
        w1 = w1_ref[...]
        w2 = w2_ref[...]
        xm = x_ref[...] + _dot(a1_ref[...], w1) + _dot(a2_ref[...], w2)
        o_ref[...] = xm
        h_scr[HALO:HALO + tm, :] = _rms(xm, g_ref[...]).astype(BF16)
        keep = (i % blocks_per_seq != 0).astype(F32)
        xmh = xh_ref[...] + _dot(a1h_ref[...], w1) + _dot(a2h_ref[...], w2)
        h_scr[0:HALO, :] = (_rms(xmh, g_ref[...]) * keep).astype(BF16)

    h = h_scr[...]
    ug_scr, uv_scr = u_scrs
    ug_scr[...] = jnp.dot(h, wg_ref[...], preferred_element_type=F32)
    uv_scr[...] = jnp.dot(h, wv_ref[...], preferred_element_type=F32)
    acts = []
    for c in range(0, ug_scr.shape[1], LANES):
        lanes = slice(c, c + LANES)
        cg = _conv3(ug_scr, cwg_ref, cbg_ref, lanes, tm)
        cv = _conv3(uv_scr, cwv_ref, cbv_ref, lanes, tm)
        acts.append((cg * _sigmoid(cg) * cv).astype(BF16))
    o_ref[...] += jnp.dot(jnp.concatenate(acts, axis=1), wd_ref[...], preferred_element_type=F32)
    tg_ref[0] = ug_scr[HALO + tm - 8:HALO + tm, :]
    tv_ref[0] = uv_scr[HALO + tm - 8:HALO + tm, :]

    if final_norm_out:
        @pl.when(j == nj - 1)
        def _():
            o_ref[...] = _rms(o_ref[...], gf_ref[...])


def ffn_prompt(x, a1, a2, w_out, e, g, w_up, w_down, layer, conv_w, conv_b, seq, final_g=None):
    m, k = x.shape
    ka = a1.shape[1]
    tm, tf = FFN_TM, FFN_TF
    nj = D_FF // tf
    nblk = m // tm
    bps = seq // tm
    halo_blocks = tm // HALO
    cb = conv_b.reshape(1, -1)
    gf = (g if final_g is None else final_g).reshape(1, k)
    before = lambda i, j: (jnp.maximum(i * halo_blocks - 1, 0), 0)
    return pl.pallas_call(
        functools.partial(_ffn_prompt_kernel, tm=tm, blocks_per_seq=bps, final_norm_out=final_g is not None),
        grid=(nblk, nj),
        in_specs=[pl.BlockSpec((tm, k), lambda i, j: (i, 0)),
                  pl.BlockSpec((tm, ka), lambda i, j: (i, 0)),
                  pl.BlockSpec((tm, ka), lambda i, j: (i, 0)),
                  pl.BlockSpec((HALO, k), before),
                  pl.BlockSpec((HALO, ka), before),
                  pl.BlockSpec((HALO, ka), before),
                  _resident((None, ka, k), lambda i, j: (e, 0, 0)),
                  _resident((None, ka, k), lambda i, j: (e, 1, 0)),
                  pl.BlockSpec((1, k), lambda i, j: (0, 0)),
                  pl.BlockSpec((1, k), lambda i, j: (0, 0)),
                  pl.BlockSpec((None, k, tf), lambda i, j: (layer, 0, j)),
                  pl.BlockSpec((None, k, tf), lambda i, j: (layer, 0, nj + j)),
                  pl.BlockSpec((CONV_W, tf), lambda i, j: (0, j)),
                  pl.BlockSpec((CONV_W, tf), lambda i, j: (0, nj + j)),
                  pl.BlockSpec((1, tf), lambda i, j: (0, j)),
                  pl.BlockSpec((1, tf), lambda i, j: (0, nj + j)),
                  pl.BlockSpec((None, tf, k), lambda i, j: (layer, j, 0))],
        out_specs=[pl.BlockSpec((tm, k), lambda i, j: (i, 0)),
                   pl.BlockSpec((1, 8, tf), lambda i, j: (i, 0, j)),
                   pl.BlockSpec((1, 8, tf), lambda i, j: (i, 0, j))],
        out_shape=[jax.ShapeDtypeStruct((m, k), F32),
                   jax.ShapeDtypeStruct((nblk, 8, D_FF), F32),
                   jax.ShapeDtypeStruct((nblk, 8, D_FF), F32)],
        scratch_shapes=[pltpu.VMEM((tm + HALO, k), BF16)] + [pltpu.VMEM((tm + HALO, tf), F32)] * 2,
        compiler_params=_cparams("parallel", "arbitrary"),
        name="ffn_prompt",
    )(x, a1, a2, x, a1, a2, w_out, w_out, g.reshape(1, k), gf, w_up, w_up, conv_w, conv_w, cb, cb, w_down)


def _ffn_sample_kernel(x_ref, g_ref, gf_ref, wg_ref, wv_ref, cwg_ref, cwv_ref, cbg_ref, cbv_ref, wd_ref,
                       sg_ref, sv_ref,
                       o_ref, ug_ref, uv_ref, h_scr, acc_scr, ug_scr, uv_scr, *, tm, seq, final_norm_out):
    j = pl.program_id(0)
    nj = pl.num_programs(0)
    nseq = tm // seq
    tf = ug_scr.shape[1]
    pos = lax.broadcasted_iota(jnp.int32, (nseq, seq, 1), 1)
    cols = slice(0, tf)

    def before(s_ref, back):
        def fix(u):
            u = u.reshape(nseq, seq, tf)
            for p in range(back):
                u = jnp.where(pos == p, s_ref[:, CONV_W - 1 - back + p:CONV_W - back + p, :], u)
            return u.reshape(tm, tf)
        return fix

    @pl.when(j == 0)
    def _():
        h_scr[...] = _rms(x_ref[...], g_ref[...]).astype(BF16)
        acc_scr[...] = jnp.zeros_like(acc_scr)
        ug_scr[0:HALO, :] = jnp.zeros((HALO, ug_scr.shape[1]), F32)
        uv_scr[0:HALO, :] = jnp.zeros((HALO, uv_scr.shape[1]), F32)

    h = h_scr[...]
    ug = jnp.dot(h, wg_ref[...], preferred_element_type=F32)
    uv = jnp.dot(h, wv_ref[...], preferred_element_type=F32)
    ug_scr[HALO:HALO + tm, :] = ug
    uv_scr[HALO:HALO + tm, :] = uv
    ug_ref[...] = ug
    uv_ref[...] = uv
    cg = _conv3(ug_scr, cwg_ref, cbg_ref, cols, tm, before(sg_ref, 1), before(sg_ref, 2))
    cv = _conv3(uv_scr, cwv_ref, cbv_ref, cols, tm, before(sv_ref, 1), before(sv_ref, 2))
    act = cg * _sigmoid(cg) * cv
    acc_scr[...] += _dot(act, wd_ref[...])

    @pl.when(j == nj - 1)
    def _():
        out = x_ref[...] + acc_scr[...]
        o_ref[...] = _rms(out, gf_ref[...]) if final_norm_out else out


def ffn_sample(x, g, w_up, w_down, layer, conv_w, conv_b, state, seq, final_g=None):
    m, k = x.shape
    tm, tf = m, FFN_TF
    nj = D_FF // tf
    nseq = m // seq
    cb = conv_b.reshape(1, -1)
    gf = (g if final_g is None else final_g).reshape(1, k)
    gate = lambda shape: pl.BlockSpec(shape, lambda j: (0, j))
    val = lambda shape: pl.BlockSpec(shape, lambda j: (0, nj + j))
    carried = lambda off: pl.BlockSpec((nseq, CONV_W - 1, tf), lambda j: (0, 0, off + j))
    return pl.pallas_call(
        functools.partial(_ffn_sample_kernel, tm=tm, seq=seq, final_norm_out=final_g is not None),
        grid=(nj,),
        in_specs=[pl.BlockSpec((tm, k), lambda j: (0, 0)),
                  pl.BlockSpec((1, k), lambda j: (0, 0)),
                  pl.BlockSpec((1, k), lambda j: (0, 0)),
                  pl.BlockSpec((None, k, tf), lambda j: (layer, 0, j)),
                  pl.BlockSpec((None, k, tf), lambda j: (layer, 0, nj + j)),
                  gate((CONV_W, tf)), val((CONV_W, tf)),
                  gate((1, tf)), val((1, tf)),
                  pl.BlockSpec((None, tf, k), lambda j: (layer, j, 0)),
                  carried(0), carried(nj)],
        out_specs=[pl.BlockSpec((tm, k), lambda j: (0, 0)), gate((tm, tf)), gate((tm, tf))],
        out_shape=[jax.ShapeDtypeStruct((m, k), F32),
                   jax.ShapeDtypeStruct((m, D_FF), F32),
                   jax.ShapeDtypeStruct((m, D_FF), F32)],
        scratch_shapes=[pltpu.VMEM((tm, k), BF16), pltpu.VMEM((tm, k), F32),
                        pltpu.VMEM((tm + HALO, tf), F32), pltpu.VMEM((tm + HALO, tf), F32)],
        compiler_params=_cparams("arbitrary"),
        name="ffn_sample",
    )(x, g.reshape(1, k), gf, w_up, w_up, conv_w, conv_w, cb, cb, w_down, state, state)


def _rot_cols(w):
    half = QK_ROPE // 2
    return jnp.concatenate([-w[..., half:], w[..., :half]], axis=-1)


def _rope_tables(pos):
    half = QK_ROPE // 2
    inv = ROPE_THETA ** (-jnp.arange(half, dtype=F32) / half)
    ang = pos.astype(F32)[:, None] * inv[None, :]
    cos = jnp.cos(ang)
    sin = jnp.sin(ang)
    zeros = jnp.zeros((pos.shape[0], LANES - QK_ROPE), F32)
    return (jnp.concatenate([cos, cos, zeros], axis=1), jnp.concatenate([sin, sin, zeros], axis=1))


def _pair_state_in(s):
    b = s.shape[0]
    m = jnp.swapaxes(s, -1, -2).reshape(b, H_B // 2, 2, HD_B, HD_B)
    z = jnp.zeros_like(m[:, :, 0])
    top = jnp.concatenate([m[:, :, 0], z], axis=-1)
    bot = jnp.concatenate([z, m[:, :, 1]], axis=-1)
    return jnp.concatenate([top, bot], axis=-2)


def _pair_state_out(m):
    b = m.shape[0]
    h0 = m[:, :, :HD_B, :HD_B]
    h1 = m[:, :, HD_B:, HD_B:]
    s = jnp.stack([h0, h1], axis=2).reshape(b, H_B, HD_B, HD_B)
    return jnp.swapaxes(s, -1, -2)


def _band_bias_prompt(table):
    t = BAND_Q
    span = BAND_SEGS * t
    nd = t + span - 1
    d = (t - 1) - jnp.arange(nd + 1)
    w = table[:, jnp.clip(d + BAND, -REL_CLIP, REL_CLIP) + REL_CLIP]
    rows = jnp.tile(w, (1, t))[:, :t * nd].reshape(table.shape[0], t, nd)
    bias = rows[:, :, t - 1:t - 1 + span]
    q = jnp.arange(t)[:, None]
    j = jnp.arange(span)[None, :]
    lo = (q // CHUNK) * CHUNK
    allowed = (j >= lo) & (j < lo + BAND + CHUNK)
    return jnp.swapaxes(jnp.where(allowed[None], bias * LOG2E, NEG_BIG), 1, 2).astype(F32)


def _band_bias_sample(table, seq, past):
    q_pos = PAST_LEN + jnp.arange(seq)
    k_pos = PAST_LEN - past + jnp.arange(past + seq)
    idx = jnp.clip(q_pos[:, None] - k_pos[None, :], -REL_CLIP, REL_CLIP) + REL_CLIP
    bias = table[:, idx].astype(F32) * LOG2E
    return bias[:, :, :past], bias[:, :, past:]


def _last_rows(proj, nseq, seq):
    last = proj.reshape(nseq, seq, -1)[:, -1]
    return jnp.concatenate([last[:, :3 * C_B], last[:, COL_WA:]], axis=1)


def _even_layer(xp, xs, e, rope_p, rope_s, dims, caches, prm):
    bp, tp, bs, ts = dims
    cache_ckv, cache_kpe, state_rwkv, state_shift = caches
    w_in = prm["ev_w_in"][e]
    wa_cols = w_in[:, :Q_LORA + KV_LORA]
    w_kpe = w_in[:, Q_LORA + KV_LORA:Q_LORA + KV_LORA + QK_ROPE]
    w_b = w_in[:, Q_LORA + KV_LORA + QK_ROPE:]
    w_ext = jnp.concatenate([w_b[:, :3 * C_B], wa_cols, w_kpe, _rot_cols(w_kpe), w_b[:, 3 * C_B:]],
                            axis=1).astype(BF16)
    g_mix = prm["norm_mix"][2 * e]
    proj_p = norm_matmul(xp, g_mix, w_ext)
    proj_s = norm_matmul(xs, g_mix, w_ext)

    wq = prm["ev_w_uq"][e].reshape(Q_LORA, H_A, QK_NOPE + QK_ROPE)
    wq_pe = wq[:, :, QK_NOPE:]
    wq_ext = jnp.concatenate([wq[:, :, :QK_NOPE], wq_pe, _rot_cols(wq_pe)], axis=-1)
    wq_ext = wq_ext.reshape(Q_LORA, H_A * HEAD_W).astype(BF16)
    wkv = prm["ev_w_ukv"][e].astype(BF16)
    qn, kvn = prm["ev_q_norm"][e], prm["ev_kv_norm"][e]
    q_p, ckv_p, kpe_p, kx_p, vt_p = mla_prep(proj_p, *rope_p, qn, kvn, wq_ext, wkv, True)
    q_s, ckv_s, kpe_s = mla_prep(proj_s, *rope_s, qn, kvn, wq_ext, wkv, False)
    o_a_p = mla_prompt(q_p, kx_p, vt_p, bp, tp)
    o_a_s = mla_sample(q_s, cache_ckv, cache_kpe, e, ckv_s, kpe_s, wkv, bs, ts)

    mu = prm["ev_mu"][e]
    mu_parts = [mu[None, 0:C_B], mu[None, C_B:2 * C_B], mu[None, 2 * C_B:3 * C_B],
                mu[None, 3 * C_B:3 * C_B + LANES], mu[None, 3 * C_B + LANES:]]
    zw = jnp.zeros((W_LORA, C_B), F32)
    ww2p = jnp.concatenate([prm["ev_w_w2"][e], zw], axis=0).astype(BF16)
    wa2p = jnp.concatenate([zw, prm["ev_w_a2"][e]], axis=0).astype(BF16)
    prep_w = (mu_parts, prm["ev_w0"][e], ww2p, prm["ev_a0"][e], wa2p, prm["ev_w_g2"][e].astype(BF16),
              prm["ev_k_k"][e], prm["ev_k_a"][e])
    acts_p = rwkv_prep(proj_p, None, *prep_w, tp)
    acts_s = rwkv_prep(proj_s, jnp.repeat(state_shift[e], ts, axis=0), *prep_w, ts)
    r_k = prm["ev_r_k"][e].reshape(-1)
    lnw, lnb = prm["ev_lnx_w"][e], prm["ev_lnx_b"][e]
    m0_p = jnp.zeros((bp, H_B // 2, LANES, LANES), F32)
    o_b_p, m_p = rwkv_chunk(acts_p, r_k, lnw, lnb, m0_p, bp, tp, min(CHUNK, tp), RWKV_PAIRS, bp)
    o_b_s, m_s = rwkv_chunk(acts_s, r_k, lnw, lnb, _pair_state_in(state_rwkv[e]), bs, ts,
                            min(CHUNK, ts), RWKV_PAIRS, 2)

    xs = out_proj(xs, o_a_s, o_b_s, prm["ev_w_out16"], e)
    outs = dict(
        ckv_p=ckv_p.reshape(bp, tp, KV_LORA), ckv_s=ckv_s.reshape(bs, ts, KV_LORA),
        kpe_p=kpe_p.reshape(bp, tp, QK_ROPE), kpe_s=kpe_s.reshape(bs, ts, QK_ROPE),
        rw_p=_pair_state_out(m_p), rw_s=_pair_state_out(m_s),
        sh_p=_last_rows(proj_p, bp, tp), sh_s=_last_rows(proj_s, bs, ts))
    return (o_a_p, o_b_p, prm["ev_w_out16"], e), xs, outs


def _odd_layer(xp, xs, o, dims, caches, prm):
    bp, tp, bs, ts = dims
    cache_k, cache_v, state_gla = caches
    band_past = cache_k.shape[2]
    w_in = prm["od_w_in"][o]
    c_gk = 3 * C_C + 2 * H_D * DK_D + H_D * DV_D
    w_ext = jnp.concatenate([w_in[:, :c_gk], w_in[:, c_gk + GK_LORA:], w_in[:, c_gk:c_gk + GK_LORA],
                             jnp.zeros((D_MODEL, LANES - GK_LORA), F32)], axis=1).astype(BF16)
    g_mix = prm["norm_mix"][2 * o + 1]
    proj_p = norm_matmul(xp, g_mix, w_ext)
    proj_s = norm_matmul(xs, g_mix, w_ext)

    table = prm["od_rel_bias"][o]
    o_c_p = band_prompt(proj_p, _band_bias_prompt(table), bp, tp)
    bias_p, bias_n = _band_bias_sample(table, ts, band_past)
    o_c_s = band_sample(proj_s, cache_k, cache_v, o, bias_p, bias_n, bs, ts)

    wgk = jnp.concatenate([prm["od_w_gk2"][o], jnp.zeros((LANES - GK_LORA, H_D * DK_D), F32)],
                          axis=0).astype(BF16)
    bgk, gn = prm["od_b_gk"][o], prm["od_gnorm"][o]
    s0_p = jnp.zeros((bp, H_D, DK_D, DV_D), F32)
    o_d_p, s_p = gla(proj_p, wgk, bgk, gn, s0_p, bp, tp, min(CHUNK, tp), bp)
    o_d_s, s_s = gla(proj_s, wgk, bgk, gn, state_gla[o], bs, ts, min(CHUNK, ts), 2)

    xs = out_proj(xs, o_c_s, o_d_s, prm["od_w_out16"], o)

    def tail_prompt(col):
        rows = proj_p.reshape(bp, tp, -1)[:, max(tp - band_past, 0):, col * C_C:(col + 1) * C_C]
        rows = rows.reshape(bp, -1, H_C, HD_C)
        return jnp.pad(rows, ((0, 0), (max(band_past - tp, 0), 0), (0, 0), (0, 0)))

    new_rows = lambda col: proj_s[:, col * C_C:(col + 1) * C_C].reshape(bs, ts, H_C, HD_C)
    outs = dict(bk_p=tail_prompt(1), bv_p=tail_prompt(2), bk_new=new_rows(1), bv_new=new_rows(2),
                gla_p=s_p, gla_s=s_s)
    return (o_c_p, o_d_p, prm["od_w_out16"], o), xs, outs


def _ffn_layer(xp, mix_p, xs, layer, dims, state_conv, prm, final_g):
    bp, tp, bs, ts = dims
    g = prm["norm_ffn"][layer]
    w_up, w_down = prm["ffn_w_up16"], prm["ffn_w_down16"]
    cw, cb = prm["ffn_conv_w"][layer], prm["ffn_conv_b"][layer]
    a1, a2, w_out, e = mix_p
    xp, tg, tv = ffn_prompt(xp, a1, a2, w_out, e, g, w_up, w_down, layer, cw, cb, tp, final_g)
    bps = tp // FFN_TM
    tails = jnp.concatenate([tg.reshape(bp, bps, 8, D_FF)[:, -1, 8 - (CONV_W - 1):],
                             tv.reshape(bp, bps, 8, D_FF)[:, -1, 8 - (CONV_W - 1):]], axis=-1)

    st = state_conv[layer]
    xs, ug, uv = ffn_sample(xs, g, w_up, w_down, layer, cw, cb, st, ts, final_g)
    keep = CONV_W - 1
    new = jnp.concatenate([ug.reshape(bs, ts, D_FF)[:, max(ts - keep, 0):],
                           uv.reshape(bs, ts, D_FF)[:, max(ts - keep, 0):]], axis=-1)
    ext = jnp.concatenate([st[:, ts:], new], axis=1)
    return xp, xs, tails, ext


@jax.jit
def kernel(x_prompt, x_sample, cache_mla_ckv, cache_mla_kpe, state_rwkv, state_rwkv_shift, cache_band_k, cache_band_v, state_gla, state_ffn_conv, norm_mix, norm_ffn, norm_final, ev_w_in, ev_q_norm, ev_w_uq, ev_kv_norm, ev_w_ukv, ev_mu, ev_w0, ev_w_w2, ev_a0, ev_w_a2, ev_w_g2, ev_k_k, ev_k_a, ev_r_k, ev_lnx_w, ev_lnx_b, ev_w_out, od_w_in, od_rel_bias, od_w_gk2, od_b_gk, od_gnorm, od_w_out, ffn_w_up, ffn_conv_w, ffn_conv_b, ffn_w_down):
    prm = dict(norm_mix=norm_mix, norm_ffn=norm_ffn, ev_w_in=ev_w_in, ev_q_norm=ev_q_norm, ev_w_uq=ev_w_uq,
               ev_kv_norm=ev_kv_norm, ev_w_ukv=ev_w_ukv, ev_mu=ev_mu, ev_w0=ev_w0, ev_w_w2=ev_w_w2,
               ev_a0=ev_a0, ev_w_a2=ev_w_a2, ev_w_g2=ev_w_g2, ev_k_k=ev_k_k, ev_k_a=ev_k_a, ev_r_k=ev_r_k,
               ev_lnx_w=ev_lnx_w, ev_lnx_b=ev_lnx_b, ev_w_out=ev_w_out, od_w_in=od_w_in,
               od_rel_bias=od_rel_bias, od_w_gk2=od_w_gk2, od_b_gk=od_b_gk, od_gnorm=od_gnorm,
               od_w_out=od_w_out, ffn_w_up=ffn_w_up, ffn_conv_w=ffn_conv_w, ffn_conv_b=ffn_conv_b,
               ffn_w_down=ffn_w_down)
    bp, tp, _ = x_prompt.shape
    bs, ts, _ = x_sample.shape
    dims = (bp, tp, bs, ts)
    depth = norm_mix.shape[0]
    xp = x_prompt.reshape(bp * tp, D_MODEL)
    xs = x_sample.reshape(bs * ts, D_MODEL)
    rope_p = _rope_tables(jnp.tile(jnp.arange(tp), bp))
    rope_s = _rope_tables(jnp.tile(PAST_LEN + jnp.arange(ts), bs))

    prm.update(ffn_w_up16=ffn_w_up.astype(BF16), ffn_w_down16=ffn_w_down.astype(BF16),
               ev_w_out16=ev_w_out.astype(BF16), od_w_out16=od_w_out.astype(BF16))

    band_rows = cache_band_k.shape[:3] + (C_C,)
    band_caches = (cache_band_k.reshape(band_rows), cache_band_v.reshape(band_rows), state_gla)

    ev, od, ffn_p, ffn_s = [], [], [], []
    for layer in range(depth):
        if layer % 2 == 0:
            mix_p, xs, outs = _even_layer(xp, xs, layer // 2, rope_p, rope_s, dims,
                                          (cache_mla_ckv, cache_mla_kpe, state_rwkv, state_rwkv_shift), prm)
            ev.append(outs)
        else:
            mix_p, xs, outs = _odd_layer(xp, xs, layer // 2, dims, band_caches, prm)
            od.append(outs)
        final_g = norm_final if layer == depth - 1 else None
        xp, xs, tails, ext = _ffn_layer(xp, mix_p, xs, layer, dims, state_ffn_conv, prm, final_g)
        ffn_p.append(tails)
        ffn_s.append(ext)

    yp, ys = xp, xs
    stack = lambda lst, key: jnp.stack([d[key] for d in lst])
    bk_s = jnp.concatenate([cache_band_k[:, :, ts:], stack(od, "bk_new")], axis=2)
    bv_s = jnp.concatenate([cache_band_v[:, :, ts:], stack(od, "bv_new")], axis=2)
    return (yp.reshape(bp, tp, D_MODEL), ys.reshape(bs, ts, D_MODEL),
            stack(ev, "ckv_p"), stack(ev, "ckv_s"), stack(ev, "kpe_p"), stack(ev, "kpe_s"),
            stack(ev, "rw_p"), stack(ev, "rw_s"), stack(ev, "sh_p"), stack(ev, "sh_s"),
            stack(od, "bk_p"), bk_s, stack(od, "bv_p"), bv_s,
            stack(od, "gla_p"), stack(od, "gla_s"), jnp.stack(ffn_p), jnp.stack(ffn_s))
```

```python
import functools

import jax
import jax.numpy as jnp
from jax import lax
from jax.experimental import pallas as pl
from jax.experimental.pallas import tpu as pltpu

F32 = jnp.float32
BF16 = jnp.bfloat16

D_MODEL = 2048
CHUNK = 64
EPS = 1e-6
PAST_LEN = 1024

H_A = 8
QK_NOPE = 128
QK_ROPE = 64
V_A = 128
Q_LORA = 512
KV_LORA = 256
ROPE_THETA = 10000.0

H_B = 16
HD_B = 64
C_B = H_B * HD_B
W_LORA = 64
A_LORA = 64
G_LORA = 128
LNX_EPS = 64e-5

H_C = 16
HD_C = 64
C_C = H_C * HD_C
BAND = 8 * CHUNK
REL_CLIP = 128

H_D = 4
DK_D = 128
DV_D = 256
GK_LORA = 16
GATE_NORM = 16.0

D_FF = 5632
CONV_W = 3

LANES = 128
VMEM_LIMIT_BYTES = 60 * 2 ** 20
NEG_BIG = -1e30
ROW_TILE = 256
LOG2E = 1.4426950408889634
MLA_SCALE = (QK_NOPE + QK_ROPE) ** -0.5 * LOG2E
BAND_SCALE = HD_C ** -0.5 * LOG2E


def _cparams(*sem):
    return pltpu.CompilerParams(dimension_semantics=sem, vmem_limit_bytes=VMEM_LIMIT_BYTES)


def _resident(shape, index_map):
    return pl.BlockSpec(shape, index_map, pipeline_mode=pl.Buffered(1))


def _dot(a, b):
    return jnp.dot(a.astype(BF16), b.astype(BF16), preferred_element_type=F32)


def _dot_nt(a, b):
    return lax.dot_general(a.astype(BF16), b.astype(BF16), (((1,), (1,)), ((), ())),
                           preferred_element_type=F32)


def _dot_tn(a, b):
    return lax.dot_general(a.astype(BF16), b.astype(BF16), (((0,), (0,)), ((), ())),
                           preferred_element_type=F32)


def _rms(x, g):
    return x * lax.rsqrt(jnp.mean(x * x, axis=-1, keepdims=True) + EPS) * g


def _sigmoid(x):
    return 1.0 / (1.0 + jnp.exp(-x))


def _softplus(x):
    return jnp.maximum(x, 0.0) + jnp.log(1.0 + jnp.exp(-jnp.abs(x)))


def _cumsum_rows(x):
    n = x.shape[0]
    row = lax.broadcasted_iota(jnp.int32, x.shape, 0)
    s = 1
    while s < n:
        x = x + jnp.where(row >= s, pltpu.roll(x, s, 0), 0.0)
        s *= 2
    return x


def _norm_matmul_kernel(x_ref, g_ref, w_ref, o_ref):
    o_ref[...] = _dot(_rms(x_ref[...], g_ref[...]), w_ref[...])


def norm_matmul(x, g, w):
    m, k = x.shape
    n = w.shape[1]
    tm = ROW_TILE
    return pl.pallas_call(
        _norm_matmul_kernel,
        grid=(m // tm,),
        in_specs=[pl.BlockSpec((tm, k), lambda i: (i, 0)),
                  pl.BlockSpec((1, k), lambda i: (0, 0)),
                  _resident((k, n), lambda i: (0, 0))],
        out_specs=pl.BlockSpec((tm, n), lambda i: (i, 0)),
        out_shape=jax.ShapeDtypeStruct((m, n), F32),
        compiler_params=_cparams("parallel"),
        name="norm_matmul",
    )(x, g.reshape(1, k), w)


def _out_proj_kernel(res_ref, a1_ref, a2_ref, w1_ref, w2_ref, o_ref):
    o_ref[...] = res_ref[...] + _dot(a1_ref[...], w1_ref[...]) + _dot(a2_ref[...], w2_ref[...])


def out_proj(res, a1, a2, w_out, e):
    m, n = res.shape
    ka = a1.shape[1]
    tm = ROW_TILE
    return pl.pallas_call(
        _out_proj_kernel,
        grid=(m // tm,),
        in_specs=[pl.BlockSpec((tm, n), lambda i: (i, 0)),
                  pl.BlockSpec((tm, ka), lambda i: (i, 0)),
                  pl.BlockSpec((tm, ka), lambda i: (i, 0)),
                  _resident((None, ka, n), lambda i: (e, 0, 0)),
                  _resident((None, ka, n), lambda i: (e, 1, 0))],
        out_specs=pl.BlockSpec((tm, n), lambda i: (i, 0)),
        out_shape=jax.ShapeDtypeStruct((m, n), F32),
        compiler_params=_cparams("parallel"),
        name="out_proj",
    )(res, a1, a2, w_out, w_out)


COL_CQ = 3 * C_B
COL_CKV = COL_CQ + Q_LORA
COL_KPE = COL_CKV + KV_LORA
COL_WA = COL_KPE + 2 * QK_ROPE
COL_G = COL_WA + W_LORA + A_LORA
COLS_EVEN_EXT = COL_G + G_LORA
HEAD_W = 2 * LANES


def _mla_prep_kernel(cq_ref, ckv_ref, kpe_ref, cos_ref, sin_ref, qn_ref, kvn_ref, wq_ref, wkv_ref,
                     q_ref, ckv_out_ref, kpe_out_ref, *kv_refs):
    cos = cos_ref[...]
    sin = sin_ref[...]
    z = _dot(_rms(cq_ref[...], qn_ref[...]), wq_ref[...])
    ckvn = _rms(ckv_ref[...], kvn_ref[...])
    ckv_out_ref[...] = ckvn
    kp = kpe_ref[...]
    kr = kp * cos + pltpu.roll(kp, QK_ROPE, 1) * sin
    kpe_out_ref[...] = kr[:, :QK_ROPE]
    for h in range(H_A):
        c0 = h * HEAD_W
        q_ref[h, :, 0:LANES] = (z[:, c0:c0 + LANES] * MLA_SCALE).astype(BF16)
        t2 = z[:, c0 + LANES:c0 + HEAD_W]
        q_ref[h, :, LANES:HEAD_W] = ((t2 * cos + pltpu.roll(t2, QK_ROPE, 1) * sin) * MLA_SCALE).astype(BF16)
    if kv_refs:
        kx_ref, vt_ref = kv_refs
        kr16 = kr.astype(BF16)
        kv = _dot(ckvn, wkv_ref[...])
        for h in range(H_A):
            c0 = h * HEAD_W
            kx_ref[h, :, 0:LANES] = kv[:, c0:c0 + LANES].astype(BF16)
            kx_ref[h, :, LANES:HEAD_W] = kr16
            vt_ref[h, 0] = kv[:, c0 + LANES:c0 + HEAD_W].T.astype(BF16)


def mla_prep(proj, cos, sin, q_norm, kv_norm, wq, wkv, expand):
    m = proj.shape[0]
    tm = ATT_K if expand else ROW_TILE
    kv_specs, kv_shapes = [], []
    if expand:
        kv_specs = [pl.BlockSpec((H_A, tm, HEAD_W), lambda i: (0, i, 0)),
                    pl.BlockSpec((H_A, 1, V_A, tm), lambda i: (0, i, 0, 0))]
        kv_shapes = [jax.ShapeDtypeStruct((H_A, m, HEAD_W), BF16),
                     jax.ShapeDtypeStruct((H_A, m // tm, V_A, tm), BF16)]
    return pl.pallas_call(
        _mla_prep_kernel,
        grid=(m // tm,),
        in_specs=[pl.BlockSpec((tm, Q_LORA), lambda i: (i, COL_CQ // Q_LORA)),
                  pl.BlockSpec((tm, KV_LORA), lambda i: (i, COL_CKV // KV_LORA)),
                  pl.BlockSpec((tm, LANES), lambda i: (i, COL_KPE // LANES)),
                  pl.BlockSpec((tm, LANES), lambda i: (i, 0)),
                  pl.BlockSpec((tm, LANES), lambda i: (i, 0)),
                  pl.BlockSpec((1, Q_LORA), lambda i: (0, 0)),
                  pl.BlockSpec((1, KV_LORA), lambda i: (0, 0)),
                  _resident((Q_LORA, H_A * HEAD_W), lambda i: (0, 0)),
                  _resident((KV_LORA, H_A * HEAD_W), lambda i: (0, 0))],
        out_specs=[pl.BlockSpec((H_A, tm, HEAD_W), lambda i: (0, i, 0)),
                   pl.BlockSpec((tm, KV_LORA), lambda i: (i, 0)),
                   pl.BlockSpec((tm, QK_ROPE), lambda i: (i, 0))] + kv_specs,
        out_shape=[jax.ShapeDtypeStruct((H_A, m, HEAD_W), BF16),
                   jax.ShapeDtypeStruct((m, KV_LORA), F32),
                   jax.ShapeDtypeStruct((m, QK_ROPE), F32)] + kv_shapes,
        compiler_params=_cparams("parallel"),
        name="mla_prep",
    )(proj, proj, proj, cos, sin, q_norm.reshape(1, -1), kv_norm.reshape(1, -1), wq, wkv)


ATT_Q = 512
ATT_K = 512
MLA_HEADS_PER_STEP = 4


def _mla_prompt_kernel(q_ref, k_ref, vt_ref, o_ref):
    t, tk = ATT_Q, ATT_K
    qi = pl.program_id(2)
    n_full = (qi * t) // tk

    def block(j, carry, masked):
        start = pl.multiple_of(j * tk, tk)
        heads = range(MLA_HEADS_PER_STEP)
        s = [lax.dot_general(k_ref[g, pl.ds(start, tk), :], q_ref[g], (((1,), (1,)), ((), ())),
                             preferred_element_type=F32) for g in heads]
        if masked:
            kc = (lax.broadcasted_iota(jnp.int32, (tk, t), 0) + start) // CHUNK
            qc = (lax.broadcasted_iota(jnp.int32, (tk, t), 1) + qi * t) // CHUNK
            s = [jnp.where(kc <= qc, x, NEG_BIG) for x in s]
        m_new = [jnp.maximum(carry[g][0], jnp.max(s[g], axis=0, keepdims=True)) for g in heads]
        p = [jnp.exp2(s[g] - m_new[g]) for g in heads]
        pv = [jnp.dot(vt_ref[g, j], p[g].astype(BF16), preferred_element_type=F32) for g in heads]
        out = []
        for g in heads:
            m, l, acc = carry[g]
            alpha = jnp.exp2(m - m_new[g])
            out.append((m_new[g], alpha * l + jnp.sum(p[g], axis=0, keepdims=True), alpha * acc + pv[g]))
        return tuple(out)

    init = tuple((jnp.full((1, t), NEG_BIG, F32), jnp.zeros((1, t), F32), jnp.zeros((V_A, t), F32))
                 for _ in range(MLA_HEADS_PER_STEP))
    carry = lax.fori_loop(0, n_full, lambda j, c: block(j, c, False), init)
    carry = block(n_full, carry, True)
    for g in range(MLA_HEADS_PER_STEP):
        m, l, acc = carry[g]
        o_ref[:, g * V_A:(g + 1) * V_A] = (acc / l).T.astype(BF16)


def mla_prompt(q, kx, vt, batch, seq):
    t = ATT_Q
    g = MLA_HEADS_PER_STEP
    assert seq % ATT_K == 0 and ATT_K % t == 0 and H_A % g == 0
    nq = seq // t
    nk = seq // ATT_K
    return pl.pallas_call(
        _mla_prompt_kernel,
        grid=(batch, H_A // g, nq),
        in_specs=[pl.BlockSpec((g, t, HEAD_W), lambda b, h, i: (h, b * nq + i, 0)),
                  pl.BlockSpec((g, seq, HEAD_W), lambda b, h, i: (h, b, 0)),
                  pl.BlockSpec((g, nk, V_A, ATT_K), lambda b, h, i: (h, b, 0, 0))],
        out_specs=pl.BlockSpec((t, g * V_A), lambda b, h, i: (b * nq + i, h)),
        out_shape=jax.ShapeDtypeStruct((batch * seq, H_A * V_A), BF16),
        compiler_params=_cparams("parallel", "parallel", "arbitrary"),
        name="mla_prompt",
    )(q, kx, vt)


def _mla_sample_kernel(q_ref, ckvp_ref, kpep_ref, ckvn_ref, kpen_ref, wkv_ref, o_ref):
    seq = ckvn_ref.shape[0]
    ckvp = ckvp_ref[0].astype(BF16)
    ckvn = ckvn_ref[...].astype(BF16)
    kpep = kpep_ref[0].astype(BF16)
    kpen = kpen_ref[...].astype(BF16)
    heads = range(H_A)
    q_lat = jnp.concatenate([_dot_nt(q_ref[h, :, 0:QK_NOPE], wkv_ref[:, h * HEAD_W:h * HEAD_W + QK_NOPE])
                             for h in heads], axis=0)
    q_pe = jnp.concatenate([q_ref[h, :, LANES:LANES + QK_ROPE] for h in heads], axis=0)
    s_p = _dot_nt(q_lat, ckvp) + _dot_nt(q_pe, kpep)
    s_n = _dot_nt(q_lat, ckvn) + _dot_nt(q_pe, kpen)
    m = jnp.maximum(jnp.max(s_p, axis=-1, keepdims=True), jnp.max(s_n, axis=-1, keepdims=True))
    p_p = jnp.exp2(s_p - m)
    p_n = jnp.exp2(s_n - m)
    l = jnp.sum(p_p, axis=-1, keepdims=True) + jnp.sum(p_n, axis=-1, keepdims=True)
    o_lat = (_dot(p_p, ckvp) + _dot(p_n, ckvn)) / l
    for h in heads:
        o_ref[:, h * V_A:(h + 1) * V_A] = _dot(
            o_lat[h * seq:(h + 1) * seq], wkv_ref[:, h * HEAD_W + QK_NOPE:(h + 1) * HEAD_W]).astype(BF16)


def mla_sample(q, ckv_past, kpe_past, e, ckv_new, kpe_new, wkv, batch, seq):
    past = ckv_past.shape[2]
    return pl.pallas_call(
        _mla_sample_kernel,
        grid=(batch,),
        in_specs=[pl.BlockSpec((H_A, seq, HEAD_W), lambda b: (0, b, 0)),
                  pl.BlockSpec((None, 1, past, KV_LORA), lambda b: (e, b, 0, 0)),
                  pl.BlockSpec((None, 1, past, QK_ROPE), lambda b: (e, b, 0, 0)),
                  pl.BlockSpec((seq, KV_LORA), lambda b: (b, 0)),
                  pl.BlockSpec((seq, QK_ROPE), lambda b: (b, 0)),
                  _resident((KV_LORA, H_A * HEAD_W), lambda b: (0, 0))],
        out_specs=pl.BlockSpec((seq, H_A * V_A), lambda b: (b, 0)),
        out_shape=jax.ShapeDtypeStruct((batch * seq, H_A * V_A), BF16),
        compiler_params=_cparams("parallel"),
        name="mla_sample",
    )(q, ckv_past, kpe_past, ckv_new, kpe_new, wkv)


def _rwkv_prep_kernel(r_ref, k_ref, v_ref, wa_ref, g_ref, rp_ref, kp_ref, vp_ref, wap_ref, gp_ref,
                      mu_r_ref, mu_k_ref, mu_v_ref, mu_wa_ref, mu_g_ref,
                      w0_ref, ww2_ref, a0_ref, wa2_ref, wg2_ref, kk_ref, ka_ref,
                      r_out, lw_out, k_out, v_out, kk_out, a_out, g_out, *, tm, seq, first_rows):
    row = lax.broadcasted_iota(jnp.int32, (tm, 1), 0)
    if not first_rows:
        keep = ((pl.program_id(0) * tm) % seq != 0).astype(F32)

    def mix(p_ref, pp_ref, mu_ref):
        p = p_ref[...]
        rolled = pltpu.roll(p, 1, 0)
        if first_rows:
            prev = jnp.where(row % seq == 0, pp_ref[...], rolled)
        else:
            prev = jnp.where(row == 0, pp_ref[7:8, :] * keep, rolled)
        return p + (prev - p) * mu_ref[...]

    r_out[...] = mix(r_ref, rp_ref, mu_r_ref)
    v_out[...] = mix(v_ref, vp_ref, mu_v_ref)
    k = mix(k_ref, kp_ref, mu_k_ref)
    xwa = mix(wa_ref, wap_ref, mu_wa_ref)
    xg = mix(g_ref, gp_ref, mu_g_ref)
    w_log = -_softplus(-(w0_ref[...] + _dot(jnp.tanh(xwa), ww2_ref[...]))) - 0.5
    lw_out[...] = -jnp.exp(w_log)
    a = _sigmoid(a0_ref[...] + _dot(xwa, wa2_ref[...]))
    a_out[...] = a
    g_out[...] = _dot(_sigmoid(xg), wg2_ref[...])
    kk_out[...] = k * kk_ref[...]
    k_out[...] = k * (1.0 + (a - 1.0) * ka_ref[...])


def rwkv_prep(proj, first, mu_parts, w0, ww2p, a0, wa2p, wg2, k_k, k_a, seq):
    m = proj.shape[0]
    tm = ROW_TILE
    wide = lambda c: pl.BlockSpec((tm, C_B), lambda i: (i, c))
    narrow = lambda c: pl.BlockSpec((tm, LANES), lambda i: (i, c))
    vec = lambda n: pl.BlockSpec((1, n), lambda i: (0, 0))
    out = jax.ShapeDtypeStruct((m, C_B), F32)
    if first is None:
        assert seq % tm == 0
        before = lambda i: jnp.maximum(i * (tm // 8) - 1, 0)
        prev_specs = [pl.BlockSpec((8, C_B), lambda i, c=c: (before(i), c)) for c in range(3)]
        prev_specs += [pl.BlockSpec((8, LANES), lambda i, c=c: (before(i), c))
                       for c in (COL_WA // LANES, COL_G // LANES)]
        prev = proj
    else:
        assert tm % seq == 0
        prev_specs = [wide(0), wide(1), wide(2), narrow(3 * C_B // LANES), narrow(3 * C_B // LANES + 1)]
        prev = first
    return pl.pallas_call(
        functools.partial(_rwkv_prep_kernel, tm=tm, seq=seq, first_rows=first is not None),
        grid=(m // tm,),
        in_specs=[wide(0), wide(1), wide(2), narrow(COL_WA // LANES), narrow(COL_G // LANES),
                  *prev_specs,
                  vec(C_B), vec(C_B), vec(C_B), vec(LANES), vec(LANES),
                  vec(C_B), _resident((LANES, C_B), lambda i: (0, 0)),
                  vec(C_B), _resident((LANES, C_B), lambda i: (0, 0)),
                  _resident((G_LORA, C_B), lambda i: (0, 0)), vec(C_B), vec(C_B)],
        out_specs=[pl.BlockSpec((tm, C_B), lambda i: (i, 0))] * 7,
        out_shape=[out] * 7,
        compiler_params=_cparams("parallel"),
        name="rwkv_prep",
    )(proj, proj, proj, proj, proj, prev, prev, prev, prev, prev,
      *mu_parts, w0.reshape(1, -1), ww2p, a0.reshape(1, -1), wa2p, wg2,
      k_k.reshape(1, -1), k_a.reshape(1, -1))


RWKV_PAIRS = 8


def _rwkv_chunk_kernel(r_ref, lw_ref, k_ref, v_ref, kk_ref, a_ref, g_ref, rk_ref, lnw_ref, lnb_ref,
                       m0_ref, o_ref, mout_ref, m_scr, *, chunk, pairs, seqs):
    c = pl.program_id(2)
    nc = pl.num_programs(2)
    L = chunk
    L2 = 2 * L

    @pl.when(c == 0)
    def _():
        m_scr[...] = m0_ref[...]

    lane = lax.broadcasted_iota(jnp.int32, (L, LANES), 1)
    low = lane < HD_B
    row_p = lax.broadcasted_iota(jnp.int32, (L, L2), 0)
    col_p = lax.broadcasted_iota(jnp.int32, (L, L2), 1) % L
    strict = row_p > col_p
    incl = row_p >= col_p
    eye_p = (row_p == col_p).astype(F32)
    same_blk = ((lax.broadcasted_iota(jnp.int32, (L2, L2), 0) < L)
                == (lax.broadcasted_iota(jnp.int32, (L2, L2), 1) < L))
    r128 = lax.broadcasted_iota(jnp.int32, (LANES, LANES), 0)
    c128 = lax.broadcasted_iota(jnp.int32, (LANES, LANES), 1)
    same_head = (r128 < HD_B) == (c128 < HD_B)
    diag = r128 == c128

    def seg_sum(x):
        s0 = jnp.sum(jnp.where(low, x, 0.0), axis=-1, keepdims=True)
        s1 = jnp.sum(jnp.where(low, 0.0, x), axis=-1, keepdims=True)
        return jnp.where(low, s0, s1)

    def split_heads(x):
        return jnp.concatenate([jnp.where(low, x, 0.0), jnp.where(low, 0.0, x)], axis=0)

    def block_diag(x):
        return jnp.where(same_blk, jnp.concatenate([x, x], axis=0), 0.0)

    P = range(seqs * pairs)
    sqs = [p // pairs for p in P]
    sls = [slice((p % pairs) * LANES, (p % pairs + 1) * LANES) for p in P]

    def prep(b, sl):
        r = r_ref[b, :, sl]
        lw = lw_ref[b, :, sl]
        k = k_ref[b, :, sl]
        kkr = kk_ref[b, :, sl]
        kk = kkr / jnp.maximum(jnp.sqrt(seg_sum(kkr * kkr)), 1e-12)
        bvec = kk * a_ref[b, :, sl]
        ci = _cumsum_rows(lw)
        c_end = ci[L - 1:L, :]
        e_neg = jnp.exp(-ci)
        e_end = jnp.exp(c_end - ci)
        return dict(r=r, k=k, v=v_ref[b, :, sl], c_end=c_end,
                    at=-kk * jnp.exp(ci - lw), bt=bvec * e_neg, kt=k * e_neg, rt=r * jnp.exp(ci),
                    bh=bvec * e_end, kh=k * e_end)

    d = [prep(sqs[p], sls[p]) for p in P]
    sc = [_dot_nt(jnp.concatenate([x["at"], x["rt"]], axis=0),
                  jnp.concatenate([split_heads(x["bt"]), split_heads(x["kt"])], axis=0)) for x in d]
    nab = [jnp.where(strict, s[0:L, 0:L2], 0.0) for s in sc]
    nak = [jnp.where(strict, s[0:L, L2:2 * L2], 0.0) for s in sc]
    lrb = [jnp.where(incl, s[L:L2, 0:L2], 0.0) for s in sc]
    lrk = [jnp.where(incl, s[L:L2, L2:2 * L2], 0.0) for s in sc]
    v_split = [split_heads(x["v"]) for x in d]

    t_inv = [eye_p + n for n in nab]
    pw = [_dot(n, block_diag(n)) for n in nab]
    nakv = [_dot(nak[p], v_split[p]) for p in P]
    m = 2
    while m < L:
        if 2 * m < L:
            res = [_dot(pw[p], jnp.concatenate([block_diag(pw[p]), block_diag(t_inv[p])], axis=1)) for p in P]
            pw = [x[:, 0:L2] for x in res]
            t_inv = [t_inv[p] + res[p][:, L2:2 * L2] for p in P]
        else:
            t_inv = [t_inv[p] + _dot(pw[p], block_diag(t_inv[p])) for p in P]
        m *= 2
    yk = [_dot(lrk[p], v_split[p]) for p in P]
    hk = [_dot_tn(d[p]["kh"], d[p]["v"]) for p in P]

    wu = [_dot(t_inv[p], jnp.concatenate([split_heads(d[p]["at"]), split_heads(nakv[p])], axis=1)) for p in P]
    qy = [_dot(lrb[p], jnp.concatenate([split_heads(wu[p][:, 0:LANES]), split_heads(wu[p][:, LANES:])], axis=1))
          for p in P]
    gh = [_dot_tn(d[p]["bh"], wu[p]) for p in P]
    q = [d[p]["rt"] + qy[p][:, 0:LANES] for p in P]
    gmat = [jnp.where(diag, jnp.broadcast_to(jnp.exp(d[p]["c_end"]), (LANES, LANES)), 0.0)
            + jnp.where(same_head, gh[p][:, 0:LANES], 0.0) for p in P]
    ym = [_dot(jnp.concatenate([q[p], gmat[p]], axis=0), m_scr[sqs[p], p % pairs]) for p in P]
    for p in P:
        m_scr[sqs[p], p % pairs] = (ym[p][L:L + LANES]
                                    + jnp.where(same_head, gh[p][:, LANES:] + hk[p], 0.0))
    for p in P:
        sl = sls[p]
        y = ym[p][0:L] + qy[p][:, LANES:] + yk[p]
        mean = seg_sum(y) * (1.0 / HD_B)
        dev = y - mean
        var = seg_sum(dev * dev) * (1.0 / HD_B)
        yn = dev * lax.rsqrt(var + LNX_EPS) * lnw_ref[:, sl] + lnb_ref[:, sl]
        bonus = seg_sum(d[p]["r"] * d[p]["k"] * rk_ref[:, sl]) * d[p]["v"]
        o_ref[sqs[p], :, sl] = ((yn + bonus) * g_ref[sqs[p], :, sl]).astype(BF16)

    @pl.when(c == nc - 1)
    def _():
        mout_ref[...] = m_scr[...]


def rwkv_chunk(acts, r_k, lnx_w, lnx_b, m0, batch, seq, chunk, pairs, seqs):
    nc = seq // chunk
    npair = C_B // LANES
    w = pairs * LANES
    act = pl.BlockSpec((seqs, chunk, w), lambda b, pg, c: (b, c, pg))
    prm = pl.BlockSpec((1, w), lambda b, pg, c: (0, pg))
    st = pl.BlockSpec((seqs, pairs, LANES, LANES), lambda b, pg, c: (b, pg, 0, 0))
    o, m = pl.pallas_call(
        functools.partial(_rwkv_chunk_kernel, chunk=chunk, pairs=pairs, seqs=seqs),
        grid=(batch // seqs, npair // pairs, nc),
        in_specs=[act] * 7 + [prm] * 3 + [st],
        out_specs=[act, st],
        out_shape=[jax.ShapeDtypeStruct((batch, seq, C_B), BF16),
                   jax.ShapeDtypeStruct((batch, npair, LANES, LANES), F32)],
        scratch_shapes=[pltpu.VMEM((seqs, pairs, LANES, LANES), F32)],
        compiler_params=_cparams("parallel", "parallel", "arbitrary"),
        name="rwkv_chunk",
    )(*(t.reshape(batch, seq, C_B) for t in acts),
      r_k.reshape(1, -1), lnx_w.reshape(1, -1), lnx_b.reshape(1, -1), m0)
    return o.reshape(batch * seq, C_B), m


BAND_Q = 256
BAND_SEGS = BAND // BAND_Q + 1
BAND_PAIRS = 4


def _band_prompt_kernel(q_ref, k_ref, v_ref, u_ref, o_ref, bias_ref):
    qb = pl.program_id(2)
    t = BAND_Q
    span = BAND_SEGS * t

    @pl.when(jnp.logical_and(pl.program_id(1) == 0, qb == 0))
    def _():
        j = lax.broadcasted_iota(jnp.int32, (span, t), 0)
        lo = (lax.broadcasted_iota(jnp.int32, (span, t), 1) // CHUNK) * CHUNK
        allowed = jnp.logical_and(j >= lo, j < lo + BAND + CHUNK)
        for h in range(2 * BAND_PAIRS):
            rows = jnp.broadcast_to(u_ref[h], (span, u_ref.shape[2]))
            sheared = pltpu.roll(rows, u_ref.shape[2] - span, 1, stride=1, stride_axis=0)
            bias_ref[h] = jnp.where(allowed, sheared[:, :t], NEG_BIG)

    lane = lax.broadcasted_iota(jnp.int32, (t, LANES), 1)
    low = lane < HD_C
    row_low = lax.broadcasted_iota(jnp.int32, (LANES, t), 0) < HD_C
    pairs = range(BAND_PAIRS)
    heads = range(2 * BAND_PAIRS)
    segs = range(BAND_SEGS)
    sls = [slice(pr * LANES, (pr + 1) * LANES) for pr in pairs]
    starts, valid = [], []
    for seg in segs:
        kb = qb - (BAND_SEGS - 1) + seg
        starts.append(pl.multiple_of(jnp.maximum(kb, 0) * t, t))
        valid.append(kb >= 0)
    ks = [[k_ref[pl.ds(starts[seg], t), sl].astype(BF16) for seg in segs] for sl in sls]
    vt = [[v_ref[pl.ds(starts[seg], t), sl].T.astype(BF16) for seg in segs] for sl in sls]
    q = [q_ref[:, sl] * BAND_SCALE for sl in sls]
    qh = [jnp.where(low if h % 2 == 0 else jnp.logical_not(low), q[h // 2], 0.0).astype(BF16) for h in heads]
    s = [[jnp.where(valid[seg],
                    lax.dot_general(ks[h // 2][seg], qh[h], (((1,), (1,)), ((), ())),
                                    preferred_element_type=F32)
                    + bias_ref[h, seg * t:(seg + 1) * t, :], NEG_BIG)
          for seg in segs] for h in heads]
    m = [functools.reduce(jnp.maximum, [jnp.max(s[h][seg], axis=0, keepdims=True) for seg in segs])
         for h in heads]
    p = [[jnp.exp2(s[h][seg] - m[h]) for seg in segs] for h in heads]
    l = [sum(jnp.sum(p[h][seg], axis=0, keepdims=True) for seg in segs) for h in heads]
    acc = [sum(jnp.dot(vt[h // 2][seg], p[h][seg].astype(BF16), preferred_element_type=F32) for seg in segs)
           for h in heads]
    for pr in pairs:
        o_ref[:, sls[pr]] = jnp.where(row_low, acc[2 * pr] / l[2 * pr],
                                      acc[2 * pr + 1] / l[2 * pr + 1]).T.astype(BF16)


def band_prompt(proj, table, batch, seq):
    t = BAND_Q
    span = BAND_SEGS * t
    nq = seq // t
    w = BAND_PAIRS * LANES
    ngrp = C_C // w
    dist = jnp.arange(t + span) - span + BAND
    u = (table[:, jnp.clip(dist, -REL_CLIP, REL_CLIP) + REL_CLIP] * LOG2E)[:, None, :]
    return pl.pallas_call(
        _band_prompt_kernel,
        grid=(ngrp, batch, nq),
        in_specs=[pl.BlockSpec((t, w), lambda hp, b, i: (b * nq + i, hp)),
                  pl.BlockSpec((seq, w), lambda hp, b, i: (b, ngrp + hp)),
                  pl.BlockSpec((seq, w), lambda hp, b, i: (b, 2 * ngrp + hp)),
                  pl.BlockSpec((2 * BAND_PAIRS, 1, t + span), lambda hp, b, i: (hp, 0, 0))],
        out_specs=pl.BlockSpec((t, w), lambda hp, b, i: (b * nq + i, hp)),
        out_shape=jax.ShapeDtypeStruct((batch * seq, C_C), BF16),
        scratch_shapes=[pltpu.VMEM((2 * BAND_PAIRS, span, t), F32)],
        compiler_params=_cparams("parallel", "arbitrary", "arbitrary"),
        name="band_prompt",
    )(proj, proj, proj, u)


def _band_sample_kernel(q_ref, kn_ref, vn_ref, kp_ref, vp_ref, biasp_ref, biasn_ref, o_ref):
    t = q_ref.shape[0]
    lane = lax.broadcasted_iota(jnp.int32, (t, LANES), 1)
    low = lane < HD_C
    pairs = range(C_C // LANES)
    heads = range(H_C)
    sls = [slice(hp * LANES, (hp + 1) * LANES) for hp in pairs]
    kp = [kp_ref[0, :, sl].astype(BF16) for sl in sls]
    vp = [vp_ref[0, :, sl].astype(BF16) for sl in sls]
    kn = [kn_ref[:, sl].astype(BF16) for sl in sls]
    vn = [vn_ref[:, sl].astype(BF16) for sl in sls]
    q = [q_ref[:, sl] * BAND_SCALE for sl in sls]
    qh = [jnp.where(low if h % 2 == 0 else jnp.logical_not(low), q[h // 2], 0.0).astype(BF16) for h in heads]
    s_p = [_dot_nt(qh[h], kp[h // 2]) + biasp_ref[h] for h in heads]
    s_n = [_dot_nt(qh[h], kn[h // 2]) + biasn_ref[h] for h in heads]
    m = [jnp.maximum(jnp.max(s_p[h], axis=-1, keepdims=True), jnp.max(s_n[h], axis=-1, keepdims=True))
         for h in heads]
    p_p = [jnp.exp2(s_p[h] - m[h]) for h in heads]
    p_n = [jnp.exp2(s_n[h] - m[h]) for h in heads]
    l = [jnp.sum(p_p[h], axis=-1, keepdims=True) + jnp.sum(p_n[h], axis=-1, keepdims=True) for h in heads]
    o = [(_dot(p_p[h], vp[h // 2]) + _dot(p_n[h], vn[h // 2])) / l[h] for h in heads]
    for hp in pairs:
        o_ref[:, sls[hp]] = jnp.where(low, o[2 * hp], o[2 * hp + 1]).astype(BF16)


def band_sample(proj, k_past, v_past, o, bias_p, bias_n, batch, seq):
    past = k_past.shape[2]
    new = lambda col: pl.BlockSpec((seq, C_C), lambda b: (b, col))
    old = pl.BlockSpec((None, 1, past, C_C), lambda b: (o, b, 0, 0))
    return pl.pallas_call(
        _band_sample_kernel,
        grid=(batch,),
        in_specs=[new(0), new(1), new(2), old, old,
                  _resident((H_C, seq, past), lambda b: (0, 0, 0)),
                  _resident((H_C, seq, seq), lambda b: (0, 0, 0))],
        out_specs=pl.BlockSpec((seq, C_C), lambda b: (b, 0)),
        out_shape=jax.ShapeDtypeStruct((batch * seq, C_C), BF16),
        compiler_params=_cparams("parallel"),
        name="band_sample",
    )(proj, proj, proj, k_past, v_past, bias_p, bias_n)


def _gla_kernel(q_ref, k_ref, v_ref, go_ref, gk_ref, wgk_ref, bgk_ref, gn_ref, s0_ref,
                o_ref, sout_ref, s_scr, *, chunk, seqs):
    c = pl.program_id(1)
    nc = pl.num_programs(1)
    L = chunk

    @pl.when(c == 0)
    def _():
        s_scr[...] = s0_ref[...]

    causal = (lax.broadcasted_iota(jnp.int32, (L, L), 0) >= lax.broadcasted_iota(jnp.int32, (L, L), 1))
    diag = (lax.broadcasted_iota(jnp.int32, (DK_D, DK_D), 0)
            == lax.broadcasted_iota(jnp.int32, (DK_D, DK_D), 1))
    S = range(seqs)
    P = [(sq, h) for sq in S for h in range(H_D)]
    ksl = [slice(h * DK_D, (h + 1) * DK_D) for h in range(H_D)]
    vsl = [slice(h * DV_D, (h + 1) * DV_D) for h in range(H_D)]
    b = [_cumsum_rows(-_softplus(-(_dot(gk_ref[sq], wgk_ref[...]) + bgk_ref[...])) / GATE_NORM) for sq in S]
    eb = [jnp.exp(x) for x in b]
    enb = [jnp.exp(-x) for x in b]
    ed = [jnp.exp(x[L - 1:L, :] - x) for x in b]
    e_end = [jnp.exp(x[L - 1:L, :]) for x in b]
    qe = [q_ref[sq, :, ksl[h]] * DK_D ** -0.5 * eb[sq][:, ksl[h]] for sq, h in P]
    vh = [v_ref[sq, :, vsl[h]].astype(BF16) for sq, h in P]
    att = [jnp.where(causal, _dot_nt(qe[i], k_ref[sq, :, ksl[h]] * enb[sq][:, ksl[h]]), 0.0)
           for i, (sq, h) in enumerate(P)]
    s_prev = [s_scr[sq, h] for sq, h in P]
    o = [_dot(att[i], vh[i]) + _dot(qe[i], s_prev[i]) for i in range(len(P))]
    kv = [_dot_tn(k_ref[sq, :, ksl[h]] * ed[sq][:, ksl[h]], vh[i]) for i, (sq, h) in enumerate(P)]
    for i, (sq, h) in enumerate(P):
        e_col = jnp.sum(jnp.where(diag, jnp.broadcast_to(e_end[sq][:, ksl[h]], (DK_D, DK_D)), 0.0),
                        axis=1, keepdims=True)
        s_scr[sq, h] = s_prev[i] * e_col + kv[i]
    for i, (sq, h) in enumerate(P):
        on = o[i] * lax.rsqrt(jnp.mean(o[i] * o[i], axis=-1, keepdims=True) + EPS) * gn_ref[...]
        gate = go_ref[sq, :, vsl[h]]
        o_ref[sq, :, vsl[h]] = (on * (gate * _sigmoid(gate))).astype(BF16)

    @pl.when(c == nc - 1)
    def _():
        sout_ref[...] = s_scr[...]


COL_QD = 3 * C_C
COL_KD = COL_QD + H_D * DK_D
COL_VD = COL_KD + H_D * DK_D
COL_GO = COL_VD + H_D * DV_D
COL_GK = COL_GO + H_D * DV_D
COLS_ODD_EXT = COL_GK + LANES


def gla(proj, wgk, bgk, gnorm, s0, batch, seq, chunk, seqs):
    nc = seq // chunk
    kw = H_D * DK_D
    vw = H_D * DV_D
    proj3 = proj.reshape(batch, seq, proj.shape[1])
    act = lambda w, col: pl.BlockSpec((seqs, chunk, w), lambda b, c: (b, c, col))
    st = pl.BlockSpec((seqs, H_D, DK_D, DV_D), lambda b, c: (b, 0, 0, 0))
    o, s = pl.pallas_call(
        functools.partial(_gla_kernel, chunk=chunk, seqs=seqs),
        grid=(batch // seqs, nc),
        in_specs=[act(kw, COL_QD // kw), act(kw, COL_KD // kw), act(vw, COL_VD // vw), act(vw, COL_GO // vw),
                  act(LANES, COL_GK // LANES),
                  _resident((LANES, kw), lambda b, c: (0, 0)),
                  pl.BlockSpec((1, kw), lambda b, c: (0, 0)),
                  pl.BlockSpec((1, DV_D), lambda b, c: (0, 0)),
                  st],
        out_specs=[act(vw, 0), st],
        out_shape=[jax.ShapeDtypeStruct((batch, seq, vw), BF16),
                   jax.ShapeDtypeStruct((batch, H_D, DK_D, DV_D), F32)],
        scratch_shapes=[pltpu.VMEM((seqs, H_D, DK_D, DV_D), F32)],
        compiler_params=_cparams("parallel", "arbitrary"),
        name="gla",
    )(proj3, proj3, proj3, proj3, proj3, wgk, bgk.reshape(1, -1), gnorm.reshape(1, -1), s0)
    return o.reshape(batch * seq, vw), s


FFN_TM = 512
FFN_TF = 512
HALO = 16


def _conv3(u_scr, cw_ref, cb_ref, cols, tm, prev1=None, prev2=None):
    u0 = u_scr[HALO:HALO + tm, cols]
    u1 = u_scr[HALO - 1:HALO - 1 + tm, cols]
    u2 = u_scr[HALO - 2:HALO - 2 + tm, cols]
    if prev1 is not None:
        u1 = prev1(u1)
        u2 = prev2(u2)
    return cb_ref[:, cols] + cw_ref[2:3, cols] * u0 + cw_ref[1:2, cols] * u1 + cw_ref[0:1, cols] * u2


def _ffn_prompt_kernel(x_ref, a1_ref, a2_ref, xh_ref, a1h_ref, a2h_ref, w1_ref, w2_ref, g_ref, gf_ref,
                       wg_ref, wv_ref, cwg_ref, cwv_ref, cbg_ref, cbv_ref, wd_ref,
                       o_ref, tg_ref, tv_ref, h_scr, *u_scrs, tm, blocks_per_seq, final_norm_out):
    i = pl.program_id(0)
    j = pl.program_id(1)
    nj = pl.num_programs(1)

    @pl.when(j == 0)
    def _():
        w1 = w1_ref[...]
        w2 = w2_ref[...]
        xm = x_ref[...] + _dot(a1_ref[...], w1) + _dot(a2_ref[...], w2)
        o_ref[...] = xm
        h_scr[HALO:HALO + tm, :] = _rms(xm, g_ref[...]).astype(BF16)
        keep = (i % blocks_per_seq != 0).astype(F32)
        xmh = xh_ref[...] + _dot(a1h_ref[...], w1) + _dot(a2h_ref[...], w2)
        h_scr[0:HALO, :] = (_rms(xmh, g_ref[...]) * keep).astype(BF16)

    h = h_scr[...]
    ug_scr, uv_scr = u_scrs
    ug_scr[...] = jnp.dot(h, wg_ref[...], preferred_element_type=F32)
    uv_scr[...] = jnp.dot(h, wv_ref[...], preferred_element_type=F32)
    acts = []
    for c in range(0, ug_scr.shape[1], LANES):
        lanes = slice(c, c + LANES)
        cg = _conv3(ug_scr, cwg_ref, cbg_ref, lanes, tm)
        cv = _conv3(uv_scr, cwv_ref, cbv_ref, lanes, tm)
        acts.append((cg * _sigmoid(cg) * cv).astype(BF16))
    o_ref[...] += jnp.dot(jnp.concatenate(acts, axis=1), wd_ref[...], preferred_element_type=F32)
    tg_ref[0] = ug_scr[HALO + tm - 8:HALO + tm, :]
    tv_ref[0] = uv_scr[HALO + tm - 8:HALO + tm, :]

    if final_norm_out:
        @pl.when(j == nj - 1)
        def _():
            o_ref[...] = _rms(o_ref[...], gf_ref[...])


def ffn_prompt(x, a1, a2, w_out, e, g, w_up, w_down, layer, conv_w, conv_b, seq, final_g=None):
    m, k = x.shape
    ka = a1.shape[1]
    tm, tf = FFN_TM, FFN_TF
    nj = D_FF // tf
    nblk = m // tm
    bps = seq // tm
    halo_blocks = tm // HALO
    cb = conv_b.reshape(1, -1)
    gf = (g if final_g is None else final_g).reshape(1, k)
    before = lambda i, j: (jnp.maximum(i * halo_blocks - 1, 0), 0)
    return pl.pallas_call(
        functools.partial(_ffn_prompt_kernel, tm=tm, blocks_per_seq=bps, final_norm_out=final_g is not None),
        grid=(nblk, nj),
        in_specs=[pl.BlockSpec((tm, k), lambda i, j: (i, 0)),
                  pl.BlockSpec((tm, ka), lambda i, j: (i, 0)),
                  pl.BlockSpec((tm, ka), lambda i, j: (i, 0)),
                  pl.BlockSpec((HALO, k), before),
                  pl.BlockSpec((HALO, ka), before),
                  pl.BlockSpec((HALO, ka), before),
                  _resident((None, ka, k), lambda i, j: (e, 0, 0)),
                  _resident((None, ka, k), lambda i, j: (e, 1, 0)),
                  pl.BlockSpec((1, k), lambda i, j: (0, 0)),
                  pl.BlockSpec((1, k), lambda i, j: (0, 0)),
                  pl.BlockSpec((None, k, tf), lambda i, j: (layer, 0, j)),
                  pl.BlockSpec((None, k, tf), lambda i, j: (layer, 0, nj + j)),
                  pl.BlockSpec((CONV_W, tf), lambda i, j: (0, j)),
                  pl.BlockSpec((CONV_W, tf), lambda i, j: (0, nj + j)),
                  pl.BlockSpec((1, tf), lambda i, j: (0, j)),
                  pl.BlockSpec((1, tf), lambda i, j: (0, nj + j)),
                  pl.BlockSpec((None, tf, k), lambda i, j: (layer, j, 0))],
        out_specs=[pl.BlockSpec((tm, k), lambda i, j: (i, 0)),
                   pl.BlockSpec((1, 8, tf), lambda i, j: (i, 0, j)),
                   pl.BlockSpec((1, 8, tf), lambda i, j: (i, 0, j))],
        out_shape=[jax.ShapeDtypeStruct((m, k), F32),
                   jax.ShapeDtypeStruct((nblk, 8, D_FF), F32),
                   jax.ShapeDtypeStruct((nblk, 8, D_FF), F32)],
        scratch_shapes=[pltpu.VMEM((tm + HALO, k), BF16)] + [pltpu.VMEM((tm + HALO, tf), F32)] * 2,
        compiler_params=_cparams("parallel", "arbitrary"),
        name="ffn_prompt",
    )(x, a1, a2, x, a1, a2, w_out, w_out, g.reshape(1, k), gf, w_up, w_up, conv_w, conv_w, cb, cb, w_down)


def _ffn_sample_kernel(x_ref, g_ref, gf_ref, wg_ref, wv_ref, cwg_ref, cwv_ref, cbg_ref, cbv_ref, wd_ref,
                       sg_ref, sv_ref,
                       o_ref, ug_ref, uv_ref, h_scr, acc_scr, ug_scr, uv_scr, *, tm, seq, final_norm_out):
    j = pl.program_id(0)
    nj = pl.num_programs(0)
    nseq = tm // seq
    tf = ug_scr.shape[1]
    pos = lax.broadcasted_iota(jnp.int32, (nseq, seq, 1), 1)
    cols = slice(0, tf)

    def before(s_ref, back):
        def fix(u):
            u = u.reshape(nseq, seq, tf)
            for p in range(back):
                u = jnp.where(pos == p, s_ref[:, CONV_W - 1 - back + p:CONV_W - back + p, :], u)
            return u.reshape(tm, tf)
        return fix

    @pl.when(j == 0)
    def _():
        h_scr[...] = _rms(x_ref[...], g_ref[...]).astype(BF16)
        acc_scr[...] = jnp.zeros_like(acc_scr)
        ug_scr[0:HALO, :] = jnp.zeros((HALO, ug_scr.shape[1]), F32)
        uv_scr[0:HALO, :] = jnp.zeros((HALO, uv_scr.shape[1]), F32)

    h = h_scr[...]
    ug = jnp.dot(h, wg_ref[...], preferred_element_type=F32)
    uv = jnp.dot(h, wv_ref[...], preferred_element_type=F32)
    ug_scr[HALO:HALO + tm, :] = ug
    uv_scr[HALO:HALO + tm, :] = uv
    ug_ref[...] = ug
    uv_ref[...] = uv
    cg = _conv3(ug_scr, cwg_ref, cbg_ref, cols, tm, before(sg_ref, 1), before(sg_ref, 2))
    cv = _conv3(uv_scr, cwv_ref, cbv_ref, cols, tm, before(sv_ref, 1), before(sv_ref, 2))
    act = cg * _sigmoid(cg) * cv
    acc_scr[...] += _dot(act, wd_ref[...])

    @pl.when(j == nj - 1)
    def _():
        out = x_ref[...] + acc_scr[...]
        o_ref[...] = _rms(out, gf_ref[...]) if final_norm_out else out


def ffn_sample(x, g, w_up, w_down, layer, conv_w, conv_b, state, seq, final_g=None):
    m, k = x.shape
    tm, tf = m, FFN_TF
    nj = D_FF // tf
    nseq = m // seq
    cb = conv_b.reshape(1, -1)
    gf = (g if final_g is None else final_g).reshape(1, k)
    gate = lambda shape: pl.BlockSpec(shape, lambda j: (0, j))
    val = lambda shape: pl.BlockSpec(shape, lambda j: (0, nj + j))
    carried = lambda off: pl.BlockSpec((nseq, CONV_W - 1, tf), lambda j: (0, 0, off + j))
    return pl.pallas_call(
        functools.partial(_ffn_sample_kernel, tm=tm, seq=seq, final_norm_out=final_g is not None),
        grid=(nj,),
        in_specs=[pl.BlockSpec((tm, k), lambda j: (0, 0)),
                  pl.BlockSpec((1, k), lambda j: (0, 0)),
                  pl.BlockSpec((1, k), lambda j: (0, 0)),
                  pl.BlockSpec((None, k, tf), lambda j: (layer, 0, j)),
                  pl.BlockSpec((None, k, tf), lambda j: (layer, 0, nj + j)),
                  gate((CONV_W, tf)), val((CONV_W, tf)),
                  gate((1, tf)), val((1, tf)),
                  pl.BlockSpec((None, tf, k), lambda j: (layer, j, 0)),
                  carried(0), carried(nj)],
        out_specs=[pl.BlockSpec((tm, k), lambda j: (0, 0)), gate((tm, tf)), gate((tm, tf))],
        out_shape=[jax.ShapeDtypeStruct((m, k), F32),
                   jax.ShapeDtypeStruct((m, D_FF), F32),
                   jax.ShapeDtypeStruct((m, D_FF), F32)],
        scratch_shapes=[pltpu.VMEM((tm, k), BF16), pltpu.VMEM((tm, k), F32),
                        pltpu.VMEM((tm + HALO, tf), F32), pltpu.VMEM((tm + HALO, tf), F32)],
        compiler_params=_cparams("arbitrary"),
        name="ffn_sample",
    )(x, g.reshape(1, k), gf, w_up, w_up, conv_w, conv_w, cb, cb, w_down, state, state)


def _rot_cols(w):
    half = QK_ROPE // 2
    return jnp.concatenate([-w[..., half:], w[..., :half]], axis=-1)


def _rope_tables(pos):
    half = QK_ROPE // 2
    inv = ROPE_THETA ** (-jnp.arange(half, dtype=F32) / half)
    ang = pos.astype(F32)[:, None] * inv[None, :]
    cos = jnp.cos(ang)
    sin = jnp.sin(ang)
    zeros = jnp.zeros((pos.shape[0], LANES - QK_ROPE), F32)
    return (jnp.concatenate([cos, cos, zeros], axis=1), jnp.concatenate([sin, sin, zeros], axis=1))


def _pair_state_in(s):
    b = s.shape[0]
    m = jnp.swapaxes(s, -1, -2).reshape(b, H_B // 2, 2, HD_B, HD_B)
    z = jnp.zeros_like(m[:, :, 0])
    top = jnp.concatenate([m[:, :, 0], z], axis=-1)
    bot = jnp.concatenate([z, m[:, :, 1]], axis=-1)
    return jnp.concatenate([top, bot], axis=-2)


def _pair_state_out(m):
    b = m.shape[0]
    h0 = m[:, :, :HD_B, :HD_B]
    h1 = m[:, :, HD_B:, HD_B:]
    s = jnp.stack([h0, h1], axis=2).reshape(b, H_B, HD_B, HD_B)
    return jnp.swapaxes(s, -1, -2)


def _band_bias_sample(table, seq, past):
    q_pos = PAST_LEN + jnp.arange(seq)
    k_pos = PAST_LEN - past + jnp.arange(past + seq)
    idx = jnp.clip(q_pos[:, None] - k_pos[None, :], -REL_CLIP, REL_CLIP) + REL_CLIP
    bias = table[:, idx].astype(F32) * LOG2E
    return bias[:, :, :past], bias[:, :, past:]


def _last_rows(proj, nseq, seq):
    last = proj.reshape(nseq, seq, -1)[:, -1]
    return jnp.concatenate([last[:, :3 * C_B], last[:, COL_WA:]], axis=1)


def _even_layer(xp, xs, e, rope_p, rope_s, dims, caches, prm):
    bp, tp, bs, ts = dims
    cache_ckv, cache_kpe, state_rwkv, state_shift = caches
    w_in = prm["ev_w_in"][e]
    wa_cols = w_in[:, :Q_LORA + KV_LORA]
    w_kpe = w_in[:, Q_LORA + KV_LORA:Q_LORA + KV_LORA + QK_ROPE]
    w_b = w_in[:, Q_LORA + KV_LORA + QK_ROPE:]
    w_ext = jnp.concatenate([w_b[:, :3 * C_B], wa_cols, w_kpe, _rot_cols(w_kpe), w_b[:, 3 * C_B:]],
                            axis=1).astype(BF16)
    g_mix = prm["norm_mix"][2 * e]
    proj_p = norm_matmul(xp, g_mix, w_ext)
    proj_s = norm_matmul(xs, g_mix, w_ext)

    wq = prm["ev_w_uq"][e].reshape(Q_LORA, H_A, QK_NOPE + QK_ROPE)
    wq_pe = wq[:, :, QK_NOPE:]
    wq_ext = jnp.concatenate([wq[:, :, :QK_NOPE], wq_pe, _rot_cols(wq_pe)], axis=-1)
    wq_ext = wq_ext.reshape(Q_LORA, H_A * HEAD_W).astype(BF16)
    wkv = prm["ev_w_ukv"][e].astype(BF16)
    qn, kvn = prm["ev_q_norm"][e], prm["ev_kv_norm"][e]
    q_p, ckv_p, kpe_p, kx_p, vt_p = mla_prep(proj_p, *rope_p, qn, kvn, wq_ext, wkv, True)
    q_s, ckv_s, kpe_s = mla_prep(proj_s, *rope_s, qn, kvn, wq_ext, wkv, False)
    o_a_p = mla_prompt(q_p, kx_p, vt_p, bp, tp)
    o_a_s = mla_sample(q_s, cache_ckv, cache_kpe, e, ckv_s, kpe_s, wkv, bs, ts)

    mu = prm["ev_mu"][e]
    mu_parts = [mu[None, 0:C_B], mu[None, C_B:2 * C_B], mu[None, 2 * C_B:3 * C_B],
                mu[None, 3 * C_B:3 * C_B + LANES], mu[None, 3 * C_B + LANES:]]
    zw = jnp.zeros((W_LORA, C_B), F32)
    ww2p = jnp.concatenate([prm["ev_w_w2"][e], zw], axis=0).astype(BF16)
    wa2p = jnp.concatenate([zw, prm["ev_w_a2"][e]], axis=0).astype(BF16)
    prep_w = (mu_parts, prm["ev_w0"][e], ww2p, prm["ev_a0"][e], wa2p, prm["ev_w_g2"][e].astype(BF16),
              prm["ev_k_k"][e], prm["ev_k_a"][e])
    acts_p = rwkv_prep(proj_p, None, *prep_w, tp)
    acts_s = rwkv_prep(proj_s, jnp.repeat(state_shift[e], ts, axis=0), *prep_w, ts)
    r_k = prm["ev_r_k"][e].reshape(-1)
    lnw, lnb = prm["ev_lnx_w"][e], prm["ev_lnx_b"][e]
    m0_p = jnp.zeros((bp, H_B // 2, LANES, LANES), F32)
    o_b_p, m_p = rwkv_chunk(acts_p, r_k, lnw, lnb, m0_p, bp, tp, min(CHUNK, tp), RWKV_PAIRS, bp)
    o_b_s, m_s = rwkv_chunk(acts_s, r_k, lnw, lnb, _pair_state_in(state_rwkv[e]), bs, ts,
                            min(CHUNK, ts), RWKV_PAIRS, 2)

    xs = out_proj(xs, o_a_s, o_b_s, prm["ev_w_out16"], e)
    outs = dict(
        ckv_p=ckv_p.reshape(bp, tp, KV_LORA), ckv_s=ckv_s.reshape(bs, ts, KV_LORA),
        kpe_p=kpe_p.reshape(bp, tp, QK_ROPE), kpe_s=kpe_s.reshape(bs, ts, QK_ROPE),
        rw_p=_pair_state_out(m_p), rw_s=_pair_state_out(m_s),
        sh_p=_last_rows(proj_p, bp, tp), sh_s=_last_rows(proj_s, bs, ts))
    return (o_a_p, o_b_p, prm["ev_w_out16"], e), xs, outs


def _odd_layer(xp, xs, o, dims, caches, prm):
    bp, tp, bs, ts = dims
    cache_k, cache_v, state_gla = caches
    band_past = cache_k.shape[2]
    w_in = prm["od_w_in"][o]
    c_gk = 3 * C_C + 2 * H_D * DK_D + H_D * DV_D
    w_ext = jnp.concatenate([w_in[:, :c_gk], w_in[:, c_gk + GK_LORA:], w_in[:, c_gk:c_gk + GK_LORA],
                             jnp.zeros((D_MODEL, LANES - GK_LORA), F32)], axis=1).astype(BF16)
    g_mix = prm["norm_mix"][2 * o + 1]
    proj_p = norm_matmul(xp, g_mix, w_ext)
    proj_s = norm_matmul(xs, g_mix, w_ext)

    table = prm["od_rel_bias"][o]
    o_c_p = band_prompt(proj_p, table, bp, tp)
    bias_p, bias_n = _band_bias_sample(table, ts, band_past)
    o_c_s = band_sample(proj_s, cache_k, cache_v, o, bias_p, bias_n, bs, ts)

    wgk = jnp.concatenate([prm["od_w_gk2"][o], jnp.zeros((LANES - GK_LORA, H_D * DK_D), F32)],
                          axis=0).astype(BF16)
    bgk, gn = prm["od_b_gk"][o], prm["od_gnorm"][o]
    s0_p = jnp.zeros((bp, H_D, DK_D, DV_D), F32)
    o_d_p, s_p = gla(proj_p, wgk, bgk, gn, s0_p, bp, tp, min(CHUNK, tp), bp)
    o_d_s, s_s = gla(proj_s, wgk, bgk, gn, state_gla[o], bs, ts, min(CHUNK, ts), 2)

    xs = out_proj(xs, o_c_s, o_d_s, prm["od_w_out16"], o)

    def tail_prompt(col):
        rows = proj_p.reshape(bp, tp, -1)[:, max(tp - band_past, 0):, col * C_C:(col + 1) * C_C]
        rows = rows.reshape(bp, -1, H_C, HD_C)
        return jnp.pad(rows, ((0, 0), (max(band_past - tp, 0), 0), (0, 0), (0, 0)))

    new_rows = lambda col: proj_s[:, col * C_C:(col + 1) * C_C].reshape(bs, ts, H_C, HD_C)
    outs = dict(bk_p=tail_prompt(1), bv_p=tail_prompt(2), bk_new=new_rows(1), bv_new=new_rows(2),
                gla_p=s_p, gla_s=s_s)
    return (o_c_p, o_d_p, prm["od_w_out16"], o), xs, outs


def _ffn_layer(xp, mix_p, xs, layer, dims, state_conv, prm, final_g):
    bp, tp, bs, ts = dims
    g = prm["norm_ffn"][layer]
    w_up, w_down = prm["ffn_w_up16"], prm["ffn_w_down16"]
    cw, cb = prm["ffn_conv_w"][layer], prm["ffn_conv_b"][layer]
    a1, a2, w_out, e = mix_p
    xp, tg, tv = ffn_prompt(xp, a1, a2, w_out, e, g, w_up, w_down, layer, cw, cb, tp, final_g)
    bps = tp // FFN_TM
    tails = jnp.concatenate([tg.reshape(bp, bps, 8, D_FF)[:, -1, 8 - (CONV_W - 1):],
                             tv.reshape(bp, bps, 8, D_FF)[:, -1, 8 - (CONV_W - 1):]], axis=-1)

    st = state_conv[layer]
    xs, ug, uv = ffn_sample(xs, g, w_up, w_down, layer, cw, cb, st, ts, final_g)
    keep = CONV_W - 1
    new = jnp.concatenate([ug.reshape(bs, ts, D_FF)[:, max(ts - keep, 0):],
                           uv.reshape(bs, ts, D_FF)[:, max(ts - keep, 0):]], axis=-1)
    ext = jnp.concatenate([st[:, ts:], new], axis=1)
    return xp, xs, tails, ext


@jax.jit
def kernel(x_prompt, x_sample, cache_mla_ckv, cache_mla_kpe, state_rwkv, state_rwkv_shift, cache_band_k, cache_band_v, state_gla, state_ffn_conv, norm_mix, norm_ffn, norm_final, ev_w_in, ev_q_norm, ev_w_uq, ev_kv_norm, ev_w_ukv, ev_mu, ev_w0, ev_w_w2, ev_a0, ev_w_a2, ev_w_g2, ev_k_k, ev_k_a, ev_r_k, ev_lnx_w, ev_lnx_b, ev_w_out, od_w_in, od_rel_bias, od_w_gk2, od_b_gk, od_gnorm, od_w_out, ffn_w_up, ffn_conv_w, ffn_conv_b, ffn_w_down):
    prm = dict(norm_mix=norm_mix, norm_ffn=norm_ffn, ev_w_in=ev_w_in, ev_q_norm=ev_q_norm, ev_w_uq=ev_w_uq,
               ev_kv_norm=ev_kv_norm, ev_w_ukv=ev_w_ukv, ev_mu=ev_mu, ev_w0=ev_w0, ev_w_w2=ev_w_w2,
               ev_a0=ev_a0, ev_w_a2=ev_w_a2, ev_w_g2=ev_w_g2, ev_k_k=ev_k_k, ev_k_a=ev_k_a, ev_r_k=ev_r_k,
               ev_lnx_w=ev_lnx_w, ev_lnx_b=ev_lnx_b, ev_w_out=ev_w_out, od_w_in=od_w_in,
               od_rel_bias=od_rel_bias, od_w_gk2=od_w_gk2, od_b_gk=od_b_gk, od_gnorm=od_gnorm,
               od_w_out=od_w_out, ffn_w_up=ffn_w_up, ffn_conv_w=ffn_conv_w, ffn_conv_b=ffn_conv_b,
               ffn_w_down=ffn_w_down)
    bp, tp, _ = x_prompt.shape
    bs, ts, _ = x_sample.shape
    dims = (bp, tp, bs, ts)
    depth = norm_mix.shape[0]
    xp = x_prompt.reshape(bp * tp, D_MODEL)
    xs = x_sample.reshape(bs * ts, D_MODEL)
    rope_p = _rope_tables(jnp.tile(jnp.arange(tp), bp))
    rope_s = _rope_tables(jnp.tile(PAST_LEN + jnp.arange(ts), bs))

    prm.update(ffn_w_up16=ffn_w_up.astype(BF16), ffn_w_down16=ffn_w_down.astype(BF16),
               ev_w_out16=ev_w_out.astype(BF16), od_w_out16=od_w_out.astype(BF16))

    band_rows = cache_band_k.shape[:3] + (C_C,)
    band_caches = (cache_band_k.reshape(band_rows), cache_band_v.reshape(band_rows), state_gla)

    ev, od, ffn_p, ffn_s = [], [], [], []
    for layer in range(depth):
        if layer % 2 == 0:
            mix_p, xs, outs = _even_layer(xp, xs, layer // 2, rope_p, rope_s, dims,
                                          (cache_mla_ckv, cache_mla_kpe, state_rwkv, state_rwkv_shift), prm)
            ev.append(outs)
        else:
            mix_p, xs, outs = _odd_layer(xp, xs, layer // 2, dims, band_caches, prm)
            od.append(outs)
        final_g = norm_final if layer == depth - 1 else None
        xp, xs, tails, ext = _ffn_layer(xp, mix_p, xs, layer, dims, state_ffn_conv, prm, final_g)
        ffn_p.append(tails)
        ffn_s.append(ext)

    yp, ys = xp, xs
    stack = lambda lst, key: jnp.stack([d[key] for d in lst])
    bk_s = jnp.concatenate([cache_band_k[:, :, ts:], stack(od, "bk_new")], axis=2)
    bv_s = jnp.concatenate([cache_band_v[:, :, ts:], stack(od, "bv_new")], axis=2)
    return (yp.reshape(bp, tp, D_MODEL), ys.reshape(bs, ts, D_MODEL),
            stack(ev, "ckv_p"), stack(ev, "ckv_s"), stack(ev, "kpe_p"), stack(ev, "kpe_s"),
            stack(ev, "rw_p"), stack(ev, "rw_s"), stack(ev, "sh_p"), stack(ev, "sh_s"),
            stack(od, "bk_p"), bk_s, stack(od, "bv_p"), bv_s,
            stack(od, "gla_p"), stack(od, "gla_s"), jnp.stack(ffn_p), jnp.stack(ffn_s))
```

```python
import functools

import jax
import jax.numpy as jnp
from jax import lax
from jax.experimental import pallas as pl
from jax.experimental.pallas import tpu as pltpu

F32 = jnp.float32
BF16 = jnp.bfloat16

D_MODEL = 2048
CHUNK = 64
EPS = 1e-6
PAST_LEN = 1024

H_A = 8
QK_NOPE = 128
QK_ROPE = 64
V_A = 128
Q_LORA = 512
KV_LORA = 256
ROPE_THETA = 10000.0

H_B = 16
HD_B = 64
C_B = H_B * HD_B
W_LORA = 64
A_LORA = 64
G_LORA = 128
LNX_EPS = 64e-5

H_C = 16
HD_C = 64
C_C = H_C * HD_C
BAND = 8 * CHUNK
REL_CLIP = 128

H_D = 4
DK_D = 128
DV_D = 256
GK_LORA = 16
GATE_NORM = 16.0

D_FF = 5632
CONV_W = 3

LANES = 128
VMEM_LIMIT_BYTES = 60 * 2 ** 20
NEG_BIG = -1e30
ROW_TILE = 256
LOG2E = 1.4426950408889634
MLA_SCALE = (QK_NOPE + QK_ROPE) ** -0.5 * LOG2E
BAND_SCALE = HD_C ** -0.5 * LOG2E


def _cparams(*sem):
    return pltpu.CompilerParams(dimension_semantics=sem, vmem_limit_bytes=VMEM_LIMIT_BYTES)


def _resident(shape, index_map):
    return pl.BlockSpec(shape, index_map, pipeline_mode=pl.Buffered(1))


def _dot(a, b):
    return jnp.dot(a.astype(BF16), b.astype(BF16), preferred_element_type=F32)


def _dot_nt(a, b):
    return lax.dot_general(a.astype(BF16), b.astype(BF16), (((1,), (1,)), ((), ())),
                           preferred_element_type=F32)


def _dot_tn(a, b):
    return lax.dot_general(a.astype(BF16), b.astype(BF16), (((0,), (0,)), ((), ())),
                           preferred_element_type=F32)


def _rms(x, g):
    return x * lax.rsqrt(jnp.mean(x * x, axis=-1, keepdims=True) + EPS) * g


def _sigmoid(x):
    return 1.0 / (1.0 + jnp.exp(-x))


def _softplus(x):
    return jnp.maximum(x, 0.0) + jnp.log(1.0 + jnp.exp(-jnp.abs(x)))


def _cumsum_rows(x):
    n = x.shape[0]
    row = lax.broadcasted_iota(jnp.int32, x.shape, 0)
    s = 1
    while s < n:
        x = x + jnp.where(row >= s, pltpu.roll(x, s, 0), 0.0)
        s *= 2
    return x


def _norm_matmul_kernel(x_ref, g_ref, w_ref, o_ref):
    o_ref[...] = _dot(_rms(x_ref[...], g_ref[...]), w_ref[...])


def norm_matmul(x, g, w):
    m, k = x.shape
    n = w.shape[1]
    tm = ROW_TILE
    return pl.pallas_call(
        _norm_matmul_kernel,
        grid=(m // tm,),
        in_specs=[pl.BlockSpec((tm, k), lambda i: (i, 0)),
                  pl.BlockSpec((1, k), lambda i: (0, 0)),
                  _resident((k, n), lambda i: (0, 0))],
        out_specs=pl.BlockSpec((tm, n), lambda i: (i, 0)),
        out_shape=jax.ShapeDtypeStruct((m, n), F32),
        compiler_params=_cparams("parallel"),
        name="norm_matmul",
    )(x, g.reshape(1, k), w)


def _out_proj_kernel(res_ref, a1_ref, a2_ref, w1_ref, w2_ref, o_ref):
    o_ref[...] = res_ref[...] + _dot(a1_ref[...], w1_ref[...]) + _dot(a2_ref[...], w2_ref[...])


def out_proj(res, a1, a2, w_out, e):
    m, n = res.shape
    ka = a1.shape[1]
    tm = ROW_TILE
    return pl.pallas_call(
        _out_proj_kernel,
        grid=(m // tm,),
        in_specs=[pl.BlockSpec((tm, n), lambda i: (i, 0)),
                  pl.BlockSpec((tm, ka), lambda i: (i, 0)),
                  pl.BlockSpec((tm, ka), lambda i: (i, 0)),
                  _resident((None, ka, n), lambda i: (e, 0, 0)),
                  _resident((None, ka, n), lambda i: (e, 1, 0))],
        out_specs=pl.BlockSpec((tm, n), lambda i: (i, 0)),
        out_shape=jax.ShapeDtypeStruct((m, n), F32),
        compiler_params=_cparams("parallel"),
        name="out_proj",
    )(res, a1, a2, w_out, w_out)


COL_CQ = 3 * C_B
COL_CKV = COL_CQ + Q_LORA
COL_KPE = COL_CKV + KV_LORA
COL_WA = COL_KPE + 2 * QK_ROPE
COL_G = COL_WA + W_LORA + A_LORA
COLS_EVEN_EXT = COL_G + G_LORA
HEAD_W = 2 * LANES


def _mla_prep_kernel(cq_ref, ckv_ref, kpe_ref, cos_ref, sin_ref, qn_ref, kvn_ref, wq_ref, wkv_ref,
                     q_ref, ckv_out_ref, kpe_out_ref, *kv_refs):
    cos = cos_ref[...]
    sin = sin_ref[...]
    z = _dot(_rms(cq_ref[...], qn_ref[...]), wq_ref[...])
    ckvn = _rms(ckv_ref[...], kvn_ref[...])
    ckv_out_ref[...] = ckvn
    kp = kpe_ref[...]
    kr = kp * cos + pltpu.roll(kp, QK_ROPE, 1) * sin
    kpe_out_ref[...] = kr[:, :QK_ROPE]
    for h in range(H_A):
        c0 = h * HEAD_W
        q_ref[h, :, 0:LANES] = (z[:, c0:c0 + LANES] * MLA_SCALE).astype(BF16)
        t2 = z[:, c0 + LANES:c0 + HEAD_W]
        q_ref[h, :, LANES:HEAD_W] = ((t2 * cos + pltpu.roll(t2, QK_ROPE, 1) * sin) * MLA_SCALE).astype(BF16)
    if kv_refs:
        kx_ref, vt_ref = kv_refs
        kr16 = kr.astype(BF16)
        kv = _dot(ckvn, wkv_ref[...])
        for h in range(H_A):
            c0 = h * HEAD_W
            kx_ref[h, :, 0:LANES] = kv[:, c0:c0 + LANES].astype(BF16)
            kx_ref[h, :, LANES:HEAD_W] = kr16
            vt_ref[h, 0] = kv[:, c0 + LANES:c0 + HEAD_W].T.astype(BF16)


def mla_prep(proj, cos, sin, q_norm, kv_norm, wq, wkv, expand):
    m = proj.shape[0]
    tm = ATT_K if expand else ROW_TILE
    kv_specs, kv_shapes = [], []
    if expand:
        kv_specs = [pl.BlockSpec((H_A, tm, HEAD_W), lambda i: (0, i, 0)),
                    pl.BlockSpec((H_A, 1, V_A, tm), lambda i: (0, i, 0, 0))]
        kv_shapes = [jax.ShapeDtypeStruct((H_A, m, HEAD_W), BF16),
                     jax.ShapeDtypeStruct((H_A, m // tm, V_A, tm), BF16)]
    return pl.pallas_call(
        _mla_prep_kernel,
        grid=(m // tm,),
        in_specs=[pl.BlockSpec((tm, Q_LORA), lambda i: (i, COL_CQ // Q_LORA)),
                  pl.BlockSpec((tm, KV_LORA), lambda i: (i, COL_CKV // KV_LORA)),
                  pl.BlockSpec((tm, LANES), lambda i: (i, COL_KPE // LANES)),
                  pl.BlockSpec((tm, LANES), lambda i: (i, 0)),
                  pl.BlockSpec((tm, LANES), lambda i: (i, 0)),
                  pl.BlockSpec((1, Q_LORA), lambda i: (0, 0)),
                  pl.BlockSpec((1, KV_LORA), lambda i: (0, 0)),
                  _resident((Q_LORA, H_A * HEAD_W), lambda i: (0, 0)),
                  _resident((KV_LORA, H_A * HEAD_W), lambda i: (0, 0))],
        out_specs=[pl.BlockSpec((H_A, tm, HEAD_W), lambda i: (0, i, 0)),
                   pl.BlockSpec((tm, KV_LORA), lambda i: (i, 0)),
                   pl.BlockSpec((tm, QK_ROPE), lambda i: (i, 0))] + kv_specs,
        out_shape=[jax.ShapeDtypeStruct((H_A, m, HEAD_W), BF16),
                   jax.ShapeDtypeStruct((m, KV_LORA), F32),
                   jax.ShapeDtypeStruct((m, QK_ROPE), F32)] + kv_shapes,
        compiler_params=_cparams("parallel"),
        name="mla_prep",
    )(proj, proj, proj, cos, sin, q_norm.reshape(1, -1), kv_norm.reshape(1, -1), wq, wkv)


ATT_Q = 512
ATT_K = 512
MLA_HEADS_PER_STEP = 8


def _mla_prompt_kernel(q_ref, k_ref, vt_ref, o_ref):
    t, tk = ATT_Q, ATT_K
    qi = pl.program_id(2)
    n_full = (qi * t) // tk

    def block(j, carry, masked):
        start = pl.multiple_of(j * tk, tk)
        heads = range(MLA_HEADS_PER_STEP)
        s = [lax.dot_general(k_ref[g, pl.ds(start, tk), :], q_ref[g], (((1,), (1,)), ((), ())),
                             preferred_element_type=F32) for g in heads]
        if masked:
            kc = (lax.broadcasted_iota(jnp.int32, (tk, t), 0) + start) // CHUNK
            qc = (lax.broadcasted_iota(jnp.int32, (tk, t), 1) + qi * t) // CHUNK
            s = [jnp.where(kc <= qc, x, NEG_BIG) for x in s]
        m_new = [jnp.maximum(carry[g][0], jnp.max(s[g], axis=0, keepdims=True)) for g in heads]
        p = [jnp.exp2(s[g] - m_new[g]) for g in heads]
        pv = [jnp.dot(vt_ref[g, j], p[g].astype(BF16), preferred_element_type=F32) for g in heads]
        out = []
        for g in heads:
            m, l, acc = carry[g]
            alpha = jnp.exp2(m - m_new[g])
            out.append((m_new[g], alpha * l + jnp.sum(p[g], axis=0, keepdims=True), alpha * acc + pv[g]))
        return tuple(out)

    init = tuple((jnp.full((1, t), NEG_BIG, F32), jnp.zeros((1, t), F32), jnp.zeros((V_A, t), F32))
                 for _ in range(MLA_HEADS_PER_STEP))
    carry = lax.fori_loop(0, n_full, lambda j, c: block(j, c, False), init)
    carry = block(n_full, carry, True)
    for g in range(MLA_HEADS_PER_STEP):
        m, l, acc = carry[g]
        o_ref[:, g * V_A:(g + 1) * V_A] = (acc / l).T.astype(BF16)


def mla_prompt(q, kx, vt, batch, seq):
    t = ATT_Q
    g = MLA_HEADS_PER_STEP
    assert seq % ATT_K == 0 and ATT_K % t == 0 and H_A % g == 0
    nq = seq // t
    nk = seq // ATT_K
    return pl.pallas_call(
        _mla_prompt_kernel,
        grid=(batch, H_A // g, nq),
        in_specs=[pl.BlockSpec((g, t, HEAD_W), lambda b, h, i: (h, b * nq + i, 0)),
                  _resident((g, seq, HEAD_W), lambda b, h, i: (h, b, 0)),
                  _resident((g, nk, V_A, ATT_K), lambda b, h, i: (h, b, 0, 0))],
        out_specs=pl.BlockSpec((t, g * V_A), lambda b, h, i: (b * nq + i, h)),
        out_shape=jax.ShapeDtypeStruct((batch * seq, H_A * V_A), BF16),
        compiler_params=_cparams("parallel", "parallel", "arbitrary"),
        name="mla_prompt",
    )(q, kx, vt)


def _mla_sample_kernel(q_ref, ckvp_ref, kpep_ref, ckvn_ref, kpen_ref, wkv_ref, o_ref):
    seq = ckvn_ref.shape[0]
    ckvp = ckvp_ref[0].astype(BF16)
    ckvn = ckvn_ref[...].astype(BF16)
    kpep = kpep_ref[0].astype(BF16)
    kpen = kpen_ref[...].astype(BF16)
    heads = range(H_A)
    q_lat = jnp.concatenate([_dot_nt(q_ref[h, :, 0:QK_NOPE], wkv_ref[:, h * HEAD_W:h * HEAD_W + QK_NOPE])
                             for h in heads], axis=0)
    q_pe = jnp.concatenate([q_ref[h, :, LANES:LANES + QK_ROPE] for h in heads], axis=0)
    s_p = _dot_nt(q_lat, ckvp) + _dot_nt(q_pe, kpep)
    s_n = _dot_nt(q_lat, ckvn) + _dot_nt(q_pe, kpen)
    m = jnp.maximum(jnp.max(s_p, axis=-1, keepdims=True), jnp.max(s_n, axis=-1, keepdims=True))
    p_p = jnp.exp2(s_p - m)
    p_n = jnp.exp2(s_n - m)
    l = jnp.sum(p_p, axis=-1, keepdims=True) + jnp.sum(p_n, axis=-1, keepdims=True)
    o_lat = (_dot(p_p, ckvp) + _dot(p_n, ckvn)) / l
    for h in heads:
        o_ref[:, h * V_A:(h + 1) * V_A] = _dot(
            o_lat[h * seq:(h + 1) * seq], wkv_ref[:, h * HEAD_W + QK_NOPE:(h + 1) * HEAD_W]).astype(BF16)


def mla_sample(q, ckv_past, kpe_past, e, ckv_new, kpe_new, wkv, batch, seq):
    past = ckv_past.shape[2]
    return pl.pallas_call(
        _mla_sample_kernel,
        grid=(batch,),
        in_specs=[pl.BlockSpec((H_A, seq, HEAD_W), lambda b: (0, b, 0)),
                  pl.BlockSpec((None, 1, past, KV_LORA), lambda b: (e, b, 0, 0)),
                  pl.BlockSpec((None, 1, past, QK_ROPE), lambda b: (e, b, 0, 0)),
                  pl.BlockSpec((seq, KV_LORA), lambda b: (b, 0)),
                  pl.BlockSpec((seq, QK_ROPE), lambda b: (b, 0)),
                  _resident((KV_LORA, H_A * HEAD_W), lambda b: (0, 0))],
        out_specs=pl.BlockSpec((seq, H_A * V_A), lambda b: (b, 0)),
        out_shape=jax.ShapeDtypeStruct((batch * seq, H_A * V_A), BF16),
        compiler_params=_cparams("parallel"),
        name="mla_sample",
    )(q, ckv_past, kpe_past, ckv_new, kpe_new, wkv)


def _rwkv_prep_kernel(r_ref, k_ref, v_ref, wa_ref, g_ref, rp_ref, kp_ref, vp_ref, wap_ref, gp_ref,
                      mu_r_ref, mu_k_ref, mu_v_ref, mu_wa_ref, mu_g_ref,
                      w0_ref, ww2_ref, a0_ref, wa2_ref, wg2_ref, kk_ref, ka_ref,
                      r_out, lw_out, k_out, v_out, kk_out, a_out, g_out, *, tm, seq, first_rows):
    row = lax.broadcasted_iota(jnp.int32, (tm, 1), 0)
    if not first_rows:
        keep = ((pl.program_id(0) * tm) % seq != 0).astype(F32)

    def mix(p_ref, pp_ref, mu_ref):
        p = p_ref[...]
        rolled = pltpu.roll(p, 1, 0)
        if first_rows:
            prev = jnp.where(row % seq == 0, pp_ref[...], rolled)
        else:
            prev = jnp.where(row == 0, pp_ref[7:8, :] * keep, rolled)
        return p + (prev - p) * mu_ref[...]

    r_out[...] = mix(r_ref, rp_ref, mu_r_ref)
    v_out[...] = mix(v_ref, vp_ref, mu_v_ref)
    k = mix(k_ref, kp_ref, mu_k_ref)
    xwa = mix(wa_ref, wap_ref, mu_wa_ref)
    xg = mix(g_ref, gp_ref, mu_g_ref)
    w_log = -_softplus(-(w0_ref[...] + _dot(jnp.tanh(xwa), ww2_ref[...]))) - 0.5
    lw_out[...] = -jnp.exp(w_log)
    a = _sigmoid(a0_ref[...] + _dot(xwa, wa2_ref[...]))
    a_out[...] = a
    g_out[...] = _dot(_sigmoid(xg), wg2_ref[...])
    kk_out[...] = k * kk_ref[...]
    k_out[...] = k * (1.0 + (a - 1.0) * ka_ref[...])


def rwkv_prep(proj, first, mu_parts, w0, ww2p, a0, wa2p, wg2, k_k, k_a, seq):
    m = proj.shape[0]
    tm = ROW_TILE
    wide = lambda c: pl.BlockSpec((tm, C_B), lambda i: (i, c))
    narrow = lambda c: pl.BlockSpec((tm, LANES), lambda i: (i, c))
    vec = lambda n: pl.BlockSpec((1, n), lambda i: (0, 0))
    out = jax.ShapeDtypeStruct((m, C_B), F32)
    if first is None:
        assert seq % tm == 0
        before = lambda i: jnp.maximum(i * (tm // 8) - 1, 0)
        prev_specs = [pl.BlockSpec((8, C_B), lambda i, c=c: (before(i), c)) for c in range(3)]
        prev_specs += [pl.BlockSpec((8, LANES), lambda i, c=c: (before(i), c))
                       for c in (COL_WA // LANES, COL_G // LANES)]
        prev = proj
    else:
        assert tm % seq == 0
        prev_specs = [wide(0), wide(1), wide(2), narrow(3 * C_B // LANES), narrow(3 * C_B // LANES + 1)]
        prev = first
    return pl.pallas_call(
        functools.partial(_rwkv_prep_kernel, tm=tm, seq=seq, first_rows=first is not None),
        grid=(m // tm,),
        in_specs=[wide(0), wide(1), wide(2), narrow(COL_WA // LANES), narrow(COL_G // LANES),
                  *prev_specs,
                  vec(C_B), vec(C_B), vec(C_B), vec(LANES), vec(LANES),
                  vec(C_B), _resident((LANES, C_B), lambda i: (0, 0)),
                  vec(C_B), _resident((LANES, C_B), lambda i: (0, 0)),
                  _resident((G_LORA, C_B), lambda i: (0, 0)), vec(C_B), vec(C_B)],
        out_specs=[pl.BlockSpec((tm, C_B), lambda i: (i, 0))] * 7,
        out_shape=[out] * 7,
        compiler_params=_cparams("parallel"),
        name="rwkv_prep",
    )(proj, proj, proj, proj, proj, prev, prev, prev, prev, prev,
      *mu_parts, w0.reshape(1, -1), ww2p, a0.reshape(1, -1), wa2p, wg2,
      k_k.reshape(1, -1), k_a.reshape(1, -1))


RWKV_PAIRS = 8


def _rwkv_chunk_kernel(r_ref, lw_ref, k_ref, v_ref, kk_ref, a_ref, g_ref, rk_ref, lnw_ref, lnb_ref,
                       m0_ref, o_ref, mout_ref, m_scr, *, chunk, pairs, seqs):
    c = pl.program_id(2)
    nc = pl.num_programs(2)
    L = chunk
    L2 = 2 * L

    @pl.when(c == 0)
    def _():
        m_scr[...] = m0_ref[...]

    lane = lax.broadcasted_iota(jnp.int32, (L, LANES), 1)
    low = lane < HD_B
    row_p = lax.broadcasted_iota(jnp.int32, (L, L2), 0)
    col_p = lax.broadcasted_iota(jnp.int32, (L, L2), 1) % L
    strict = row_p > col_p
    incl = row_p >= col_p
    eye_p = (row_p == col_p).astype(F32)
    same_blk = ((lax.broadcasted_iota(jnp.int32, (L2, L2), 0) < L)
                == (lax.broadcasted_iota(jnp.int32, (L2, L2), 1) < L))
    r128 = lax.broadcasted_iota(jnp.int32, (LANES, LANES), 0)
    c128 = lax.broadcasted_iota(jnp.int32, (LANES, LANES), 1)
    same_head = (r128 < HD_B) == (c128 < HD_B)
    diag = r128 == c128

    def seg_sum(x):
        s0 = jnp.sum(jnp.where(low, x, 0.0), axis=-1, keepdims=True)
        s1 = jnp.sum(jnp.where(low, 0.0, x), axis=-1, keepdims=True)
        return jnp.where(low, s0, s1)

    def split_heads(x):
        return jnp.concatenate([jnp.where(low, x, 0.0), jnp.where(low, 0.0, x)], axis=0)

    def block_diag(x):
        return jnp.where(same_blk, jnp.concatenate([x, x], axis=0), 0.0)

    P = range(seqs * pairs)
    sqs = [p // pairs for p in P]
    sls = [slice((p % pairs) * LANES, (p % pairs + 1) * LANES) for p in P]

    def prep(b, sl):
        r = r_ref[b, :, sl]
        lw = lw_ref[b, :, sl]
        k = k_ref[b, :, sl]
        kkr = kk_ref[b, :, sl]
        kk = kkr / jnp.maximum(jnp.sqrt(seg_sum(kkr * kkr)), 1e-12)
        bvec = kk * a_ref[b, :, sl]
        ci = _cumsum_rows(lw)
        c_end = ci[L - 1:L, :]
        e_neg = jnp.exp(-ci)
        e_end = jnp.exp(c_end - ci)
        return dict(r=r, k=k, v=v_ref[b, :, sl], c_end=c_end,
                    at=-kk * jnp.exp(ci - lw), bt=bvec * e_neg, kt=k * e_neg, rt=r * jnp.exp(ci),
                    bh=bvec * e_end, kh=k * e_end)

    d = [prep(sqs[p], sls[p]) for p in P]
    sc = [_dot_nt(jnp.concatenate([x["at"], x["rt"]], axis=0),
                  jnp.concatenate([split_heads(x["bt"]), split_heads(x["kt"])], axis=0)) for x in d]
    nab = [jnp.where(strict, s[0:L, 0:L2], 0.0) for s in sc]
    nak = [jnp.where(strict, s[0:L, L2:2 * L2], 0.0) for s in sc]
    lrb = [jnp.where(incl, s[L:L2, 0:L2], 0.0) for s in sc]
    lrk = [jnp.where(incl, s[L:L2, L2:2 * L2], 0.0) for s in sc]
    v_split = [split_heads(x["v"]) for x in d]

    t_inv = [eye_p + n for n in nab]
    pw = [_dot(n, block_diag(n)) for n in nab]
    nakv = [_dot(nak[p], v_split[p]) for p in P]
    m = 2
    while m < L:
        if 2 * m < L:
            res = [_dot(pw[p], jnp.concatenate([block_diag(pw[p]), block_diag(t_inv[p])], axis=1)) for p in P]
            pw = [x[:, 0:L2] for x in res]
            t_inv = [t_inv[p] + res[p][:, L2:2 * L2] for p in P]
        else:
            t_inv = [t_inv[p] + _dot(pw[p], block_diag(t_inv[p])) for p in P]
        m *= 2
    yk = [_dot(lrk[p], v_split[p]) for p in P]
    hk = [_dot_tn(d[p]["kh"], d[p]["v"]) for p in P]

    wu = [_dot(t_inv[p], jnp.concatenate([split_heads(d[p]["at"]), split_heads(nakv[p])], axis=1)) for p in P]
    qy = [_dot(lrb[p], jnp.concatenate([split_heads(wu[p][:, 0:LANES]), split_heads(wu[p][:, LANES:])], axis=1))
          for p in P]
    gh = [_dot_tn(d[p]["bh"], wu[p]) for p in P]
    q = [d[p]["rt"] + qy[p][:, 0:LANES] for p in P]
    gmat = [jnp.where(diag, jnp.broadcast_to(jnp.exp(d[p]["c_end"]), (LANES, LANES)), 0.0)
            + jnp.where(same_head, gh[p][:, 0:LANES], 0.0) for p in P]
    ym = [_dot(jnp.concatenate([q[p], gmat[p]], axis=0), m_scr[sqs[p], p % pairs]) for p in P]
    for p in P:
        m_scr[sqs[p], p % pairs] = (ym[p][L:L + LANES]
                                    + jnp.where(same_head, gh[p][:, LANES:] + hk[p], 0.0))
    for p in P:
        sl = sls[p]
        y = ym[p][0:L] + qy[p][:, LANES:] + yk[p]
        mean = seg_sum(y) * (1.0 / HD_B)
        dev = y - mean
        var = seg_sum(dev * dev) * (1.0 / HD_B)
        yn = dev * lax.rsqrt(var + LNX_EPS) * lnw_ref[:, sl] + lnb_ref[:, sl]
        bonus = seg_sum(d[p]["r"] * d[p]["k"] * rk_ref[:, sl]) * d[p]["v"]
        o_ref[sqs[p], :, sl] = ((yn + bonus) * g_ref[sqs[p], :, sl]).astype(BF16)

    @pl.when(c == nc - 1)
    def _():
        mout_ref[...] = m_scr[...]


def rwkv_chunk(acts, r_k, lnx_w, lnx_b, m0, batch, seq, chunk, pairs, seqs):
    nc = seq // chunk
    npair = C_B // LANES
    w = pairs * LANES
    act = pl.BlockSpec((seqs, chunk, w), lambda b, pg, c: (b, c, pg))
    prm = pl.BlockSpec((1, w), lambda b, pg, c: (0, pg))
    st = pl.BlockSpec((seqs, pairs, LANES, LANES), lambda b, pg, c: (b, pg, 0, 0))
    o, m = pl.pallas_call(
        functools.partial(_rwkv_chunk_kernel, chunk=chunk, pairs=pairs, seqs=seqs),
        grid=(batch // seqs, npair // pairs, nc),
        in_specs=[act] * 7 + [prm] * 3 + [st],
        out_specs=[act, st],
        out_shape=[jax.ShapeDtypeStruct((batch, seq, C_B), BF16),
                   jax.ShapeDtypeStruct((batch, npair, LANES, LANES), F32)],
        scratch_shapes=[pltpu.VMEM((seqs, pairs, LANES, LANES), F32)],
        compiler_params=_cparams("parallel", "parallel", "arbitrary"),
        name="rwkv_chunk",
    )(*(t.reshape(batch, seq, C_B) for t in acts),
      r_k.reshape(1, -1), lnx_w.reshape(1, -1), lnx_b.reshape(1, -1), m0)
    return o.reshape(batch * seq, C_B), m


BAND_Q = 256
BAND_SEGS = BAND // BAND_Q + 1
BAND_PAIRS = 4


def _band_prompt_kernel(q_ref, k_ref, v_ref, u_ref, o_ref, bias_ref):
    qb = pl.program_id(2)
    t = BAND_Q
    span = BAND_SEGS * t

    @pl.when(jnp.logical_and(pl.program_id(1) == 0, qb == 0))
    def _():
        j = lax.broadcasted_iota(jnp.int32, (span, t), 0)
        lo = (lax.broadcasted_iota(jnp.int32, (span, t), 1) // CHUNK) * CHUNK
        allowed = jnp.logical_and(j >= lo, j < lo + BAND + CHUNK)
        for h in range(2 * BAND_PAIRS):
            rows = jnp.broadcast_to(u_ref[h], (span, u_ref.shape[2]))
            sheared = pltpu.roll(rows, u_ref.shape[2] - span, 1, stride=1, stride_axis=0)
            bias_ref[h] = jnp.where(allowed, sheared[:, :t], NEG_BIG)

    lane = lax.broadcasted_iota(jnp.int32, (t, LANES), 1)
    low = lane < HD_C
    row_low = lax.broadcasted_iota(jnp.int32, (LANES, t), 0) < HD_C
    pairs = range(BAND_PAIRS)
    heads = range(2 * BAND_PAIRS)
    segs = range(BAND_SEGS)
    sls = [slice(pr * LANES, (pr + 1) * LANES) for pr in pairs]
    starts, valid = [], []
    for seg in segs:
        kb = qb - (BAND_SEGS - 1) + seg
        starts.append(pl.multiple_of(jnp.maximum(kb, 0) * t, t))
        valid.append(kb >= 0)
    ks = [[k_ref[pl.ds(starts[seg], t), sl].astype(BF16) for seg in segs] for sl in sls]
    vt = [[v_ref[pl.ds(starts[seg], t), sl].T.astype(BF16) for seg in segs] for sl in sls]
    q = [q_ref[:, sl] * BAND_SCALE for sl in sls]
    qh = [jnp.where(low if h % 2 == 0 else jnp.logical_not(low), q[h // 2], 0.0).astype(BF16) for h in heads]
    s = [[jnp.where(valid[seg],
                    lax.dot_general(ks[h // 2][seg], qh[h], (((1,), (1,)), ((), ())),
                                    preferred_element_type=F32)
                    + bias_ref[h, seg * t:(seg + 1) * t, :], NEG_BIG)
          for seg in segs] for h in heads]
    m = [functools.reduce(jnp.maximum, [jnp.max(s[h][seg], axis=0, keepdims=True) for seg in segs])
         for h in heads]
    p = [[jnp.exp2(s[h][seg] - m[h]) for seg in segs] for h in heads]
    l = [sum(jnp.sum(p[h][seg], axis=0, keepdims=True) for seg in segs) for h in heads]
    acc = [sum(jnp.dot(vt[h // 2][seg], p[h][seg].astype(BF16), preferred_element_type=F32) for seg in segs)
           for h in heads]
    for pr in pairs:
        o_ref[:, sls[pr]] = jnp.where(row_low, acc[2 * pr] / l[2 * pr],
                                      acc[2 * pr + 1] / l[2 * pr + 1]).T.astype(BF16)


def band_prompt(proj, table, batch, seq):
    t = BAND_Q
    span = BAND_SEGS * t
    nq = seq // t
    w = BAND_PAIRS * LANES
    ngrp = C_C // w
    dist = jnp.arange(t + span) - span + BAND
    u = (table[:, jnp.clip(dist, -REL_CLIP, REL_CLIP) + REL_CLIP] * LOG2E)[:, None, :]
    return pl.pallas_call(
        _band_prompt_kernel,
        grid=(ngrp, batch, nq),
        in_specs=[pl.BlockSpec((t, w), lambda hp, b, i: (b * nq + i, hp)),
                  pl.BlockSpec((seq, w), lambda hp, b, i: (b, ngrp + hp)),
                  pl.BlockSpec((seq, w), lambda hp, b, i: (b, 2 * ngrp + hp)),
                  pl.BlockSpec((2 * BAND_PAIRS, 1, t + span), lambda hp, b, i: (hp, 0, 0))],
        out_specs=pl.BlockSpec((t, w), lambda hp, b, i: (b * nq + i, hp)),
        out_shape=jax.ShapeDtypeStruct((batch * seq, C_C), BF16),
        scratch_shapes=[pltpu.VMEM((2 * BAND_PAIRS, span, t), F32)],
        compiler_params=_cparams("parallel", "arbitrary", "arbitrary"),
        name="band_prompt",
    )(proj, proj, proj, u)


def _band_sample_kernel(q_ref, kn_ref, vn_ref, kp_ref, vp_ref, biasp_ref, biasn_ref, o_ref):
    t = q_ref.shape[0]
    lane = lax.broadcasted_iota(jnp.int32, (t, LANES), 1)
    low = lane < HD_C
    pairs = range(C_C // LANES)
    heads = range(H_C)
    sls = [slice(hp * LANES, (hp + 1) * LANES) for hp in pairs]
    kp = [kp_ref[0, :, sl].astype(BF16) for sl in sls]
    vp = [vp_ref[0, :, sl].astype(BF16) for sl in sls]
    kn = [kn_ref[:, sl].astype(BF16) for sl in sls]
    vn = [vn_ref[:, sl].astype(BF16) for sl in sls]
    q = [q_ref[:, sl] * BAND_SCALE for sl in sls]
    qh = [jnp.where(low if h % 2 == 0 else jnp.logical_not(low), q[h // 2], 0.0).astype(BF16) for h in heads]
    s_p = [_dot_nt(qh[h], kp[h // 2]) + biasp_ref[h] for h in heads]
    s_n = [_dot_nt(qh[h], kn[h // 2]) + biasn_ref[h] for h in heads]
    m = [jnp.maximum(jnp.max(s_p[h], axis=-1, keepdims=True), jnp.max(s_n[h], axis=-1, keepdims=True))
         for h in heads]
    p_p = [jnp.exp2(s_p[h] - m[h]) for h in heads]
    p_n = [jnp.exp2(s_n[h] - m[h]) for h in heads]
    l = [jnp.sum(p_p[h], axis=-1, keepdims=True) + jnp.sum(p_n[h], axis=-1, keepdims=True) for h in heads]
    o = [(_dot(p_p[h], vp[h // 2]) + _dot(p_n[h], vn[h // 2])) / l[h] for h in heads]
    for hp in pairs:
        o_ref[:, sls[hp]] = jnp.where(low, o[2 * hp], o[2 * hp + 1]).astype(BF16)


def band_sample(proj, k_past, v_past, o, bias_p, bias_n, batch, seq):
    past = k_past.shape[2]
    new = lambda col: pl.BlockSpec((seq, C_C), lambda b: (b, col))
    old = pl.BlockSpec((None, 1, past, C_C), lambda b: (o, b, 0, 0))
    return pl.pallas_call(
        _band_sample_kernel,
        grid=(batch,),
        in_specs=[new(0), new(1), new(2), old, old,
                  _resident((H_C, seq, past), lambda b: (0, 0, 0)),
                  _resident((H_C, seq, seq), lambda b: (0, 0, 0))],
        out_specs=pl.BlockSpec((seq, C_C), lambda b: (b, 0)),
        out_shape=jax.ShapeDtypeStruct((batch * seq, C_C), BF16),
        compiler_params=_cparams("parallel"),
        name="band_sample",
    )(proj, proj, proj, k_past, v_past, bias_p, bias_n)


def _gla_kernel(q_ref, k_ref, v_ref, go_ref, gk_ref, wgk_ref, bgk_ref, gn_ref, s0_ref,
                o_ref, sout_ref, s_scr, *, chunk, seqs):
    c = pl.program_id(1)
    nc = pl.num_programs(1)
    L = chunk

    @pl.when(c == 0)
    def _():
        s_scr[...] = s0_ref[...]

    causal = (lax.broadcasted_iota(jnp.int32, (L, L), 0) >= lax.broadcasted_iota(jnp.int32, (L, L), 1))
    diag = (lax.broadcasted_iota(jnp.int32, (DK_D, DK_D), 0)
            == lax.broadcasted_iota(jnp.int32, (DK_D, DK_D), 1))
    S = range(seqs)
    P = [(sq, h) for sq in S for h in range(H_D)]
    ksl = [slice(h * DK_D, (h + 1) * DK_D) for h in range(H_D)]
    vsl = [slice(h * DV_D, (h + 1) * DV_D) for h in range(H_D)]
    b = [_cumsum_rows(-_softplus(-(_dot(gk_ref[sq], wgk_ref[...]) + bgk_ref[...])) / GATE_NORM) for sq in S]
    eb = [jnp.exp(x) for x in b]
    enb = [jnp.exp(-x) for x in b]
    ed = [jnp.exp(x[L - 1:L, :] - x) for x in b]
    e_end = [jnp.exp(x[L - 1:L, :]) for x in b]
    qe = [q_ref[sq, :, ksl[h]] * DK_D ** -0.5 * eb[sq][:, ksl[h]] for sq, h in P]
    vh = [v_ref[sq, :, vsl[h]].astype(BF16) for sq, h in P]
    att = [jnp.where(causal, _dot_nt(qe[i], k_ref[sq, :, ksl[h]] * enb[sq][:, ksl[h]]), 0.0)
           for i, (sq, h) in enumerate(P)]
    s_prev = [s_scr[sq, h] for sq, h in P]
    o = [_dot(att[i], vh[i]) + _dot(qe[i], s_prev[i]) for i in range(len(P))]
    kv = [_dot_tn(k_ref[sq, :, ksl[h]] * ed[sq][:, ksl[h]], vh[i]) for i, (sq, h) in enumerate(P)]
    for i, (sq, h) in enumerate(P):
        e_col = jnp.sum(jnp.where(diag, jnp.broadcast_to(e_end[sq][:, ksl[h]], (DK_D, DK_D)), 0.0),
                        axis=1, keepdims=True)
        s_scr[sq, h] = s_prev[i] * e_col + kv[i]
    for i, (sq, h) in enumerate(P):
        on = o[i] * lax.rsqrt(jnp.mean(o[i] * o[i], axis=-1, keepdims=True) + EPS) * gn_ref[...]
        gate = go_ref[sq, :, vsl[h]]
        o_ref[sq, :, vsl[h]] = (on * (gate * _sigmoid(gate))).astype(BF16)

    @pl.when(c == nc - 1)
    def _():
        sout_ref[...] = s_scr[...]


COL_QD = 3 * C_C
COL_KD = COL_QD + H_D * DK_D
COL_VD = COL_KD + H_D * DK_D
COL_GO = COL_VD + H_D * DV_D
COL_GK = COL_GO + H_D * DV_D
COLS_ODD_EXT = COL_GK + LANES


def gla(proj, wgk, bgk, gnorm, s0, batch, seq, chunk, seqs):
    nc = seq // chunk
    kw = H_D * DK_D
    vw = H_D * DV_D
    proj3 = proj.reshape(batch, seq, proj.shape[1])
    act = lambda w, col: pl.BlockSpec((seqs, chunk, w), lambda b, c: (b, c, col))
    st = pl.BlockSpec((seqs, H_D, DK_D, DV_D), lambda b, c: (b, 0, 0, 0))
    o, s = pl.pallas_call(
        functools.partial(_gla_kernel, chunk=chunk, seqs=seqs),
        grid=(batch // seqs, nc),
        in_specs=[act(kw, COL_QD // kw), act(kw, COL_KD // kw), act(vw, COL_VD // vw), act(vw, COL_GO // vw),
                  act(LANES, COL_GK // LANES),
                  _resident((LANES, kw), lambda b, c: (0, 0)),
                  pl.BlockSpec((1, kw), lambda b, c: (0, 0)),
                  pl.BlockSpec((1, DV_D), lambda b, c: (0, 0)),
                  st],
        out_specs=[act(vw, 0), st],
        out_shape=[jax.ShapeDtypeStruct((batch, seq, vw), BF16),
                   jax.ShapeDtypeStruct((batch, H_D, DK_D, DV_D), F32)],
        scratch_shapes=[pltpu.VMEM((seqs, H_D, DK_D, DV_D), F32)],
        compiler_params=_cparams("parallel", "arbitrary"),
        name="gla",
    )(proj3, proj3, proj3, proj3, proj3, wgk, bgk.reshape(1, -1), gnorm.reshape(1, -1), s0)
    return o.reshape(batch * seq, vw), s


FFN_TM = 512
FFN_TF = 512
HALO = 16


def _conv3(u_scr, cw_ref, cb_ref, cols, tm, prev1=None, prev2=None):
    u0 = u_scr[HALO:HALO + tm, cols]
    u1 = u_scr[HALO - 1:HALO - 1 + tm, cols]
    u2 = u_scr[HALO - 2:HALO - 2 + tm, cols]
    if prev1 is not None:
        u1 = prev1(u1)
        u2 = prev2(u2)
    return cb_ref[:, cols] + cw_ref[2:3, cols] * u0 + cw_ref[1:2, cols] * u1 + cw_ref[0:1, cols] * u2


def _ffn_prompt_kernel(x_ref, a1_ref, a2_ref, xh_ref, a1h_ref, a2h_ref, w1_ref, w2_ref, g_ref, gf_ref,
                       wg_ref, wv_ref, cwg_ref, cwv_ref, cbg_ref, cbv_ref, wd_ref,
                       o_ref, tg_ref, tv_ref, h_scr, *u_scrs, tm, blocks_per_seq, final_norm_out):
    i = pl.program_id(0)
    j = pl.program_id(1)
    nj = pl.num_programs(1)

    @pl.when(j == 0)
    def _():
        w1 = w1_ref[...]
        w2 = w2_ref[...]
        xm = x_ref[...] + _dot(a1_ref[...], w1) + _dot(a2_ref[...], w2)
        o_ref[...] = xm
        h_scr[HALO:HALO + tm, :] = _rms(xm, g_ref[...]).astype(BF16)
        keep = (i % blocks_per_seq != 0).astype(F32)
        xmh = xh_ref[...] + _dot(a1h_ref[...], w1) + _dot(a2h_ref[...], w2)
        h_scr[0:HALO, :] = (_rms(xmh, g_ref[...]) * keep).astype(BF16)

    h = h_scr[...]
    ug_scr, uv_scr = u_scrs
    ug_scr[...] = jnp.dot(h, wg_ref[...], preferred_element_type=F32)
    uv_scr[...] = jnp.dot(h, wv_ref[...], preferred_element_type=F32)
    acts = []
    for c in range(0, ug_scr.shape[1], LANES):
        lanes = slice(c, c + LANES)
        cg = _conv3(ug_scr, cwg_ref, cbg_ref, lanes, tm)
        cv = _conv3(uv_scr, cwv_ref, cbv_ref, lanes, tm)
        acts.append((cg * _sigmoid(cg) * cv).astype(BF16))
    o_ref[...] += jnp.dot(jnp.concatenate(acts, axis=1), wd_ref[...], preferred_element_type=F32)
    tg_ref[0] = ug_scr[HALO + tm - 8:HALO + tm, :]
    tv_ref[0] = uv_scr[HALO + tm - 8:HALO + tm, :]

    if final_norm_out:
        @pl.when(j == nj - 1)
        def _():
            o_ref[...] = _rms(o_ref[...], gf_ref[...])


def ffn_prompt(x, a1, a2, w_out, e, g, w_up, w_down, layer, conv_w, conv_b, seq, final_g=None):
    m, k = x.shape
    ka = a1.shape[1]
    tm, tf = FFN_TM, FFN_TF
    nj = D_FF // tf
    nblk = m // tm
    bps = seq // tm
    halo_blocks = tm // HALO
    cb = conv_b.reshape(1, -1)
    gf = (g if final_g is None else final_g).reshape(1, k)
    before = lambda i, j: (jnp.maximum(i * halo_blocks - 1, 0), 0)
    return pl.pallas_call(
        functools.partial(_ffn_prompt_kernel, tm=tm, blocks_per_seq=bps, final_norm_out=final_g is not None),
        grid=(nblk, nj),
        in_specs=[pl.BlockSpec((tm, k), lambda i, j: (i, 0)),
                  pl.BlockSpec((tm, ka), lambda i, j: (i, 0)),
                  pl.BlockSpec((tm, ka), lambda i, j: (i, 0)),
                  pl.BlockSpec((HALO, k), before),
                  pl.BlockSpec((HALO, ka), before),
                  pl.BlockSpec((HALO, ka), before),
                  _resident((None, ka, k), lambda i, j: (e, 0, 0)),
                  _resident((None, ka, k), lambda i, j: (e, 1, 0)),
                  pl.BlockSpec((1, k), lambda i, j: (0, 0)),
                  pl.BlockSpec((1, k), lambda i, j: (0, 0)),
                  pl.BlockSpec((None, k, tf), lambda i, j: (layer, 0, j)),
                  pl.BlockSpec((None, k, tf), lambda i, j: (layer, 0, nj + j)),
                  pl.BlockSpec((CONV_W, tf), lambda i, j: (0, j)),
                  pl.BlockSpec((CONV_W, tf), lambda i, j: (0, nj + j)),
                  pl.BlockSpec((1, tf), lambda i, j: (0, j)),
                  pl.BlockSpec((1, tf), lambda i, j: (0, nj + j)),
                  pl.BlockSpec((None, tf, k), lambda i, j: (layer, j, 0))],
        out_specs=[pl.BlockSpec((tm, k), lambda i, j: (i, 0)),
                   pl.BlockSpec((1, 8, tf), lambda i, j: (i, 0, j)),
                   pl.BlockSpec((1, 8, tf), lambda i, j: (i, 0, j))],
        out_shape=[jax.ShapeDtypeStruct((m, k), F32),
                   jax.ShapeDtypeStruct((nblk, 8, D_FF), F32),
                   jax.ShapeDtypeStruct((nblk, 8, D_FF), F32)],
        scratch_shapes=[pltpu.VMEM((tm + HALO, k), BF16)] + [pltpu.VMEM((tm + HALO, tf), F32)] * 2,
        compiler_params=_cparams("parallel", "arbitrary"),
        name="ffn_prompt",
    )(x, a1, a2, x, a1, a2, w_out, w_out, g.reshape(1, k), gf, w_up, w_up, conv_w, conv_w, cb, cb, w_down)


def _ffn_sample_kernel(x_ref, g_ref, gf_ref, wg_ref, wv_ref, cwg_ref, cwv_ref, cbg_ref, cbv_ref, wd_ref,
                       sg_ref, sv_ref,
                       o_ref, ug_ref, uv_ref, h_scr, acc_scr, ug_scr, uv_scr, *, tm, seq, final_norm_out):
    j = pl.program_id(0)
    nj = pl.num_programs(0)
    nseq = tm // seq
    tf = ug_scr.shape[1]
    pos = lax.broadcasted_iota(jnp.int32, (nseq, seq, 1), 1)
    cols = slice(0, tf)

    def before(s_ref, back):
        def fix(u):
            u = u.reshape(nseq, seq, tf)
            for p in range(back):
                u = jnp.where(pos == p, s_ref[:, CONV_W - 1 - back + p:CONV_W - back + p, :], u)
            return u.reshape(tm, tf)
        return fix

    @pl.when(j == 0)
    def _():
        h_scr[...] = _rms(x_ref[...], g_ref[...]).astype(BF16)
        acc_scr[...] = jnp.zeros_like(acc_scr)
        ug_scr[0:HALO, :] = jnp.zeros((HALO, ug_scr.shape[1]), F32)
        uv_scr[0:HALO, :] = jnp.zeros((HALO, uv_scr.shape[1]), F32)

    h = h_scr[...]
    ug = jnp.dot(h, wg_ref[...], preferred_element_type=F32)
    uv = jnp.dot(h, wv_ref[...], preferred_element_type=F32)
    ug_scr[HALO:HALO + tm, :] = ug
    uv_scr[HALO:HALO + tm, :] = uv
    ug_ref[...] = ug
    uv_ref[...] = uv
    cg = _conv3(ug_scr, cwg_ref, cbg_ref, cols, tm, before(sg_ref, 1), before(sg_ref, 2))
    cv = _conv3(uv_scr, cwv_ref, cbv_ref, cols, tm, before(sv_ref, 1), before(sv_ref, 2))
    act = cg * _sigmoid(cg) * cv
    acc_scr[...] += _dot(act, wd_ref[...])

    @pl.when(j == nj - 1)
    def _():
        out = x_ref[...] + acc_scr[...]
        o_ref[...] = _rms(out, gf_ref[...]) if final_norm_out else out


def ffn_sample(x, g, w_up, w_down, layer, conv_w, conv_b, state, seq, final_g=None):
    m, k = x.shape
    tm, tf = m, FFN_TF
    nj = D_FF // tf
    nseq = m // seq
    cb = conv_b.reshape(1, -1)
    gf = (g if final_g is None else final_g).reshape(1, k)
    gate = lambda shape: pl.BlockSpec(shape, lambda j: (0, j))
    val = lambda shape: pl.BlockSpec(shape, lambda j: (0, nj + j))
    carried = lambda off: pl.BlockSpec((nseq, CONV_W - 1, tf), lambda j: (0, 0, off + j))
    return pl.pallas_call(
        functools.partial(_ffn_sample_kernel, tm=tm, seq=seq, final_norm_out=final_g is not None),
        grid=(nj,),
        in_specs=[pl.BlockSpec((tm, k), lambda j: (0, 0)),
                  pl.BlockSpec((1, k), lambda j: (0, 0)),
                  pl.BlockSpec((1, k), lambda j: (0, 0)),
                  pl.BlockSpec((None, k, tf), lambda j: (layer, 0, j)),
                  pl.BlockSpec((None, k, tf), lambda j: (layer, 0, nj + j)),
                  gate((CONV_W, tf)), val((CONV_W, tf)),
                  gate((1, tf)), val((1, tf)),
                  pl.BlockSpec((None, tf, k), lambda j: (layer, j, 0)),
                  carried(0), carried(nj)],
        out_specs=[pl.BlockSpec((tm, k), lambda j: (0, 0)), gate((tm, tf)), gate((tm, tf))],
        out_shape=[jax.ShapeDtypeStruct((m, k), F32),
                   jax.ShapeDtypeStruct((m, D_FF), F32),
                   jax.ShapeDtypeStruct((m, D_FF), F32)],
        scratch_shapes=[pltpu.VMEM((tm, k), BF16), pltpu.VMEM((tm, k), F32),
                        pltpu.VMEM((tm + HALO, tf), F32), pltpu.VMEM((tm + HALO, tf), F32)],
        compiler_params=_cparams("arbitrary"),
        name="ffn_sample",
    )(x, g.reshape(1, k), gf, w_up, w_up, conv_w, conv_w, cb, cb, w_down, state, state)


def _rot_cols(w):
    half = QK_ROPE // 2
    return jnp.concatenate([-w[..., half:], w[..., :half]], axis=-1)


def _rope_tables(pos):
    half = QK_ROPE // 2
    inv = ROPE_THETA ** (-jnp.arange(half, dtype=F32) / half)
    ang = pos.astype(F32)[:, None] * inv[None, :]
    cos = jnp.cos(ang)
    sin = jnp.sin(ang)
    zeros = jnp.zeros((pos.shape[0], LANES - QK_ROPE), F32)
    return (jnp.concatenate([cos, cos, zeros], axis=1), jnp.concatenate([sin, sin, zeros], axis=1))


def _pair_state_in(s):
    b = s.shape[0]
    m = jnp.swapaxes(s, -1, -2).reshape(b, H_B // 2, 2, HD_B, HD_B)
    z = jnp.zeros_like(m[:, :, 0])
    top = jnp.concatenate([m[:, :, 0], z], axis=-1)
    bot = jnp.concatenate([z, m[:, :, 1]], axis=-1)
    return jnp.concatenate([top, bot], axis=-2)


def _pair_state_out(m):
    b = m.shape[0]
    h0 = m[:, :, :HD_B, :HD_B]
    h1 = m[:, :, HD_B:, HD_B:]
    s = jnp.stack([h0, h1], axis=2).reshape(b, H_B, HD_B, HD_B)
    return jnp.swapaxes(s, -1, -2)


def _band_bias_sample(table, seq, past):
    q_pos = PAST_LEN + jnp.arange(seq)
    k_pos = PAST_LEN - past + jnp.arange(past + seq)
    idx = jnp.clip(q_pos[:, None] - k_pos[None, :], -REL_CLIP, REL_CLIP) + REL_CLIP
    bias = table[:, idx].astype(F32) * LOG2E
    return bias[:, :, :past], bias[:, :, past:]


def _last_rows(proj, nseq, seq):
    last = proj.reshape(nseq, seq, -1)[:, -1]
    return jnp.concatenate([last[:, :3 * C_B], last[:, COL_WA:]], axis=1)


def _even_layer(xp, xs, e, rope_p, rope_s, dims, caches, prm):
    bp, tp, bs, ts = dims
    cache_ckv, cache_kpe, state_rwkv, state_shift = caches
    w_in = prm["ev_w_in"][e]
    wa_cols = w_in[:, :Q_LORA + KV_LORA]
    w_kpe = w_in[:, Q_LORA + KV_LORA:Q_LORA + KV_LORA + QK_ROPE]
    w_b = w_in[:, Q_LORA + KV_LORA + QK_ROPE:]
    w_ext = jnp.concatenate([w_b[:, :3 * C_B], wa_cols, w_kpe, _rot_cols(w_kpe), w_b[:, 3 * C_B:]],
                            axis=1).astype(BF16)
    g_mix = prm["norm_mix"][2 * e]
    proj_p = norm_matmul(xp, g_mix, w_ext)
    proj_s = norm_matmul(xs, g_mix, w_ext)

    wq = prm["ev_w_uq"][e].reshape(Q_LORA, H_A, QK_NOPE + QK_ROPE)
    wq_pe = wq[:, :, QK_NOPE:]
    wq_ext = jnp.concatenate([wq[:, :, :QK_NOPE], wq_pe, _rot_cols(wq_pe)], axis=-1)
    wq_ext = wq_ext.reshape(Q_LORA, H_A * HEAD_W).astype(BF16)
    wkv = prm["ev_w_ukv"][e].astype(BF16)
    qn, kvn = prm["ev_q_norm"][e], prm["ev_kv_norm"][e]
    q_p, ckv_p, kpe_p, kx_p, vt_p = mla_prep(proj_p, *rope_p, qn, kvn, wq_ext, wkv, True)
    q_s, ckv_s, kpe_s = mla_prep(proj_s, *rope_s, qn, kvn, wq_ext, wkv, False)
    o_a_p = mla_prompt(q_p, kx_p, vt_p, bp, tp)
    o_a_s = mla_sample(q_s, cache_ckv, cache_kpe, e, ckv_s, kpe_s, wkv, bs, ts)

    mu = prm["ev_mu"][e]
    mu_parts = [mu[None, 0:C_B], mu[None, C_B:2 * C_B], mu[None, 2 * C_B:3 * C_B],
                mu[None, 3 * C_B:3 * C_B + LANES], mu[None, 3 * C_B + LANES:]]
    zw = jnp.zeros((W_LORA, C_B), F32)
    ww2p = jnp.concatenate([prm["ev_w_w2"][e], zw], axis=0).astype(BF16)
    wa2p = jnp.concatenate([zw, prm["ev_w_a2"][e]], axis=0).astype(BF16)
    prep_w = (mu_parts, prm["ev_w0"][e], ww2p, prm["ev_a0"][e], wa2p, prm["ev_w_g2"][e].astype(BF16),
              prm["ev_k_k"][e], prm["ev_k_a"][e])
    acts_p = rwkv_prep(proj_p, None, *prep_w, tp)
    acts_s = rwkv_prep(proj_s, jnp.repeat(state_shift[e], ts, axis=0), *prep_w, ts)
    r_k = prm["ev_r_k"][e].reshape(-1)
    lnw, lnb = prm["ev_lnx_w"][e], prm["ev_lnx_b"][e]
    m0_p = jnp.zeros((bp, H_B // 2, LANES, LANES), F32)
    o_b_p, m_p = rwkv_chunk(acts_p, r_k, lnw, lnb, m0_p, bp, tp, min(CHUNK, tp), RWKV_PAIRS, bp)
    o_b_s, m_s = rwkv_chunk(acts_s, r_k, lnw, lnb, _pair_state_in(state_rwkv[e]), bs, ts,
                            min(CHUNK, ts), RWKV_PAIRS, 2)

    xs = out_proj(xs, o_a_s, o_b_s, prm["ev_w_out16"], e)
    outs = dict(
        ckv_p=ckv_p.reshape(bp, tp, KV_LORA), ckv_s=ckv_s.reshape(bs, ts, KV_LORA),
        kpe_p=kpe_p.reshape(bp, tp, QK_ROPE), kpe_s=kpe_s.reshape(bs, ts, QK_ROPE),
        rw_p=_pair_state_out(m_p), rw_s=_pair_state_out(m_s),
        sh_p=_last_rows(proj_p, bp, tp), sh_s=_last_rows(proj_s, bs, ts))
    return (o_a_p, o_b_p, prm["ev_w_out16"], e), xs, outs


def _odd_layer(xp, xs, o, dims, caches, prm):
    bp, tp, bs, ts = dims
    cache_k, cache_v, state_gla = caches
    band_past = cache_k.shape[2]
    w_in = prm["od_w_in"][o]
    c_gk = 3 * C_C + 2 * H_D * DK_D + H_D * DV_D
    w_ext = jnp.concatenate([w_in[:, :c_gk], w_in[:, c_gk + GK_LORA:], w_in[:, c_gk:c_gk + GK_LORA],
                             jnp.zeros((D_MODEL, LANES - GK_LORA), F32)], axis=1).astype(BF16)
    g_mix = prm["norm_mix"][2 * o + 1]
    proj_p = norm_matmul(xp, g_mix, w_ext)
    proj_s = norm_matmul(xs, g_mix, w_ext)

    table = prm["od_rel_bias"][o]
    o_c_p = band_prompt(proj_p, table, bp, tp)
    bias_p, bias_n = _band_bias_sample(table, ts, band_past)
    o_c_s = band_sample(proj_s, cache_k, cache_v, o, bias_p, bias_n, bs, ts)

    wgk = jnp.concatenate([prm["od_w_gk2"][o], jnp.zeros((LANES - GK_LORA, H_D * DK_D), F32)],
                          axis=0).astype(BF16)
    bgk, gn = prm["od_b_gk"][o], prm["od_gnorm"][o]
    s0_p = jnp.zeros((bp, H_D, DK_D, DV_D), F32)
    o_d_p, s_p = gla(proj_p, wgk, bgk, gn, s0_p, bp, tp, min(CHUNK, tp), bp)
    o_d_s, s_s = gla(proj_s, wgk, bgk, gn, state_gla[o], bs, ts, min(CHUNK, ts), 2)

    xs = out_proj(xs, o_c_s, o_d_s, prm["od_w_out16"], o)

    def tail_prompt(col):
        rows = proj_p.reshape(bp, tp, -1)[:, max(tp - band_past, 0):, col * C_C:(col + 1) * C_C]
        rows = rows.reshape(bp, -1, H_C, HD_C)
        return jnp.pad(rows, ((0, 0), (max(band_past - tp, 0), 0), (0, 0), (0, 0)))

    new_rows = lambda col: proj_s[:, col * C_C:(col + 1) * C_C].reshape(bs, ts, H_C, HD_C)
    outs = dict(bk_p=tail_prompt(1), bv_p=tail_prompt(2), bk_new=new_rows(1), bv_new=new_rows(2),
                gla_p=s_p, gla_s=s_s)
    return (o_c_p, o_d_p, prm["od_w_out16"], o), xs, outs


def _ffn_layer(xp, mix_p, xs, layer, dims, state_conv, prm, final_g):
    bp, tp, bs, ts = dims
    g = prm["norm_ffn"][layer]
    w_up, w_down = prm["ffn_w_up16"], prm["ffn_w_down16"]
    cw, cb = prm["ffn_conv_w"][layer], prm["ffn_conv_b"][layer]
    a1, a2, w_out, e = mix_p
    xp, tg, tv = ffn_prompt(xp, a1, a2, w_out, e, g, w_up, w_down, layer, cw, cb, tp, final_g)
    bps = tp // FFN_TM
    tails = jnp.concatenate([tg.reshape(bp, bps, 8, D_FF)[:, -1, 8 - (CONV_W - 1):],
                             tv.reshape(bp, bps, 8, D_FF)[:, -1, 8 - (CONV_W - 1):]], axis=-1)

    st = state_conv[layer]
    xs, ug, uv = ffn_sample(xs, g, w_up, w_down, layer, cw, cb, st, ts, final_g)
    keep = CONV_W - 1
    new = jnp.concatenate([ug.reshape(bs, ts, D_FF)[:, max(ts - keep, 0):],
                           uv.reshape(bs, ts, D_FF)[:, max(ts - keep, 0):]], axis=-1)
    ext = jnp.concatenate([st[:, ts:], new], axis=1)
    return xp, xs, tails, ext


@jax.jit
def kernel(x_prompt, x_sample, cache_mla_ckv, cache_mla_kpe, state_rwkv, state_rwkv_shift, cache_band_k, cache_band_v, state_gla, state_ffn_conv, norm_mix, norm_ffn, norm_final, ev_w_in, ev_q_norm, ev_w_uq, ev_kv_norm, ev_w_ukv, ev_mu, ev_w0, ev_w_w2, ev_a0, ev_w_a2, ev_w_g2, ev_k_k, ev_k_a, ev_r_k, ev_lnx_w, ev_lnx_b, ev_w_out, od_w_in, od_rel_bias, od_w_gk2, od_b_gk, od_gnorm, od_w_out, ffn_w_up, ffn_conv_w, ffn_conv_b, ffn_w_down):
    prm = dict(norm_mix=norm_mix, norm_ffn=norm_ffn, ev_w_in=ev_w_in, ev_q_norm=ev_q_norm, ev_w_uq=ev_w_uq,
               ev_kv_norm=ev_kv_norm, ev_w_ukv=ev_w_ukv, ev_mu=ev_mu, ev_w0=ev_w0, ev_w_w2=ev_w_w2,
               ev_a0=ev_a0, ev_w_a2=ev_w_a2, ev_w_g2=ev_w_g2, ev_k_k=ev_k_k, ev_k_a=ev_k_a, ev_r_k=ev_r_k,
               ev_lnx_w=ev_lnx_w, ev_lnx_b=ev_lnx_b, ev_w_out=ev_w_out, od_w_in=od_w_in,
               od_rel_bias=od_rel_bias, od_w_gk2=od_w_gk2, od_b_gk=od_b_gk, od_gnorm=od_gnorm,
               od_w_out=od_w_out, ffn_w_up=ffn_w_up, ffn_conv_w=ffn_conv_w, ffn_conv_b=ffn_conv_b,
               ffn_w_down=ffn_w_down)
    bp, tp, _ = x_prompt.shape
    bs, ts, _ = x_sample.shape
    dims = (bp, tp, bs, ts)
    depth = norm_mix.shape[0]
    xp = x_prompt.reshape(bp * tp, D_MODEL)
    xs = x_sample.reshape(bs * ts, D_MODEL)
    rope_p = _rope_tables(jnp.tile(jnp.arange(tp), bp))
    rope_s = _rope_tables(jnp.tile(PAST_LEN + jnp.arange(ts), bs))

    prm.update(ffn_w_up16=ffn_w_up.astype(BF16), ffn_w_down16=ffn_w_down.astype(BF16),
               ev_w_out16=ev_w_out.astype(BF16), od_w_out16=od_w_out.astype(BF16))

    band_rows = cache_band_k.shape[:3] + (C_C,)
    band_caches = (cache_band_k.reshape(band_rows), cache_band_v.reshape(band_rows), state_gla)

    ev, od, ffn_p, ffn_s = [], [], [], []
    for layer in range(depth):
        if layer % 2 == 0:
            mix_p, xs, outs = _even_layer(xp, xs, layer // 2, rope_p, rope_s, dims,
                                          (cache_mla_ckv, cache_mla_kpe, state_rwkv, state_rwkv_shift), prm)
            ev.append(outs)
        else:
            mix_p, xs, outs = _odd_layer(xp, xs, layer // 2, dims, band_caches, prm)
            od.append(outs)
        final_g = norm_final if layer == depth - 1 else None
        xp, xs, tails, ext = _ffn_layer(xp, mix_p, xs, layer, dims, state_ffn_conv, prm, final_g)
        ffn_p.append(tails)
        ffn_s.append(ext)

    yp, ys = xp, xs
    stack = lambda lst, key: jnp.stack([d[key] for d in lst])
    bk_s = jnp.concatenate([cache_band_k[:, :, ts:], stack(od, "bk_new")], axis=2)
    bv_s = jnp.concatenate([cache_band_v[:, :, ts:], stack(od, "bv_new")], axis=2)
    return (yp.reshape(bp, tp, D_MODEL), ys.reshape(bs, ts, D_MODEL),
            stack(ev, "ckv_p"), stack(ev, "ckv_s"), stack(ev, "kpe_p"), stack(ev, "kpe_s"),
            stack(ev, "rw_p"), stack(ev, "rw_s"), stack(ev, "sh_p"), stack(ev, "sh_s"),
            stack(od, "bk_p"), bk_s, stack(od, "bv_p"), bv_s,
            stack(od, "gla_p"), stack(od, "gla_s"), jnp.stack(ffn_p), jnp.stack(ffn_s))
```
